```python
import jax
import jax.numpy as jnp
from jax import lax
import numpy as np

D_MODEL = 1024
BATCH = 4
SEQ = 8192
DEPTH = 2

GRID_W = 64
CTX_LEN = 256
N_GROUPS = 4
GROUP_W = D_MODEL // N_GROUPS
MIX_W = N_GROUPS * GROUP_W

GLA_HEADS = 4
GLA_DK = GROUP_W // (2 * GLA_HEADS)
GLA_DV = GROUP_W // GLA_HEADS
GLA_RANK = 16
GLA_GATE_NORMALIZER = 16.0
GLA_LOG_DECAY_MIN = -1.0
GLA_CHUNK = 64

POOL_WINDOWS = (2, 4, 8, 16)
POOL_CH = GROUP_W // len(POOL_WINDOWS)

SGU_CHUNK = 128
SGU_HEADS = 4
SGU_HD = GROUP_W // SGU_HEADS

CONV_WIDTH = 31
FFN_HIDDEN = 2816
FFN_CONV = 3

GLA_QK = GLA_HEADS * GLA_DK
COLS_A = 2 * GLA_QK + 2 * GROUP_W + 2 * GLA_RANK
COLS_B = GROUP_W
COLS_C = 2 * GROUP_W
COLS_D = 2 * GROUP_W
IN_COLS = COLS_A + COLS_B + COLS_C + COLS_D
MIX_SPLITS = (COLS_A, COLS_A + COLS_B, COLS_A + COLS_B + COLS_C)
GLA_SPLITS = (GLA_QK, 2 * GLA_QK, 2 * GLA_QK + GROUP_W, 2 * GLA_QK + 2 * GROUP_W,
              2 * GLA_QK + 2 * GROUP_W + GLA_RANK)

ALPHA = (2 * DEPTH) ** 0.25
BETA = (8 * DEPTH) ** -0.25
EPS = 1e-6

kernel_name = 'hybrid_parallel_mixer_dit_block'


def _norm(x):
    x32 = x.astype(jnp.float32)
    mu = jnp.mean(x32, axis=-1, keepdims=True)
    var = jnp.mean(jnp.square(x32 - mu), axis=-1, keepdims=True)
    return ((x32 - mu) * lax.rsqrt(var + EPS)).astype(x.dtype)


def layer_norm(x, w, b):
    return _norm(x) * w + b


def modulate(x, shift, scale):
    return _norm(x) * (1 + scale) + shift


def split_heads(t, h):
    b, l, _ = t.shape
    return t.reshape(b, l, h, -1).transpose(0, 2, 1, 3)


def merge_heads(t):
    b, h, l, d = t.shape
    return t.transpose(0, 2, 1, 3).reshape(b, l, h * d)


def _to_chunks(t):
    b, h, l, d = t.shape
    return t.reshape(b, h, l // GLA_CHUNK, GLA_CHUNK, d)


def gla_chunk_summaries(k, v, g):
    cum = jnp.cumsum(g, axis=3)
    cum_last = cum[:, :, :, -1, :]
    k_dec = k * jnp.exp(cum_last[:, :, :, None, :] - cum)
    kv = jnp.einsum('bhncd,bhnce->bhnde', k_dec, v)
    return cum, kv, jnp.exp(cum_last)


def gla_chunk_states(kv, decay, s0):
    def step(s, inp):
        kv_n, dec_n = inp
        return dec_n[..., None] * s + kv_n, s
    s_fin, s_start = lax.scan(step, s0, (jnp.moveaxis(kv, 2, 0), jnp.moveaxis(decay, 2, 0)))
    return jnp.moveaxis(s_start, 0, 2), s_fin


def gla_scan(q, k, v, g, s0):
    b, h, l, dv = v.shape
    q, k, v, g = (_to_chunks(t) for t in (q, k, v, g))
    cum, kv, decay = gla_chunk_summaries(k, v, g)
    s_start, s_fin = gla_chunk_states(kv, decay, s0)
    q_in = q * jnp.exp(cum)
    k_in = k * jnp.exp(-cum)
    tri = jnp.tril(jnp.ones((GLA_CHUNK, GLA_CHUNK), dtype=bool))
    a = jnp.where(tri, jnp.einsum('bhncd,bhnsd->bhncs', q_in, k_in), 0.0)
    o = jnp.einsum('bhncs,bhnse->bhnce', a, v) + jnp.einsum('bhncd,bhnde->bhnce', q_in, s_start)
    return o.reshape(b, h, l, dv), s_fin


def gla_final_state(k, v, g, s0):
    _, kv, decay = gla_chunk_summaries(_to_chunks(k), _to_chunks(v), _to_chunks(g))
    _, s_fin = gla_chunk_states(kv, decay, s0)
    return s_fin


def gla_log_decay(z_low, w_up, b_up):
    logit = z_low @ w_up + b_up
    return jnp.maximum(jax.nn.log_sigmoid(logit) / GLA_GATE_NORMALIZER, GLA_LOG_DECAY_MIN)


def gla_mixer(za_lat, za_ctx, w_gate, b_gate, norm_w, with_ctx_out):
    def prep(za):
        q, k, v, r, low_f, low_b = jnp.split(za.astype(jnp.float32), GLA_SPLITS, axis=-1)
        q = split_heads(q, GLA_HEADS) * GLA_DK ** -0.5
        k = split_heads(k, GLA_HEADS)
        v = split_heads(v, GLA_HEADS)
        g_f = split_heads(gla_log_decay(low_f, w_gate[0], b_gate[0]), GLA_HEADS)
        g_b = split_heads(gla_log_decay(low_b, w_gate[1], b_gate[1]), GLA_HEADS)
        return q, k, v, r, g_f, g_b

    def readout(o, r, dtype):
        o = o * lax.rsqrt(jnp.mean(jnp.square(o), axis=-1, keepdims=True) + EPS)
        return (merge_heads(o) * norm_w * jax.nn.silu(r)).astype(dtype)

    flip = lambda t: jnp.flip(t, axis=2)
    s0 = jnp.zeros((za_ctx.shape[0], GLA_HEADS, GLA_DK, GLA_DV), jnp.float32)
    qc, kc, vc, rc, gfc, gbc = prep(za_ctx)
    if with_ctx_out:
        oc_f, s_f = gla_scan(qc, kc, vc, gfc, s0)
        oc_b, s_b = gla_scan(flip(qc), flip(kc), flip(vc), flip(gbc), s0)
        y_ctx = readout(oc_f + flip(oc_b), rc, za_ctx.dtype)
    else:
        s_f = gla_final_state(kc, vc, gfc, s0)
        s_b = gla_final_state(flip(kc), flip(vc), flip(gbc), s0)
        y_ctx = None
    q, k, v, r, g_f, g_b = prep(za_lat)
    o_f, _ = gla_scan(q, k, v, g_f, s_f)
    o_b, _ = gla_scan(flip(q), flip(k), flip(v), flip(g_b), s_b)
    return readout(o_f + flip(o_b), r, za_lat.dtype), y_ctx


def pool_mixer(zb, pool_w, pool_scale):
    b, l, _ = zb.shape
    x32 = zb.astype(jnp.float32)
    cs = jnp.concatenate([jnp.zeros((b, 1, GROUP_W), jnp.float32), jnp.cumsum(x32, axis=1)], axis=1)
    t = jnp.arange(l)
    outs = []
    for gi, w in enumerate(POOL_WINDOWS):
        lo = jnp.clip(t - w // 2, 0, l)
        hi = jnp.clip(t + w - w // 2, 0, l)
        sl = slice(gi * POOL_CH, (gi + 1) * POOL_CH)
        mean = (cs[:, hi, sl] - cs[:, lo, sl]) / (hi - lo).astype(jnp.float32)[None, :, None]
        outs.append(mean - x32[:, :, sl])
    y = jnp.stack(outs, axis=2).astype(zb.dtype)
    y = jnp.einsum('blgc,gcd->blgd', y, pool_w).reshape(b, l, GROUP_W)
    return y * pool_scale


def sgu_mixer(zc, sgu_w, sgu_b, ln_w, ln_b):
    z = jax.nn.gelu(zc)
    u, v = jnp.split(z, 2, axis=-1)
    v = layer_norm(v, ln_w, ln_b)
    b, l, _ = v.shape
    vh = v.reshape(b, l // SGU_CHUNK, SGU_CHUNK, SGU_HEADS, SGU_HD)
    s = jnp.einsum('hts,bnshd->bnthd', sgu_w, vh) + sgu_b.T[:, :, None]
    return u * s.reshape(b, l, GROUP_W)


def conv_module(zd, conv_w, conv_b, ln_w, ln_b):
    a, g = jnp.split(zd, 2, axis=-1)
    y = a * jax.nn.sigmoid(g)
    y = lax.conv_general_dilated(y, conv_w[:, None, :], window_strides=(1,),
                                 padding=[(CONV_WIDTH // 2, CONV_WIDTH // 2)],
                                 dimension_numbers=('NWC', 'WIO', 'NWC'),
                                 feature_group_count=GROUP_W) + conv_b
    return jax.nn.silu(layer_norm(y, ln_w, ln_b))


def conv_ffn(h, w_up, conv_w, w_down, rows, width):
    b, l, _ = h.shape
    u = (h @ w_up).reshape(b, rows, width, 2 * FFN_HIDDEN)
    u = lax.conv_general_dilated(u, conv_w[:, :, None, :], window_strides=(1, 1), padding='SAME',
                                 dimension_numbers=('NHWC', 'HWIO', 'NHWC'),
                                 feature_group_count=2 * FFN_HIDDEN)
    a, g = jnp.split(u.reshape(b, l, 2 * FFN_HIDDEN), 2, axis=-1)
    return (jax.nn.silu(g) * a) @ w_down


def token_mixer(h_lat, h_ctx, w_in, gla_w_gate, gla_b_gate, gla_norm_w, pool_w, pool_scale,
                sgu_w, sgu_b, sgu_ln_w, sgu_ln_b, cm_conv_w, cm_conv_b, cm_ln_w, cm_ln_b, w_out,
                with_ctx_out):
    def other_groups(z):
        _, zb, zc, zd = jnp.split(z, MIX_SPLITS, axis=-1)
        return [pool_mixer(zb, pool_w, pool_scale),
                sgu_mixer(zc, sgu_w, sgu_b, sgu_ln_w, sgu_ln_b),
                conv_module(zd, cm_conv_w, cm_conv_b, cm_ln_w, cm_ln_b)]
    z_lat = h_lat @ w_in
    z_ctx = h_ctx @ (w_in if with_ctx_out else w_in[:, :COLS_A])
    ya_lat, ya_ctx = gla_mixer(z_lat[..., :COLS_A], z_ctx[..., :COLS_A],
                               gla_w_gate, gla_b_gate, gla_norm_w, with_ctx_out)
    y_lat = jnp.concatenate([ya_lat] + other_groups(z_lat), axis=-1) @ w_out
    y_ctx = (jnp.concatenate([ya_ctx] + other_groups(z_ctx), axis=-1) @ w_out) if with_ctx_out else None
    return y_lat, y_ctx


def setup_inputs(seed: int = 0) -> dict:
    key = jax.random.key(seed)
    ks = jax.random.split(key, 26)

    def nrm(k, shape, scale):
        return jax.random.normal(k, shape, jnp.float32) * scale

    return {
        'x': nrm(ks[0], (BATCH, SEQ, D_MODEL), 1.0),
        'c': nrm(ks[1], (BATCH, D_MODEL), 1.0),
        'ctx': nrm(ks[2], (BATCH, CTX_LEN, D_MODEL), 1.0),
        'c_ctx': nrm(ks[3], (D_MODEL,), 1.0),
        'w_mod': nrm(ks[4], (DEPTH, D_MODEL, 6 * D_MODEL), 0.5 * D_MODEL ** -0.5),
        'b_mod': nrm(ks[5], (DEPTH, 6 * D_MODEL), 0.02),
        'w_in': nrm(ks[6], (DEPTH, D_MODEL, IN_COLS), D_MODEL ** -0.5),
        'gla_w_gate': nrm(ks[7], (DEPTH, 2, GLA_RANK, GLA_QK), GLA_RANK ** -0.5),
        'gla_b_gate': nrm(ks[8], (DEPTH, 2, GLA_QK), 0.02),
        'gla_norm_w': 1.0 + nrm(ks[9], (DEPTH, GROUP_W), 0.02),
        'pool_w': nrm(ks[10], (DEPTH, len(POOL_WINDOWS), POOL_CH, POOL_CH), POOL_CH ** -0.5),
        'pool_scale': 1.0 + nrm(ks[11], (DEPTH, GROUP_W), 0.02),
        'sgu_w': nrm(ks[12], (DEPTH, SGU_HEADS, SGU_CHUNK, SGU_CHUNK), SGU_CHUNK ** -0.5),
        'sgu_b': 1.0 + nrm(ks[13], (DEPTH, SGU_HEADS, SGU_CHUNK), 0.02),
        'sgu_ln_w': 1.0 + nrm(ks[14], (DEPTH, GROUP_W), 0.02),
        'sgu_ln_b': nrm(ks[15], (DEPTH, GROUP_W), 0.02),
        'cm_conv_w': nrm(ks[16], (DEPTH, CONV_WIDTH, GROUP_W), CONV_WIDTH ** -0.5),
        'cm_conv_b': nrm(ks[17], (DEPTH, GROUP_W), 0.02),
        'cm_ln_w': 1.0 + nrm(ks[18], (DEPTH, GROUP_W), 0.02),
        'cm_ln_b': nrm(ks[19], (DEPTH, GROUP_W), 0.02),
        'w_out': nrm(ks[20], (DEPTH, MIX_W, D_MODEL), BETA * MIX_W ** -0.5),
        'ffn_w_up': nrm(ks[21], (DEPTH, D_MODEL, 2 * FFN_HIDDEN), D_MODEL ** -0.5),
        'ffn_conv_w': nrm(ks[22], (DEPTH, FFN_CONV, FFN_CONV, 2 * FFN_HIDDEN), 1.0 / FFN_CONV),
        'ffn_w_down': nrm(ks[23], (DEPTH, FFN_HIDDEN, D_MODEL), BETA * FFN_HIDDEN ** -0.5),
        'post_ln_w': 1.0 + nrm(ks[24], (DEPTH, 2, D_MODEL), 0.02),
        'post_ln_b': nrm(ks[25], (DEPTH, 2, D_MODEL), 0.02),
    }


def reference(x, c, ctx, c_ctx, w_mod, b_mod, w_in, gla_w_gate, gla_b_gate, gla_norm_w,
              pool_w, pool_scale, sgu_w, sgu_b, sgu_ln_w, sgu_ln_b, cm_conv_w, cm_conv_b,
              cm_ln_w, cm_ln_b, w_out, ffn_w_up, ffn_conv_w, ffn_w_down, post_ln_w, post_ln_b):
    rows = x.shape[1] // GRID_W
    ctx_len = ctx.shape[1]
    sc = jax.nn.silu(c)
    sc_ctx = jax.nn.silu(c_ctx)
    for l in range(DEPTH):
        with_ctx_out = l < DEPTH - 1
        m_lat = jnp.split((sc @ w_mod[l] + b_mod[l])[:, None, :], 6, axis=-1)
        m_ctx = jnp.split(sc_ctx @ w_mod[l] + b_mod[l], 6, axis=-1)
        h_lat = modulate(x, m_lat[0], m_lat[1])
        h_ctx = modulate(ctx, m_ctx[0], m_ctx[1])
        y_lat, y_ctx = token_mixer(h_lat, h_ctx, w_in[l], gla_w_gate[l], gla_b_gate[l], gla_norm_w[l],
                                   pool_w[l], pool_scale[l], sgu_w[l], sgu_b[l], sgu_ln_w[l], sgu_ln_b[l],
                                   cm_conv_w[l], cm_conv_b[l], cm_ln_w[l], cm_ln_b[l], w_out[l],
                                   with_ctx_out)
        x = layer_norm(ALPHA * x + m_lat[2] * y_lat, post_ln_w[l, 0], post_ln_b[l, 0])
        h_lat = modulate(x, m_lat[3], m_lat[4])
        f_lat = conv_ffn(h_lat, ffn_w_up[l], ffn_conv_w[l], ffn_w_down[l], rows, GRID_W)
        x = layer_norm(ALPHA * x + m_lat[5] * f_lat, post_ln_w[l, 1], post_ln_b[l, 1])
        if with_ctx_out:
            ctx = layer_norm(ALPHA * ctx + m_ctx[2] * y_ctx, post_ln_w[l, 0], post_ln_b[l, 0])
            h_ctx = modulate(ctx, m_ctx[3], m_ctx[4])
            f_ctx = conv_ffn(h_ctx, ffn_w_up[l], ffn_conv_w[l], ffn_w_down[l], 1, ctx_len)
            ctx = layer_norm(ALPHA * ctx + m_ctx[5] * f_ctx, post_ln_w[l, 1], post_ln_b[l, 1])
    return x
```

```python
import functools

import jax
import jax.numpy as jnp
from jax import lax
from jax.experimental import pallas as pl
from jax.experimental.pallas import tpu as pltpu

EPS = 1e-6
LANES = 128
SUBLANES = 8
GRID_W = 64
FFN_CHUNK = 256
FFN_GRANULE = 256
BF16_ROWS = 16
MIX_TILE = 512
FFN_TILE = 1024
VMEM_LIMIT = 60 * 1024 * 1024

_F32 = jnp.float32
_BF16 = jnp.bfloat16


def _norm_rows(x):
    mu = jnp.mean(x, axis=-1, keepdims=True)
    xc = x - mu
    var = jnp.mean(xc * xc, axis=-1, keepdims=True)
    return xc * lax.rsqrt(var + EPS)


def _sigmoid(x):
    return 1.0 / (1.0 + jnp.exp(-x))


def _const_spec(shape):
    nd = len(shape)
    return pl.BlockSpec(shape, lambda *_: (0,) * nd, pipeline_mode=pl.Buffered(1))


def _ffn_kernel(*refs, tile, gw, n_tiles, n_gran, has_halo, alpha):
    if has_halo:
        (xm_ref, xp_ref, xn_ref, mod_ref, wup_ref, wcv_ref, wdn_ref, lnw_ref, lnb_ref,
         o_ref, h_scr, u0_scr, u1_scr, s_scr, acc_scr) = refs
    else:
        (xm_ref, mod_ref, wup_ref, wcv_ref, wdn_ref, lnw_ref, lnb_ref,
         o_ref, h_scr, u0_scr, u1_scr, s_scr, acc_scr) = refs
    u_bufs = (u0_scr, u1_scr)
    i = pl.program_id(1)
    n_chunks = wup_ref.shape[0]
    pad = gw if has_halo else 0
    n_rows = tile // gw

    shift = mod_ref[0, 3:4, :]
    scale1 = 1.0 + mod_ref[0, 4:5, :]
    gate = mod_ref[0, 5:6, :]

    def modulated(xv):
        return _norm_rows(xv) * scale1 + shift

    h_scr[pad:pad + tile, :] = modulated(xm_ref[0]).astype(_BF16)
    if has_halo:
        hp = jnp.where(i > 0, modulated(xp_ref[0]), 0.0)
        h_scr[0:gw, :] = hp.astype(_BF16)
        hn = jnp.where(i < n_tiles - 1, modulated(xn_ref[0]), 0.0)
        h_scr[pad + tile:pad + tile + gw, :] = hn.astype(_BF16)

    sub_id = lax.broadcasted_iota(jnp.int32, (SUBLANES, LANES), 0)

    def zero_row(v, row):
        lo = (row // SUBLANES) * SUBLANES
        slab = jnp.where(sub_id == row - lo, 0.0, v[lo:lo + SUBLANES])
        parts = [p for p in (v[:lo], slab, v[lo + SUBLANES:]) if p.shape[0]]
        return jnp.concatenate(parts, axis=0)

    ext = h_scr.shape[0]
    up_rows = ext // n_gran
    rows_per_gran = n_rows // n_gran

    gran_rows = rows_per_gran * gw

    def up_proj(c, buf, g):
        rows = pl.ds(pl.multiple_of(g * up_rows, BF16_ROWS), up_rows)
        u_bufs[buf][rows, :] = jnp.dot(h_scr[rows, :], wup_ref[c], preferred_element_type=_F32)

    def conv_cols(ub, wc, base, lo):
        def rows(off):
            return ub[pl.ds(base + off, gw), lo:lo + LANES]

        def taps(dc):
            k = dc + 1
            v = wc[3 + k:4 + k, lo:lo + LANES] * rows(pad)
            if has_halo:
                v = v + wc[k:k + 1, lo:lo + LANES] * rows(0)
                v = v + wc[6 + k:7 + k, lo:lo + LANES] * rows(2 * gw)
            return v
        left = zero_row(pltpu.roll(taps(-1), 1, axis=0), 0)
        right = zero_row(pltpu.roll(taps(1), gw - 1, axis=0), gw - 1)
        return taps(0) + left + right

    def conv_gate(c, buf, g):
        ub = u_bufs[buf]
        wc = wcv_ref[c]
        for r in range(rows_per_gran):
            base = pl.multiple_of(g * gran_rows + r * gw, gw)
            for j in range(FFN_CHUNK // LANES):
                a = conv_cols(ub, wc, base, j * LANES)
                gt = conv_cols(ub, wc, base, FFN_CHUNK + j * LANES)
                s_scr[pl.ds(base, gw), j * LANES:(j + 1) * LANES] = (a * gt * _sigmoid(gt)).astype(_BF16)

    def down_proj(c, g):
        rows = pl.ds(pl.multiple_of(g * gran_rows, gran_rows), gran_rows)
        acc_scr[rows, :] += jnp.dot(s_scr[rows, :], wdn_ref[c], preferred_element_type=_F32)

    def chunk_steps(c, buf, has_next):
        def step(g, carry):
            wrap = (g == 0).astype(jnp.int32)
            down_proj(jnp.maximum(c - wrap, 0), g - 1 + wrap * n_gran)
            if has_next:
                up_proj(c + 1, 1 - buf, g)
            conv_gate(c, buf, g)
            return carry
        lax.fori_loop(0, n_gran, step, 0)

    acc_scr[...] = jnp.zeros_like(acc_scr)
    s_scr[...] = jnp.zeros_like(s_scr)
    lax.fori_loop(0, n_gran, lambda g, carry: (up_proj(0, 0, g), carry)[1], 0)

    def body(k, carry):
        c = 2 * k
        chunk_steps(c, 0, True)
        chunk_steps(c + 1, 1, True)
        return carry

    n_pairs = (n_chunks - 1) // 2
    lax.fori_loop(0, n_pairs, body, 0)
    if n_chunks % 2 == 0:
        chunk_steps(n_chunks - 2, 0, True)
    chunk_steps(n_chunks - 1, (n_chunks - 1) % 2, False)
    down_proj(n_chunks - 1, n_gran - 1)

    y = alpha * xm_ref[0] + gate * acc_scr[...]
    o_ref[0] = _norm_rows(y) * lnw_ref[...] + lnb_ref[...]


def _ffn_call(x, mod, wup, wcv, wdn, lnw, lnb, *, tile, gw, alpha):
    b, l, d = x.shape
    n_tiles = l // tile
    has_halo = l > gw
    rows_per_tile = tile // gw
    n_grid_rows = l // gw
    ext = tile + (2 * gw if has_halo else 0)

    x_spec = pl.BlockSpec((1, tile, d), lambda bi, i: (bi, i, 0))
    in_specs = [x_spec]
    args = [x]
    if has_halo:
        in_specs.append(pl.BlockSpec(
            (1, gw, d), lambda bi, i: (bi, jnp.maximum(i * rows_per_tile - 1, 0), 0)))
        in_specs.append(pl.BlockSpec(
            (1, gw, d), lambda bi, i: (bi, jnp.minimum((i + 1) * rows_per_tile, n_grid_rows - 1), 0)))
        args += [x, x]
    in_specs += [
        pl.BlockSpec((1, 6, d), lambda bi, i: (bi, 0, 0)),
        _const_spec(wup.shape), _const_spec(wcv.shape), _const_spec(wdn.shape),
        _const_spec(lnw.shape), _const_spec(lnb.shape),
    ]
    args += [mod, wup, wcv, wdn, lnw, lnb]
    n_gran = max(1, tile // FFN_GRANULE)
    assert ext % (n_gran * BF16_ROWS) == 0 and rows_per_tile % n_gran == 0
    kern = functools.partial(_ffn_kernel, tile=tile, gw=gw, n_tiles=n_tiles, n_gran=n_gran,
                             has_halo=has_halo, alpha=alpha)
    return pl.pallas_call(
        kern,
        grid=(b, n_tiles),
        in_specs=in_specs,
        out_specs=x_spec,
        out_shape=jax.ShapeDtypeStruct(x.shape, _F32),
        scratch_shapes=[
            pltpu.VMEM((ext, d), _BF16),
            pltpu.VMEM((ext, 2 * FFN_CHUNK), _F32),
            pltpu.VMEM((ext, 2 * FFN_CHUNK), _F32),
            pltpu.VMEM((tile, FFN_CHUNK), _BF16),
            pltpu.VMEM((tile, d), _F32),
        ],
        compiler_params=pltpu.CompilerParams(
            dimension_semantics=("arbitrary", "arbitrary"),
            vmem_limit_bytes=VMEM_LIMIT),
        name="channel_mixer",
    )(*args)


def _prep_ffn_weights(w_up, conv_w, w_down):
    d, two_h = w_up.shape
    hid = two_h // 2
    n_chunks = hid // FFN_CHUNK
    wa = w_up[:, :hid].reshape(d, n_chunks, FFN_CHUNK)
    wg = w_up[:, hid:].reshape(d, n_chunks, FFN_CHUNK)
    wup = jnp.concatenate([wa, wg], axis=-1).transpose(1, 0, 2).astype(_BF16)
    cw = conv_w.reshape(9, two_h)
    ca = cw[:, :hid].reshape(9, n_chunks, FFN_CHUNK)
    cg = cw[:, hid:].reshape(9, n_chunks, FFN_CHUNK)
    wcv = jnp.concatenate([ca, cg], axis=-1).transpose(1, 0, 2)
    wdn = w_down.reshape(n_chunks, FFN_CHUNK, -1).astype(_BF16)
    return wup, wcv, wdn


GLA_HEADS = 4
GLA_DK = 32
GLA_DV = 64
GLA_QK = GLA_HEADS * GLA_DK
GLA_V = GLA_HEADS * GLA_DV
GLA_RANK = 16
GLA_CHUNK = 64
GLA_GATE_NORMALIZER = 16.0
GLA_LOG_DECAY_MIN = -1.0


def _log_decay(logit):
    log_sig = jnp.minimum(logit, 0.0) - jnp.log(1.0 + jnp.exp(-jnp.abs(logit)))
    return jnp.maximum(log_sig / GLA_GATE_NORMALIZER, GLA_LOG_DECAY_MIN)


def _chunk_cumsum(x, reverse):
    n = x.shape[0]
    pos = lax.broadcasted_iota(jnp.int32, x.shape, 0) % GLA_CHUNK
    s = 1
    while s < GLA_CHUNK:
        if reverse:
            x = x + jnp.where(pos < GLA_CHUNK - s, pltpu.roll(x, n - s, axis=0), 0.0)
        else:
            x = x + jnp.where(pos >= s, pltpu.roll(x, s, axis=0), 0.0)
        s *= 2
    return x


def _state_mask():
    r = lax.broadcasted_iota(jnp.int32, (GLA_V, GLA_QK), 0) // GLA_DV
    c = lax.broadcasted_iota(jnp.int32, (GLA_V, GLA_QK), 1) // GLA_DK
    return r == c


def _stack4(a):
    return jnp.concatenate([a, a, a, a], axis=0)


def _compact_state(s):
    return (s[0:GLA_DV] + s[GLA_DV:2 * GLA_DV]) + (s[2 * GLA_DV:3 * GLA_DV] + s[3 * GLA_DV:4 * GLA_DV])


def _state_update(state, v_bf, k_dec_bf, decay_row, mask):
    kv_t = lax.dot_general(v_bf, k_dec_bf, (((0,), (0,)), ((), ())), preferred_element_type=_F32)
    return state * decay_row + jnp.where(mask, kv_t, 0.0)


def _state_kernel(x_ref, mod_ref, w_ref, wg_ref, bg_ref, s0_ref, start_ref, fin_ref, st_scr,
                  *, tile, reverse):
    i = pl.program_id(1)
    n_tiles = pl.num_programs(1)
    chunks = tile // GLA_CHUNK

    @pl.when(i == 0)
    def _():
        st_scr[...] = s0_ref[0]

    shift = mod_ref[0, 0:1, :]
    scale1 = 1.0 + mod_ref[0, 1:2, :]
    h = (_norm_rows(x_ref[0]) * scale1 + shift).astype(_BF16)
    z = jnp.dot(h, w_ref[...], preferred_element_type=_F32)
    k = z[:, 0:GLA_QK]
    v_bf = z[:, GLA_QK:GLA_QK + GLA_V].astype(_BF16)
    low = z[:, GLA_QK + GLA_V:].astype(_BF16)
    g = _log_decay(jnp.dot(low, wg_ref[...], preferred_element_type=_F32) + bg_ref[...])
    run = _chunk_cumsum(g, reverse)
    mask = _state_mask()
    state = st_scr[...]
    order = range(chunks - 1, -1, -1) if reverse else range(chunks)
    for ci in order:
        rows = slice(ci * GLA_CHUNK, (ci + 1) * GLA_CHUNK)
        edge = ci * GLA_CHUNK if reverse else (ci + 1) * GLA_CHUNK - 1
        total = run[edge:edge + 1, :]
        k_dec = (k[rows] * jnp.exp(total - run[rows])).astype(_BF16)
        start_ref[0, ci] = _compact_state(state)
        state = _state_update(state, v_bf[rows], k_dec, jnp.exp(total), mask)
    st_scr[...] = state

    @pl.when(i == n_tiles - 1)
    def _():
        fin_ref[0] = state


def _state_call(x, mod, w, wg, bg, s0, *, tile, reverse):
    b, l, d = x.shape
    n_tiles = l // tile
    chunks = tile // GLA_CHUNK
    tmap = (lambda bi, i: (bi, n_tiles - 1 - i, 0)) if reverse else (lambda bi, i: (bi, i, 0))
    smap = (lambda bi, i: (bi, n_tiles - 1 - i, 0, 0)) if reverse else (lambda bi, i: (bi, i, 0, 0))
    st_spec = pl.BlockSpec((1, GLA_V, GLA_QK), lambda bi, i: (bi, 0, 0))
    kern = functools.partial(_state_kernel, tile=tile, reverse=reverse)
    return pl.pallas_call(
        kern,
        grid=(b, n_tiles),
        in_specs=[
            pl.BlockSpec((1, tile, d), tmap),
            pl.BlockSpec((1, 6, d), lambda bi, i: (bi, 0, 0)),
            _const_spec(w.shape), _const_spec(wg.shape), _const_spec(bg.shape),
            st_spec,
        ],
        out_specs=[
            pl.BlockSpec((1, chunks, GLA_DV, GLA_QK), smap),
            st_spec,
        ],
        out_shape=[
            jax.ShapeDtypeStruct((b, l // GLA_CHUNK, GLA_DV, GLA_QK), _F32),
            jax.ShapeDtypeStruct((b, GLA_V, GLA_QK), _F32),
        ],
        scratch_shapes=[pltpu.VMEM((GLA_V, GLA_QK), _F32)],
        compiler_params=pltpu.CompilerParams(
            dimension_semantics=("arbitrary", "arbitrary"),
            vmem_limit_bytes=VMEM_LIMIT),
        name="gla_state_rev" if reverse else "gla_state_fwd",
    )(x, mod, w, wg, bg, s0)


GROUP_W = 256
COL_Q, COL_K, COL_V, COL_R, COL_LOW, COL_C, COL_B, COL_D = 0, 128, 256, 512, 768, 896, 1408, 1664
IN_COLS = 2176
HALO = 16
POOL_REACH = 8
MIX_BLOCK = 128
CONV_WIDTH = 31
SGU_HEADS = 4
SGU_HD = 64


def _gelu_tanh(x):
    return 0.5 * x * (1.0 + jnp.tanh(0.7978845608028654 * (x + 0.044715 * (x * x * x))))


def _mixer_kernel(xm_ref, xp_ref, xn_ref, mod_ref, win_ref, wg_ref, bg_ref, gnw_ref,
                  pblk_ref, pscale_ref, sguw_ref, sgub_ref, slnw_ref, slnb_ref,
                  cw_ref, cb_ref, clnw_ref, clnb_ref, wout_ref, plw_ref, plb_ref,
                  sb_ref, sf0_ref,
                  o_ref, sfin_ref,
                  z_scr, zb_scr, y_scr, run_scr, o_scr, cat_scr, st_scr,
                  *, tile, seq_len, alpha):
    i = pl.program_id(1)
    n_tiles = pl.num_programs(1)
    n_blocks = tile // MIX_BLOCK

    @pl.when(i == 0)
    def _():
        st_scr[...] = sf0_ref[0]

    shift = mod_ref[0, 0:1, :]
    scale1 = 1.0 + mod_ref[0, 1:2, :]
    gate = mod_ref[0, 2:3, :]

    def modulated(xv):
        return _norm_rows(xv) * scale1 + shift

    h = modulated(xm_ref[0]).astype(_BF16)
    z_scr[...] = jnp.dot(h, win_ref[...], preferred_element_type=_F32)
    hp = jnp.where(i > 0, modulated(xp_ref[0]), 0.0).astype(_BF16)
    hn = jnp.where(i < n_tiles - 1, modulated(xn_ref[0]), 0.0).astype(_BF16)
    zh = jnp.dot(jnp.concatenate([hp, hn], axis=0), win_ref[:, COL_B:],
                 preferred_element_type=_F32)

    def glu(zd):
        return zd[:, :GROUP_W] * _sigmoid(zd[:, GROUP_W:])

    zb_scr[0:HALO, :] = zh[0:HALO, 0:GROUP_W]
    zb_scr[HALO + tile:, :] = zh[HALO:, 0:GROUP_W]
    y_scr[0:HALO, :] = glu(zh[0:HALO, GROUP_W:])
    y_scr[HALO + tile:, :] = glu(zh[HALO:, GROUP_W:])
    for blk in range(n_blocks):
        rows = slice(blk * MIX_BLOCK, (blk + 1) * MIX_BLOCK)
        ext_rows = slice(HALO + blk * MIX_BLOCK, HALO + (blk + 1) * MIX_BLOCK)
        zb_scr[ext_rows, :] = z_scr[rows, COL_B:COL_B + GROUP_W]
        y_scr[ext_rows, :] = glu(z_scr[rows, COL_D:COL_D + 2 * GROUP_W])

    lane = lax.broadcasted_iota(jnp.int32, (MIX_BLOCK, GROUP_W), 1)
    row = lax.broadcasted_iota(jnp.int32, (MIX_BLOCK, GROUP_W), 0)
    lane_group = lane // SGU_HD
    pool_half = jnp.left_shift(1, lane_group)

    for blk in range(n_blocks):
        r0 = blk * MIX_BLOCK
        rows = slice(r0, r0 + MIX_BLOCK)

        n = MIX_BLOCK + 2 * POOL_REACH
        xb = zb_scr[r0 + HALO - POOL_REACH:r0 + HALO - POOL_REACH + n, :]
        p2 = xb + pltpu.roll(xb, 1, axis=0)
        p4 = pltpu.roll(p2, 1, axis=0) + pltpu.roll(p2, n - 1, axis=0)
        p8 = pltpu.roll(p4, 2, axis=0) + pltpu.roll(p4, n - 2, axis=0)
        p16 = pltpu.roll(p8, 4, axis=0) + pltpu.roll(p8, n - 4, axis=0)
        inner = slice(POOL_REACH, POOL_REACH + MIX_BLOCK)
        win = jnp.where(lane_group == 0, p2[inner],
                        jnp.where(lane_group == 1, p4[inner],
                                  jnp.where(lane_group == 2, p8[inner], p16[inner])))
        tok = i * tile + r0 + row
        count = jnp.minimum(tok + pool_half, seq_len) - jnp.maximum(tok - pool_half, 0)
        pooled = win / count.astype(_F32) - xb[inner]
        yb = jnp.dot(pooled.astype(_BF16), pblk_ref[...], preferred_element_type=_F32) * pscale_ref[...]
        cat_scr[rows, GROUP_W:2 * GROUP_W] = yb.astype(_BF16)

        gz = _gelu_tanh(z_scr[rows, COL_C:COL_C + 2 * GROUP_W])
        vn = _norm_rows(gz[:, GROUP_W:]) * slnw_ref[...] + slnb_ref[...]
        vstack = jnp.concatenate(
            [jnp.where(lane_group == hd, vn, 0.0) for hd in range(SGU_HEADS)], axis=0).astype(_BF16)
        sg = jnp.dot(sguw_ref[...], vstack, preferred_element_type=_F32) + sgub_ref[...]
        cat_scr[rows, 2 * GROUP_W:3 * GROUP_W] = (gz[:, :GROUP_W] * sg).astype(_BF16)

        span = MIX_BLOCK + SUBLANES
        conv = None
        for b in range(SUBLANES):
            part = None
            for a in range(4):
                o = SUBLANES * a + b
                if 1 <= o <= CONV_WIDTH:
                    term = cw_ref[o - 1:o, :] * y_scr[r0 + SUBLANES * a:r0 + SUBLANES * a + span, :]
                    part = term if part is None else part + term
            part = part[b:b + MIX_BLOCK]
            conv = part if conv is None else conv + part
        yd = _norm_rows(conv + cb_ref[...]) * clnw_ref[...] + clnb_ref[...]
        cat_scr[rows, 3 * GROUP_W:4 * GROUP_W] = (yd * _sigmoid(yd)).astype(_BF16)

    low = z_scr[:, COL_LOW:COL_LOW + LANES].astype(_BF16)
    g = _log_decay(jnp.dot(low, wg_ref[...], preferred_element_type=_F32) + bg_ref[...])
    run_scr[:, 0:GLA_QK] = _chunk_cumsum(g[:, 0:GLA_QK], False)
    run_scr[:, GLA_QK:] = _chunk_cumsum(g[:, GLA_QK:], True)

    mask_s = _state_mask()
    vr = lax.broadcasted_iota(jnp.int32, (GLA_V, GLA_V), 0) // GLA_DV
    vc = lax.broadcasted_iota(jnp.int32, (GLA_V, GLA_V), 1) // GLA_DV
    mask_v = vr == vc
    t_id = lax.broadcasted_iota(jnp.int32, (GLA_CHUNK, GLA_V), 0)
    s_id = lax.broadcasted_iota(jnp.int32, (GLA_CHUNK, GLA_V), 1) % GLA_CHUNK
    nt_dims = (((1,), (1,)), ((), ()))

    def chunk_step(ci, carry):
        r = pl.ds(pl.multiple_of(ci * GLA_CHUNK, GLA_CHUNK), GLA_CHUNK)
        q = z_scr[r, COL_Q:COL_Q + GLA_QK] * (GLA_DK ** -0.5)
        k = z_scr[r, COL_K:COL_K + GLA_QK]
        v = z_scr[r, COL_V:COL_V + GLA_V].astype(_BF16)
        cf = run_scr[r, 0:GLA_QK]
        rb = run_scr[r, GLA_QK:]
        total = cf[GLA_CHUNK - 1:GLA_CHUNK, :]
        qf = (q * jnp.exp(cf)).astype(_BF16)
        kf = (k * jnp.exp(-cf)).astype(_BF16)
        qb = (q * jnp.exp(rb)).astype(_BF16)
        kb = (k * jnp.exp(-rb)).astype(_BF16)
        k_dec = (k * jnp.exp(total - cf)).astype(_BF16)
        zero = jnp.zeros((), _BF16)
        a_f = lax.dot_general(qf, jnp.where(mask_s, _stack4(kf), zero), nt_dims,
                              preferred_element_type=_F32)
        a_b = lax.dot_general(qb, jnp.where(mask_s, _stack4(kb), zero), nt_dims,
                              preferred_element_type=_F32)
        a = (jnp.where(s_id <= t_id, a_f, 0.0) + jnp.where(s_id >= t_id, a_b, 0.0)).astype(_BF16)
        o = jnp.dot(a, jnp.where(mask_v, _stack4(v), zero), preferred_element_type=_F32)
        sf = st_scr[...]
        sb = jnp.where(mask_s, _stack4(sb_ref[0, ci]), 0.0)
        o = o + lax.dot_general(qf, sf.astype(_BF16), nt_dims, preferred_element_type=_F32)
        o = o + lax.dot_general(qb, sb.astype(_BF16), nt_dims, preferred_element_type=_F32)
        o_scr[r, :] = o
        st_scr[...] = _state_update(sf, v, k_dec, jnp.exp(total), mask_s)
        return carry

    lax.fori_loop(0, tile // GLA_CHUNK, chunk_step, 0)

    @pl.when(i == n_tiles - 1)
    def _():
        sfin_ref[0] = st_scr[...]

    head_mean = jnp.where(mask_v, 1.0 / GLA_DV, 0.0).astype(_BF16)
    for blk in range(n_blocks):
        rows = slice(blk * MIX_BLOCK, (blk + 1) * MIX_BLOCK)
        o = o_scr[rows, :]
        sq = o * o
        sq_hi = sq.astype(_BF16)
        sq_lo = (sq - sq_hi.astype(_F32)).astype(_BF16)
        ms = (jnp.dot(sq_hi, head_mean, preferred_element_type=_F32)
              + jnp.dot(sq_lo, head_mean, preferred_element_type=_F32))
        rg = z_scr[rows, COL_R:COL_R + GROUP_W]
        ya = o * lax.rsqrt(ms + EPS) * gnw_ref[...] * (rg * _sigmoid(rg))
        cat_scr[rows, 0:GROUP_W] = ya.astype(_BF16)

    y = jnp.dot(cat_scr[...], wout_ref[...], preferred_element_type=_F32)
    o_ref[0] = _norm_rows(alpha * xm_ref[0] + gate * y) * plw_ref[...] + plb_ref[...]


def _mixer_call(x, mod, mw, sb_start, sf0, *, tile, alpha):
    b, l, d = x.shape
    n_tiles = l // tile
    halo_per_tile = tile // HALO
    n_halo_blocks = l // HALO
    consts = [mw[k] for k in ("w_in", "wg2", "bg2", "gnw", "pool_blk", "pool_scale", "sgu_w", "sgu_b",
                              "sgu_ln_w", "sgu_ln_b", "cm_w", "cm_b", "cm_ln_w", "cm_ln_b", "w_out",
                              "post_w", "post_b")]
    x_spec = pl.BlockSpec((1, tile, d), lambda bi, i: (bi, i, 0))
    st_spec = pl.BlockSpec((1, GLA_V, GLA_QK), lambda bi, i: (bi, 0, 0))
    in_specs = [
        x_spec,
        pl.BlockSpec((1, HALO, d), lambda bi, i: (bi, jnp.maximum(i * halo_per_tile - 1, 0), 0)),
        pl.BlockSpec((1, HALO, d), lambda bi, i: (bi, jnp.minimum((i + 1) * halo_per_tile, n_halo_blocks - 1), 0)),
        pl.BlockSpec((1, 6, d), lambda bi, i: (bi, 0, 0)),
    ] + [_const_spec(c.shape) for c in consts] + [
        pl.BlockSpec((1, tile // GLA_CHUNK, GLA_DV, GLA_QK), lambda bi, i: (bi, i, 0, 0)),
        st_spec,
    ]
    kern = functools.partial(_mixer_kernel, tile=tile, seq_len=l, alpha=alpha)
    return pl.pallas_call(
        kern,
        grid=(b, n_tiles),
        in_specs=in_specs,
        out_specs=[x_spec, st_spec],
        out_shape=[jax.ShapeDtypeStruct(x.shape, _F32),
                   jax.ShapeDtypeStruct((b, GLA_V, GLA_QK), _F32)],
        scratch_shapes=[
            pltpu.VMEM((tile, IN_COLS), _F32),
            pltpu.VMEM((tile + 2 * HALO, GROUP_W), _F32),
            pltpu.VMEM((tile + 2 * HALO, GROUP_W), _F32),
            pltpu.VMEM((tile, 2 * GLA_QK), _F32),
            pltpu.VMEM((tile, GLA_V), _F32),
            pltpu.VMEM((tile, 4 * GROUP_W), _BF16),
            pltpu.VMEM((GLA_V, GLA_QK), _F32),
        ],
        compiler_params=pltpu.CompilerParams(
            dimension_semantics=("arbitrary", "arbitrary"),
            vmem_limit_bytes=VMEM_LIMIT),
        name="token_mixer",
    )(x, x, x, mod, *consts, sb_start, sf0)


def _prep_mixer_weights(w_in, gla_w_gate, gla_b_gate, gla_norm_w, pool_w, pool_scale, sgu_w, sgu_b,
                        sgu_ln_w, sgu_ln_b, cm_conv_w, cm_conv_b, cm_ln_w, cm_ln_b, w_out,
                        post_w, post_b):
    d = w_in.shape[0]
    qkvr = w_in[:, 0:768]
    low = w_in[:, 768:800]
    zb = w_in[:, 800:1056]
    zc = w_in[:, 1056:1568]
    zd = w_in[:, 1568:2080]
    pad = jnp.zeros((d, LANES - 2 * GLA_RANK), _F32)
    w_main = jnp.concatenate([qkvr, low, pad, zc, zb, zd], axis=1).astype(_BF16)
    kv = w_in[:, 128:512]
    pad1 = jnp.zeros((d, LANES - GLA_RANK), _F32)
    w_state = [jnp.concatenate([kv, low[:, di * GLA_RANK:(di + 1) * GLA_RANK], pad1], axis=1).astype(_BF16)
               for di in range(2)]
    wg1 = [jnp.zeros((LANES, GLA_QK), _F32).at[0:GLA_RANK].set(gla_w_gate[di]).astype(_BF16)
           for di in range(2)]
    bg1 = [gla_b_gate[di][None, :] for di in range(2)]
    wg2 = jnp.zeros((LANES, 2 * GLA_QK), _F32)
    wg2 = wg2.at[0:GLA_RANK, 0:GLA_QK].set(gla_w_gate[0])
    wg2 = wg2.at[GLA_RANK:2 * GLA_RANK, GLA_QK:].set(gla_w_gate[1]).astype(_BF16)
    bg2 = jnp.concatenate([gla_b_gate[0], gla_b_gate[1]])[None, :]
    pool_blk = jnp.zeros((GROUP_W, GROUP_W), _F32)
    for gi in range(pool_w.shape[0]):
        sl = slice(gi * SGU_HD, (gi + 1) * SGU_HD)
        pool_blk = pool_blk.at[sl, sl].set(pool_w[gi])
    row = lambda a: a[None, :]
    return {
        "w_in": w_main, "w_state": w_state, "wg1": wg1, "bg1": bg1, "wg2": wg2, "bg2": bg2,
        "gnw": row(gla_norm_w), "pool_blk": pool_blk.astype(_BF16), "pool_scale": row(pool_scale),
        "sgu_w": jnp.concatenate([sgu_w[hd] for hd in range(SGU_HEADS)], axis=1).astype(_BF16),
        "sgu_b": jnp.repeat(sgu_b.T, SGU_HD, axis=1),
        "sgu_ln_w": row(sgu_ln_w), "sgu_ln_b": row(sgu_ln_b),
        "cm_w": jnp.concatenate([cm_conv_w, jnp.zeros((1, GROUP_W), _F32)], axis=0),
        "cm_b": row(cm_conv_b), "cm_ln_w": row(cm_ln_w), "cm_ln_b": row(cm_ln_b),
        "w_out": w_out.astype(_BF16), "post_w": row(post_w), "post_b": row(post_b),
    }


def _mod_kernel(c_ref, w_ref, b_ref, o_ref):
    cv = c_ref[...]
    o_ref[0] = jnp.dot(cv * _sigmoid(cv), w_ref[0], preferred_element_type=_F32,
                       precision=lax.Precision.HIGHEST) + b_ref[0]


def _mod_call(cond, w_mod, b_mod):
    depth, d, six_d = w_mod.shape
    rows = cond.shape[0]
    return pl.pallas_call(
        _mod_kernel,
        grid=(depth, six_d // d),
        in_specs=[
            pl.BlockSpec((rows, d), lambda li, j: (0, 0)),
            pl.BlockSpec((1, d, d), lambda li, j: (li, 0, j)),
            pl.BlockSpec((1, 1, d), lambda li, j: (li, 0, j)),
        ],
        out_specs=pl.BlockSpec((1, rows, d), lambda li, j: (li, 0, j)),
        out_shape=jax.ShapeDtypeStruct((depth, rows, six_d), _F32),
        compiler_params=pltpu.CompilerParams(
            dimension_semantics=("arbitrary", "arbitrary"),
            vmem_limit_bytes=VMEM_LIMIT),
        name="adaln_modulation",
    )(cond, w_mod, b_mod.reshape(depth, 1, six_d))


def _pick_tile(length, target):
    return min(length, target)


def kernel(x, c, ctx, c_ctx, w_mod, b_mod, w_in, gla_w_gate, gla_b_gate, gla_norm_w, pool_w, pool_scale, sgu_w, sgu_b, sgu_ln_w, sgu_ln_b, cm_conv_w, cm_conv_b, cm_ln_w, cm_ln_b, w_out, ffn_w_up, ffn_conv_w, ffn_w_down, post_ln_w, post_ln_b):
    batch, seq, d = x.shape
    ctx_len = ctx.shape[1]
    depth = w_mod.shape[0]
    alpha = (2 * depth) ** 0.25

    cond_rows = -(-(batch + 1) // SUBLANES) * SUBLANES
    cond = jnp.concatenate([c, c_ctx[None, :], jnp.zeros((cond_rows - batch - 1, d), _F32)], axis=0)
    mod = _mod_call(cond, w_mod, b_mod)

    lat_mix_tile = _pick_tile(seq, MIX_TILE)
    ctx_mix_tile = _pick_tile(ctx_len, MIX_TILE)
    lat_ffn_tile = _pick_tile(seq, FFN_TILE)
    zero_state = jnp.zeros((batch, GLA_V, GLA_QK), _F32)

    for li in range(depth):
        with_ctx_out = li < depth - 1
        mod_lat = mod[li, :batch].reshape(batch, 6, d)
        mod_ctx = jnp.broadcast_to(mod[li, batch].reshape(1, 6, d), (batch, 6, d))
        mw = _prep_mixer_weights(w_in[li], gla_w_gate[li], gla_b_gate[li], gla_norm_w[li], pool_w[li],
                                 pool_scale[li], sgu_w[li], sgu_b[li], sgu_ln_w[li], sgu_ln_b[li],
                                 cm_conv_w[li], cm_conv_b[li], cm_ln_w[li], cm_ln_b[li], w_out[li],
                                 post_ln_w[li, 0], post_ln_b[li, 0])
        ffn_w = _prep_ffn_weights(ffn_w_up[li], ffn_conv_w[li], ffn_w_down[li])
        ffn_ln = (post_ln_w[li, 1][None, :], post_ln_b[li, 1][None, :])

        ctx_sb, ctx_sb_fin = _state_call(ctx, mod_ctx, mw["w_state"][1], mw["wg1"][1], mw["bg1"][1],
                                         zero_state, tile=ctx_mix_tile, reverse=True)
        if with_ctx_out:
            ctx_mixed, ctx_sf_fin = _mixer_call(ctx, mod_ctx, mw, ctx_sb, zero_state,
                                                tile=ctx_mix_tile, alpha=alpha)
        else:
            _, ctx_sf_fin = _state_call(ctx, mod_ctx, mw["w_state"][0], mw["wg1"][0], mw["bg1"][0],
                                        zero_state, tile=ctx_mix_tile, reverse=False)

        lat_sb, _ = _state_call(x, mod_lat, mw["w_state"][1], mw["wg1"][1], mw["bg1"][1],
                                ctx_sb_fin, tile=lat_mix_tile, reverse=True)
        x, _ = _mixer_call(x, mod_lat, mw, lat_sb, ctx_sf_fin, tile=lat_mix_tile, alpha=alpha)
        x = _ffn_call(x, mod_lat, *ffn_w, *ffn_ln, tile=lat_ffn_tile, gw=GRID_W, alpha=alpha)
        if with_ctx_out:
            ctx = _ffn_call(ctx_mixed, mod_ctx, *ffn_w, *ffn_ln, tile=ctx_len, gw=ctx_len, alpha=alpha)
    return x
```

```python
import functools

import jax
import jax.numpy as jnp
from jax import lax
from jax.experimental import pallas as pl
from jax.experimental.pallas import tpu as pltpu

EPS = 1e-6
LANES = 128
SUBLANES = 8
GRID_W = 64
FFN_CHUNK = 256
FFN_GRANULE = 512
BF16_ROWS = 16
MIX_TILE = 512
FFN_TILE = 1024
VMEM_LIMIT = 60 * 1024 * 1024

_F32 = jnp.float32
_BF16 = jnp.bfloat16


def _norm_rows(x):
    mu = jnp.mean(x, axis=-1, keepdims=True)
    xc = x - mu
    var = jnp.mean(xc * xc, axis=-1, keepdims=True)
    return xc * lax.rsqrt(var + EPS)


def _sigmoid(x):
    return 1.0 / (1.0 + jnp.exp(-x))


def _const_spec(shape):
    nd = len(shape)
    return pl.BlockSpec(shape, lambda *_: (0,) * nd, pipeline_mode=pl.Buffered(1))


def _ffn_kernel(*refs, tile, gw, n_tiles, n_gran, has_halo, alpha):
    if has_halo:
        (xm_ref, xp_ref, xn_ref, mod_ref, wup_ref, wcv_ref, wdn_ref, lnw_ref, lnb_ref,
         o_ref, h_scr, u0_scr, u1_scr, s_scr, acc_scr) = refs
    else:
        (xm_ref, mod_ref, wup_ref, wcv_ref, wdn_ref, lnw_ref, lnb_ref,
         o_ref, h_scr, u0_scr, u1_scr, s_scr, acc_scr) = refs
    u_bufs = (u0_scr, u1_scr)
    i = pl.program_id(1)
    n_chunks = wup_ref.shape[0]
    pad = gw if has_halo else 0
    n_rows = tile // gw

    shift = mod_ref[0, 3:4, :]
    scale1 = 1.0 + mod_ref[0, 4:5, :]
    gate = mod_ref[0, 5:6, :]

    def modulated(xv):
        return _norm_rows(xv) * scale1 + shift

    h_scr[pad:pad + tile, :] = modulated(xm_ref[0]).astype(_BF16)
    if has_halo:
        hp = jnp.where(i > 0, modulated(xp_ref[0]), 0.0)
        h_scr[0:gw, :] = hp.astype(_BF16)
        hn = jnp.where(i < n_tiles - 1, modulated(xn_ref[0]), 0.0)
        h_scr[pad + tile:pad + tile + gw, :] = hn.astype(_BF16)

    sub_id = lax.broadcasted_iota(jnp.int32, (SUBLANES, LANES), 0)

    def zero_row(v, row):
        lo = (row // SUBLANES) * SUBLANES
        slab = jnp.where(sub_id == row - lo, 0.0, v[lo:lo + SUBLANES])
        parts = [p for p in (v[:lo], slab, v[lo + SUBLANES:]) if p.shape[0]]
        return jnp.concatenate(parts, axis=0)

    ext = h_scr.shape[0]
    up_rows = ext // n_gran
    rows_per_gran = n_rows // n_gran

    gran_rows = rows_per_gran * gw

    def up_proj(c, buf, g):
        rows = pl.ds(pl.multiple_of(g * up_rows, BF16_ROWS), up_rows)
        u_bufs[buf][rows, :] = jnp.dot(h_scr[rows, :], wup_ref[c], preferred_element_type=_F32)

    def conv_cols(ub, wc, base, lo):
        def rows(off):
            return ub[pl.ds(base + off, gw), lo:lo + LANES]

        def taps(dc):
            k = dc + 1
            v = wc[3 + k:4 + k, lo:lo + LANES] * rows(pad)
            if has_halo:
                v = v + wc[k:k + 1, lo:lo + LANES] * rows(0)
                v = v + wc[6 + k:7 + k, lo:lo + LANES] * rows(2 * gw)
            return v
        left = zero_row(pltpu.roll(taps(-1), 1, axis=0), 0)
        right = zero_row(pltpu.roll(taps(1), gw - 1, axis=0), gw - 1)
        return taps(0) + left + right

    def conv_gate(c, buf, g):
        ub = u_bufs[buf]
        wc = wcv_ref[c]
        for r in range(rows_per_gran):
            base = pl.multiple_of(g * gran_rows + r * gw, gw)
            for j in range(FFN_CHUNK // LANES):
                a = conv_cols(ub, wc, base, j * LANES)
                gt = conv_cols(ub, wc, base, FFN_CHUNK + j * LANES)
                s_scr[pl.ds(base, gw), j * LANES:(j + 1) * LANES] = (a * gt * _sigmoid(gt)).astype(_BF16)

    def down_proj(c, g):
        rows = pl.ds(pl.multiple_of(g * gran_rows, gran_rows), gran_rows)
        acc_scr[rows, :] += jnp.dot(s_scr[rows, :], wdn_ref[c], preferred_element_type=_F32)

    def chunk_steps(c, buf, has_next):
        def step(g, carry):
            if has_next:
                up_proj(c + 1, 1 - buf, g)
            wrap = (g == 0).astype(jnp.int32)
            down_proj(jnp.maximum(c - wrap, 0), g - 1 + wrap * n_gran)
            conv_gate(c, buf, g)
            return carry
        lax.fori_loop(0, n_gran, step, 0)

    acc_scr[...] = jnp.zeros_like(acc_scr)
    s_scr[...] = jnp.zeros_like(s_scr)
    lax.fori_loop(0, n_gran, lambda g, carry: (up_proj(0, 0, g), carry)[1], 0)

    def body(k, carry):
        c = 2 * k
        chunk_steps(c, 0, True)
        chunk_steps(c + 1, 1, True)
        return carry

    n_pairs = (n_chunks - 1) // 2
    lax.fori_loop(0, n_pairs, body, 0)
    if n_chunks % 2 == 0:
        chunk_steps(n_chunks - 2, 0, True)
    chunk_steps(n_chunks - 1, (n_chunks - 1) % 2, False)
    down_proj(n_chunks - 1, n_gran - 1)

    y = alpha * xm_ref[0] + gate * acc_scr[...]
    o_ref[0] = _norm_rows(y) * lnw_ref[...] + lnb_ref[...]


def _ffn_call(x, mod, wup, wcv, wdn, lnw, lnb, *, tile, gw, alpha):
    b, l, d = x.shape
    n_tiles = l // tile
    has_halo = l > gw
    rows_per_tile = tile // gw
    n_grid_rows = l // gw
    ext = tile + (2 * gw if has_halo else 0)

    x_spec = pl.BlockSpec((1, tile, d), lambda bi, i: (bi, i, 0))
    in_specs = [x_spec]
    args = [x]
    if has_halo:
        in_specs.append(pl.BlockSpec(
            (1, gw, d), lambda bi, i: (bi, jnp.maximum(i * rows_per_tile - 1, 0), 0)))
        in_specs.append(pl.BlockSpec(
            (1, gw, d), lambda bi, i: (bi, jnp.minimum((i + 1) * rows_per_tile, n_grid_rows - 1), 0)))
        args += [x, x]
    in_specs += [
        pl.BlockSpec((1, 6, d), lambda bi, i: (bi, 0, 0)),
        _const_spec(wup.shape), _const_spec(wcv.shape), _const_spec(wdn.shape),
        _const_spec(lnw.shape), _const_spec(lnb.shape),
    ]
    args += [mod, wup, wcv, wdn, lnw, lnb]
    n_gran = max(1, tile // FFN_GRANULE)
    assert ext % (n_gran * BF16_ROWS) == 0 and rows_per_tile % n_gran == 0
    kern = functools.partial(_ffn_kernel, tile=tile, gw=gw, n_tiles=n_tiles, n_gran=n_gran,
                             has_halo=has_halo, alpha=alpha)
    return pl.pallas_call(
        kern,
        grid=(b, n_tiles),
        in_specs=in_specs,
        out_specs=x_spec,
        out_shape=jax.ShapeDtypeStruct(x.shape, _F32),
        scratch_shapes=[
            pltpu.VMEM((ext, d), _BF16),
            pltpu.VMEM((ext, 2 * FFN_CHUNK), _F32),
            pltpu.VMEM((ext, 2 * FFN_CHUNK), _F32),
            pltpu.VMEM((tile, FFN_CHUNK), _BF16),
            pltpu.VMEM((tile, d), _F32),
        ],
        compiler_params=pltpu.CompilerParams(
            dimension_semantics=("arbitrary", "arbitrary"),
            vmem_limit_bytes=VMEM_LIMIT),
        name="channel_mixer",
    )(*args)


def _prep_ffn_weights(w_up, conv_w, w_down):
    d, two_h = w_up.shape
    hid = two_h // 2
    n_chunks = hid // FFN_CHUNK
    wa = w_up[:, :hid].reshape(d, n_chunks, FFN_CHUNK)
    wg = w_up[:, hid:].reshape(d, n_chunks, FFN_CHUNK)
    wup = jnp.concatenate([wa, wg], axis=-1).transpose(1, 0, 2).astype(_BF16)
    cw = conv_w.reshape(9, two_h)
    ca = cw[:, :hid].reshape(9, n_chunks, FFN_CHUNK)
    cg = cw[:, hid:].reshape(9, n_chunks, FFN_CHUNK)
    wcv = jnp.concatenate([ca, cg], axis=-1).transpose(1, 0, 2)
    wdn = w_down.reshape(n_chunks, FFN_CHUNK, -1).astype(_BF16)
    return wup, wcv, wdn


GLA_HEADS = 4
GLA_DK = 32
GLA_DV = 64
GLA_QK = GLA_HEADS * GLA_DK
GLA_V = GLA_HEADS * GLA_DV
GLA_RANK = 16
GLA_CHUNK = 64
GLA_GATE_NORMALIZER = 16.0
GLA_LOG_DECAY_MIN = -1.0


def _log_decay(logit):
    log_sig = jnp.minimum(logit, 0.0) - jnp.log(1.0 + jnp.exp(-jnp.abs(logit)))
    return jnp.maximum(log_sig / GLA_GATE_NORMALIZER, GLA_LOG_DECAY_MIN)


def _chunk_cumsum(x, reverse):
    n = x.shape[0]
    pos = lax.broadcasted_iota(jnp.int32, x.shape, 0) % GLA_CHUNK
    s = 1
    while s < GLA_CHUNK:
        if reverse:
            x = x + jnp.where(pos < GLA_CHUNK - s, pltpu.roll(x, n - s, axis=0), 0.0)
        else:
            x = x + jnp.where(pos >= s, pltpu.roll(x, s, axis=0), 0.0)
        s *= 2
    return x


def _state_mask():
    r = lax.broadcasted_iota(jnp.int32, (GLA_V, GLA_QK), 0) // GLA_DV
    c = lax.broadcasted_iota(jnp.int32, (GLA_V, GLA_QK), 1) // GLA_DK
    return r == c


def _stack4(a):
    return jnp.concatenate([a, a, a, a], axis=0)


def _compact_state(s):
    return (s[0:GLA_DV] + s[GLA_DV:2 * GLA_DV]) + (s[2 * GLA_DV:3 * GLA_DV] + s[3 * GLA_DV:4 * GLA_DV])


def _state_update(state, v_bf, k_dec_bf, decay_row, mask):
    kv_t = lax.dot_general(v_bf, k_dec_bf, (((0,), (0,)), ((), ())), preferred_element_type=_F32)
    return state * decay_row + jnp.where(mask, kv_t, 0.0)


def _state_kernel(x_ref, mod_ref, w_ref, wg_ref, bg_ref, s0_ref, start_ref, fin_ref, st_scr,
                  *, tile, reverse):
    i = pl.program_id(1)
    n_tiles = pl.num_programs(1)
    chunks = tile // GLA_CHUNK

    @pl.when(i == 0)
    def _():
        st_scr[...] = s0_ref[0]

    shift = mod_ref[0, 0:1, :]
    scale1 = 1.0 + mod_ref[0, 1:2, :]
    h = (_norm_rows(x_ref[0]) * scale1 + shift).astype(_BF16)
    z = jnp.dot(h, w_ref[...], preferred_element_type=_F32)
    k = z[:, 0:GLA_QK]
    v_bf = z[:, GLA_QK:GLA_QK + GLA_V].astype(_BF16)
    low = z[:, GLA_QK + GLA_V:].astype(_BF16)
    g = _log_decay(jnp.dot(low, wg_ref[...], preferred_element_type=_F32) + bg_ref[...])
    run = _chunk_cumsum(g, reverse)
    mask = _state_mask()
    state = st_scr[...]
    order = range(chunks - 1, -1, -1) if reverse else range(chunks)
    for ci in order:
        rows = slice(ci * GLA_CHUNK, (ci + 1) * GLA_CHUNK)
        edge = ci * GLA_CHUNK if reverse else (ci + 1) * GLA_CHUNK - 1
        total = run[edge:edge + 1, :]
        k_dec = (k[rows] * jnp.exp(total - run[rows])).astype(_BF16)
        start_ref[0, ci] = _compact_state(state)
        state = _state_update(state, v_bf[rows], k_dec, jnp.exp(total), mask)
    st_scr[...] = state

    @pl.when(i == n_tiles - 1)
    def _():
        fin_ref[0] = state


def _state_call(x, mod, w, wg, bg, s0, *, tile, reverse):
    b, l, d = x.shape
    n_tiles = l // tile
    chunks = tile // GLA_CHUNK
    tmap = (lambda bi, i: (bi, n_tiles - 1 - i, 0)) if reverse else (lambda bi, i: (bi, i, 0))
    smap = (lambda bi, i: (bi, n_tiles - 1 - i, 0, 0)) if reverse else (lambda bi, i: (bi, i, 0, 0))
    st_spec = pl.BlockSpec((1, GLA_V, GLA_QK), lambda bi, i: (bi, 0, 0))
    kern = functools.partial(_state_kernel, tile=tile, reverse=reverse)
    return pl.pallas_call(
        kern,
        grid=(b, n_tiles),
        in_specs=[
            pl.BlockSpec((1, tile, d), tmap),
            pl.BlockSpec((1, 6, d), lambda bi, i: (bi, 0, 0)),
            _const_spec(w.shape), _const_spec(wg.shape), _const_spec(bg.shape),
            st_spec,
        ],
        out_specs=[
            pl.BlockSpec((1, chunks, GLA_DV, GLA_QK), smap),
            st_spec,
        ],
        out_shape=[
            jax.ShapeDtypeStruct((b, l // GLA_CHUNK, GLA_DV, GLA_QK), _F32),
            jax.ShapeDtypeStruct((b, GLA_V, GLA_QK), _F32),
        ],
        scratch_shapes=[pltpu.VMEM((GLA_V, GLA_QK), _F32)],
        compiler_params=pltpu.CompilerParams(
            dimension_semantics=("arbitrary", "arbitrary"),
            vmem_limit_bytes=VMEM_LIMIT),
        name="gla_state_rev" if reverse else "gla_state_fwd",
    )(x, mod, w, wg, bg, s0)


GROUP_W = 256
COL_Q, COL_K, COL_V, COL_R, COL_LOW, COL_C, COL_B, COL_D = 0, 128, 256, 512, 768, 896, 1408, 1664
IN_COLS = 2176
HALO = 16
POOL_REACH = 8
MIX_BLOCK = 128
CONV_WIDTH = 31
SGU_HEADS = 4
SGU_HD = 64


def _gelu_tanh(x):
    return 0.5 * x * (1.0 + jnp.tanh(0.7978845608028654 * (x + 0.044715 * (x * x * x))))


def _mixer_kernel(xm_ref, xp_ref, xn_ref, mod_ref, win_ref, wg_ref, bg_ref, gnw_ref,
                  pblk_ref, pscale_ref, sguw_ref, sgub_ref, slnw_ref, slnb_ref,
                  cw_ref, cb_ref, clnw_ref, clnb_ref, wout_ref, plw_ref, plb_ref,
                  sb_ref, sf0_ref,
                  o_ref, sfin_ref,
                  z_scr, zb_scr, y_scr, run_scr, o_scr, cat_scr, st_scr,
                  *, tile, seq_len, alpha):
    i = pl.program_id(1)
    n_tiles = pl.num_programs(1)
    n_blocks = tile // MIX_BLOCK

    @pl.when(i == 0)
    def _():
        st_scr[...] = sf0_ref[0]

    shift = mod_ref[0, 0:1, :]
    scale1 = 1.0 + mod_ref[0, 1:2, :]
    gate = mod_ref[0, 2:3, :]

    def modulated(xv):
        return _norm_rows(xv) * scale1 + shift

    h = modulated(xm_ref[0]).astype(_BF16)
    z_scr[...] = jnp.dot(h, win_ref[...], preferred_element_type=_F32)
    hp = jnp.where(i > 0, modulated(xp_ref[0]), 0.0).astype(_BF16)
    hn = jnp.where(i < n_tiles - 1, modulated(xn_ref[0]), 0.0).astype(_BF16)
    zh = jnp.dot(jnp.concatenate([hp, hn], axis=0), win_ref[:, COL_B:],
                 preferred_element_type=_F32)

    def glu(zd):
        return zd[:, :GROUP_W] * _sigmoid(zd[:, GROUP_W:])

    zb_scr[0:HALO, :] = zh[0:HALO, 0:GROUP_W]
    zb_scr[HALO + tile:, :] = zh[HALO:, 0:GROUP_W]
    y_scr[0:HALO, :] = glu(zh[0:HALO, GROUP_W:])
    y_scr[HALO + tile:, :] = glu(zh[HALO:, GROUP_W:])
    for blk in range(n_blocks):
        rows = slice(blk * MIX_BLOCK, (blk + 1) * MIX_BLOCK)
        ext_rows = slice(HALO + blk * MIX_BLOCK, HALO + (blk + 1) * MIX_BLOCK)
        zb_scr[ext_rows, :] = z_scr[rows, COL_B:COL_B + GROUP_W]
        y_scr[ext_rows, :] = glu(z_scr[rows, COL_D:COL_D + 2 * GROUP_W])

    lane = lax.broadcasted_iota(jnp.int32, (MIX_BLOCK, GROUP_W), 1)
    row = lax.broadcasted_iota(jnp.int32, (MIX_BLOCK, GROUP_W), 0)
    lane_group = lane // SGU_HD
    pool_half = jnp.left_shift(1, lane_group)

    for blk in range(n_blocks):
        r0 = blk * MIX_BLOCK
        rows = slice(r0, r0 + MIX_BLOCK)

        n = MIX_BLOCK + 2 * POOL_REACH
        xb = zb_scr[r0 + HALO - POOL_REACH:r0 + HALO - POOL_REACH + n, :]
        p2 = xb + pltpu.roll(xb, 1, axis=0)
        p4 = pltpu.roll(p2, 1, axis=0) + pltpu.roll(p2, n - 1, axis=0)
        p8 = pltpu.roll(p4, 2, axis=0) + pltpu.roll(p4, n - 2, axis=0)
        p16 = pltpu.roll(p8, 4, axis=0) + pltpu.roll(p8, n - 4, axis=0)
        inner = slice(POOL_REACH, POOL_REACH + MIX_BLOCK)
        win = jnp.where(lane_group == 0, p2[inner],
                        jnp.where(lane_group == 1, p4[inner],
                                  jnp.where(lane_group == 2, p8[inner], p16[inner])))
        tok = i * tile + r0 + row
        count = jnp.minimum(tok + pool_half, seq_len) - jnp.maximum(tok - pool_half, 0)
        pooled = win / count.astype(_F32) - xb[inner]
        yb = jnp.dot(pooled.astype(_BF16), pblk_ref[...], preferred_element_type=_F32) * pscale_ref[...]
        cat_scr[rows, GROUP_W:2 * GROUP_W] = yb.astype(_BF16)

        gz = _gelu_tanh(z_scr[rows, COL_C:COL_C + 2 * GROUP_W])
        vn = _norm_rows(gz[:, GROUP_W:]) * slnw_ref[...] + slnb_ref[...]
        vstack = jnp.concatenate(
            [jnp.where(lane_group == hd, vn, 0.0) for hd in range(SGU_HEADS)], axis=0).astype(_BF16)
        sg = jnp.dot(sguw_ref[...], vstack, preferred_element_type=_F32) + sgub_ref[...]
        cat_scr[rows, 2 * GROUP_W:3 * GROUP_W] = (gz[:, :GROUP_W] * sg).astype(_BF16)

        span = MIX_BLOCK + SUBLANES
        conv = None
        for b in range(SUBLANES):
            part = None
            for a in range(4):
                o = SUBLANES * a + b
                if 1 <= o <= CONV_WIDTH:
                    term = cw_ref[o - 1:o, :] * y_scr[r0 + SUBLANES * a:r0 + SUBLANES * a + span, :]
                    part = term if part is None else part + term
            part = part[b:b + MIX_BLOCK]
            conv = part if conv is None else conv + part
        yd = _norm_rows(conv + cb_ref[...]) * clnw_ref[...] + clnb_ref[...]
        cat_scr[rows, 3 * GROUP_W:4 * GROUP_W] = (yd * _sigmoid(yd)).astype(_BF16)

    low = z_scr[:, COL_LOW:COL_LOW + LANES].astype(_BF16)
    g = _log_decay(jnp.dot(low, wg_ref[...], preferred_element_type=_F32) + bg_ref[...])
    run_scr[:, 0:GLA_QK] = _chunk_cumsum(g[:, 0:GLA_QK], False)
    run_scr[:, GLA_QK:] = _chunk_cumsum(g[:, GLA_QK:], True)

    mask_s = _state_mask()
    vr = lax.broadcasted_iota(jnp.int32, (GLA_V, GLA_V), 0) // GLA_DV
    vc = lax.broadcasted_iota(jnp.int32, (GLA_V, GLA_V), 1) // GLA_DV
    mask_v = vr == vc
    t_id = lax.broadcasted_iota(jnp.int32, (GLA_CHUNK, GLA_V), 0)
    s_id = lax.broadcasted_iota(jnp.int32, (GLA_CHUNK, GLA_V), 1) % GLA_CHUNK
    nt_dims = (((1,), (1,)), ((), ()))

    def chunk_step(ci, carry):
        r = pl.ds(pl.multiple_of(ci * GLA_CHUNK, GLA_CHUNK), GLA_CHUNK)
        q = z_scr[r, COL_Q:COL_Q + GLA_QK] * (GLA_DK ** -0.5)
        k = z_scr[r, COL_K:COL_K + GLA_QK]
        v = z_scr[r, COL_V:COL_V + GLA_V].astype(_BF16)
        cf = run_scr[r, 0:GLA_QK]
        rb = run_scr[r, GLA_QK:]
        total = cf[GLA_CHUNK - 1:GLA_CHUNK, :]
        qf = (q * jnp.exp(cf)).astype(_BF16)
        kf = (k * jnp.exp(-cf)).astype(_BF16)
        qb = (q * jnp.exp(rb)).astype(_BF16)
        kb = (k * jnp.exp(-rb)).astype(_BF16)
        k_dec = (k * jnp.exp(total - cf)).astype(_BF16)
        zero = jnp.zeros((), _BF16)
        a_f = lax.dot_general(qf, jnp.where(mask_s, _stack4(kf), zero), nt_dims,
                              preferred_element_type=_F32)
        a_b = lax.dot_general(qb, jnp.where(mask_s, _stack4(kb), zero), nt_dims,
                              preferred_element_type=_F32)
        a = (jnp.where(s_id <= t_id, a_f, 0.0) + jnp.where(s_id >= t_id, a_b, 0.0)).astype(_BF16)
        o = jnp.dot(a, jnp.where(mask_v, _stack4(v), zero), preferred_element_type=_F32)
        sf = st_scr[...]
        sb = jnp.where(mask_s, _stack4(sb_ref[0, ci]), 0.0)
        o = o + lax.dot_general(qf, sf.astype(_BF16), nt_dims, preferred_element_type=_F32)
        o = o + lax.dot_general(qb, sb.astype(_BF16), nt_dims, preferred_element_type=_F32)
        o_scr[r, :] = o
        st_scr[...] = _state_update(sf, v, k_dec, jnp.exp(total), mask_s)
        return carry

    lax.fori_loop(0, tile // GLA_CHUNK, chunk_step, 0)

    @pl.when(i == n_tiles - 1)
    def _():
        sfin_ref[0] = st_scr[...]

    head_mean = jnp.where(mask_v, 1.0 / GLA_DV, 0.0).astype(_BF16)
    for blk in range(n_blocks):
        rows = slice(blk * MIX_BLOCK, (blk + 1) * MIX_BLOCK)
        o = o_scr[rows, :]
        sq = o * o
        sq_hi = sq.astype(_BF16)
        sq_lo = (sq - sq_hi.astype(_F32)).astype(_BF16)
        ms = (jnp.dot(sq_hi, head_mean, preferred_element_type=_F32)
              + jnp.dot(sq_lo, head_mean, preferred_element_type=_F32))
        rg = z_scr[rows, COL_R:COL_R + GROUP_W]
        ya = o * lax.rsqrt(ms + EPS) * gnw_ref[...] * (rg * _sigmoid(rg))
        cat_scr[rows, 0:GROUP_W] = ya.astype(_BF16)

    y = jnp.dot(cat_scr[...], wout_ref[...], preferred_element_type=_F32)
    o_ref[0] = _norm_rows(alpha * xm_ref[0] + gate * y) * plw_ref[...] + plb_ref[...]


def _mixer_call(x, mod, mw, sb_start, sf0, *, tile, alpha):
    b, l, d = x.shape
    n_tiles = l // tile
    halo_per_tile = tile // HALO
    n_halo_blocks = l // HALO
    consts = [mw[k] for k in ("w_in", "wg2", "bg2", "gnw", "pool_blk", "pool_scale", "sgu_w", "sgu_b",
                              "sgu_ln_w", "sgu_ln_b", "cm_w", "cm_b", "cm_ln_w", "cm_ln_b", "w_out",
                              "post_w", "post_b")]
    x_spec = pl.BlockSpec((1, tile, d), lambda bi, i: (bi, i, 0))
    st_spec = pl.BlockSpec((1, GLA_V, GLA_QK), lambda bi, i: (bi, 0, 0))
    in_specs = [
        x_spec,
        pl.BlockSpec((1, HALO, d), lambda bi, i: (bi, jnp.maximum(i * halo_per_tile - 1, 0), 0)),
        pl.BlockSpec((1, HALO, d), lambda bi, i: (bi, jnp.minimum((i + 1) * halo_per_tile, n_halo_blocks - 1), 0)),
        pl.BlockSpec((1, 6, d), lambda bi, i: (bi, 0, 0)),
    ] + [_const_spec(c.shape) for c in consts] + [
        pl.BlockSpec((1, tile // GLA_CHUNK, GLA_DV, GLA_QK), lambda bi, i: (bi, i, 0, 0)),
        st_spec,
    ]
    kern = functools.partial(_mixer_kernel, tile=tile, seq_len=l, alpha=alpha)
    return pl.pallas_call(
        kern,
        grid=(b, n_tiles),
        in_specs=in_specs,
        out_specs=[x_spec, st_spec],
        out_shape=[jax.ShapeDtypeStruct(x.shape, _F32),
                   jax.ShapeDtypeStruct((b, GLA_V, GLA_QK), _F32)],
        scratch_shapes=[
            pltpu.VMEM((tile, IN_COLS), _F32),
            pltpu.VMEM((tile + 2 * HALO, GROUP_W), _F32),
            pltpu.VMEM((tile + 2 * HALO, GROUP_W), _F32),
            pltpu.VMEM((tile, 2 * GLA_QK), _F32),
            pltpu.VMEM((tile, GLA_V), _F32),
            pltpu.VMEM((tile, 4 * GROUP_W), _BF16),
            pltpu.VMEM((GLA_V, GLA_QK), _F32),
        ],
        compiler_params=pltpu.CompilerParams(
            dimension_semantics=("arbitrary", "arbitrary"),
            vmem_limit_bytes=VMEM_LIMIT),
        name="token_mixer",
    )(x, x, x, mod, *consts, sb_start, sf0)


def _prep_mixer_weights(w_in, gla_w_gate, gla_b_gate, gla_norm_w, pool_w, pool_scale, sgu_w, sgu_b,
                        sgu_ln_w, sgu_ln_b, cm_conv_w, cm_conv_b, cm_ln_w, cm_ln_b, w_out,
                        post_w, post_b):
    d = w_in.shape[0]
    qkvr = w_in[:, 0:768]
    low = w_in[:, 768:800]
    zb = w_in[:, 800:1056]
    zc = w_in[:, 1056:1568]
    zd = w_in[:, 1568:2080]
    pad = jnp.zeros((d, LANES - 2 * GLA_RANK), _F32)
    w_main = jnp.concatenate([qkvr, low, pad, zc, zb, zd], axis=1).astype(_BF16)
    kv = w_in[:, 128:512]
    pad1 = jnp.zeros((d, LANES - GLA_RANK), _F32)
    w_state = [jnp.concatenate([kv, low[:, di * GLA_RANK:(di + 1) * GLA_RANK], pad1], axis=1).astype(_BF16)
               for di in range(2)]
    wg1 = [jnp.zeros((LANES, GLA_QK), _F32).at[0:GLA_RANK].set(gla_w_gate[di]).astype(_BF16)
           for di in range(2)]
    bg1 = [gla_b_gate[di][None, :] for di in range(2)]
    wg2 = jnp.zeros((LANES, 2 * GLA_QK), _F32)
    wg2 = wg2.at[0:GLA_RANK, 0:GLA_QK].set(gla_w_gate[0])
    wg2 = wg2.at[GLA_RANK:2 * GLA_RANK, GLA_QK:].set(gla_w_gate[1]).astype(_BF16)
    bg2 = jnp.concatenate([gla_b_gate[0], gla_b_gate[1]])[None, :]
    pool_blk = jnp.zeros((GROUP_W, GROUP_W), _F32)
    for gi in range(pool_w.shape[0]):
        sl = slice(gi * SGU_HD, (gi + 1) * SGU_HD)
        pool_blk = pool_blk.at[sl, sl].set(pool_w[gi])
    row = lambda a: a[None, :]
    return {
        "w_in": w_main, "w_state": w_state, "wg1": wg1, "bg1": bg1, "wg2": wg2, "bg2": bg2,
        "gnw": row(gla_norm_w), "pool_blk": pool_blk.astype(_BF16), "pool_scale": row(pool_scale),
        "sgu_w": jnp.concatenate([sgu_w[hd] for hd in range(SGU_HEADS)], axis=1).astype(_BF16),
        "sgu_b": jnp.repeat(sgu_b.T, SGU_HD, axis=1),
        "sgu_ln_w": row(sgu_ln_w), "sgu_ln_b": row(sgu_ln_b),
        "cm_w": jnp.concatenate([cm_conv_w, jnp.zeros((1, GROUP_W), _F32)], axis=0),
        "cm_b": row(cm_conv_b), "cm_ln_w": row(cm_ln_w), "cm_ln_b": row(cm_ln_b),
        "w_out": w_out.astype(_BF16), "post_w": row(post_w), "post_b": row(post_b),
    }


def _mod_kernel(c_ref, w_ref, b_ref, o_ref):
    cv = c_ref[...]
    o_ref[0] = jnp.dot(cv * _sigmoid(cv), w_ref[0], preferred_element_type=_F32,
                       precision=lax.Precision.HIGHEST) + b_ref[0]


def _mod_call(cond, w_mod, b_mod):
    depth, d, six_d = w_mod.shape
    rows = cond.shape[0]
    return pl.pallas_call(
        _mod_kernel,
        grid=(depth, six_d // d),
        in_specs=[
            pl.BlockSpec((rows, d), lambda li, j: (0, 0)),
            pl.BlockSpec((1, d, d), lambda li, j: (li, 0, j)),
            pl.BlockSpec((1, 1, d), lambda li, j: (li, 0, j)),
        ],
        out_specs=pl.BlockSpec((1, rows, d), lambda li, j: (li, 0, j)),
        out_shape=jax.ShapeDtypeStruct((depth, rows, six_d), _F32),
        compiler_params=pltpu.CompilerParams(
            dimension_semantics=("arbitrary", "arbitrary"),
            vmem_limit_bytes=VMEM_LIMIT),
        name="adaln_modulation",
    )(cond, w_mod, b_mod.reshape(depth, 1, six_d))


def _pick_tile(length, target):
    return min(length, target)


def kernel(x, c, ctx, c_ctx, w_mod, b_mod, w_in, gla_w_gate, gla_b_gate, gla_norm_w, pool_w, pool_scale, sgu_w, sgu_b, sgu_ln_w, sgu_ln_b, cm_conv_w, cm_conv_b, cm_ln_w, cm_ln_b, w_out, ffn_w_up, ffn_conv_w, ffn_w_down, post_ln_w, post_ln_b):
    batch, seq, d = x.shape
    ctx_len = ctx.shape[1]
    depth = w_mod.shape[0]
    alpha = (2 * depth) ** 0.25

    cond_rows = -(-(batch + 1) // SUBLANES) * SUBLANES
    cond = jnp.concatenate([c, c_ctx[None, :], jnp.zeros((cond_rows - batch - 1, d), _F32)], axis=0)
    mod = _mod_call(cond, w_mod, b_mod)

    lat_mix_tile = _pick_tile(seq, MIX_TILE)
    ctx_mix_tile = _pick_tile(ctx_len, MIX_TILE)
    lat_ffn_tile = _pick_tile(seq, FFN_TILE)
    zero_state = jnp.zeros((batch, GLA_V, GLA_QK), _F32)

    for li in range(depth):
        with_ctx_out = li < depth - 1
        mod_lat = mod[li, :batch].reshape(batch, 6, d)
        mod_ctx = jnp.broadcast_to(mod[li, batch].reshape(1, 6, d), (batch, 6, d))
        mw = _prep_mixer_weights(w_in[li], gla_w_gate[li], gla_b_gate[li], gla_norm_w[li], pool_w[li],
                                 pool_scale[li], sgu_w[li], sgu_b[li], sgu_ln_w[li], sgu_ln_b[li],
                                 cm_conv_w[li], cm_conv_b[li], cm_ln_w[li], cm_ln_b[li], w_out[li],
                                 post_ln_w[li, 0], post_ln_b[li, 0])
        ffn_w = _prep_ffn_weights(ffn_w_up[li], ffn_conv_w[li], ffn_w_down[li])
        ffn_ln = (post_ln_w[li, 1][None, :], post_ln_b[li, 1][None, :])

        ctx_sb, ctx_sb_fin = _state_call(ctx, mod_ctx, mw["w_state"][1], mw["wg1"][1], mw["bg1"][1],
                                         zero_state, tile=ctx_mix_tile, reverse=True)
        if with_ctx_out:
            ctx_mixed, ctx_sf_fin = _mixer_call(ctx, mod_ctx, mw, ctx_sb, zero_state,
                                                tile=ctx_mix_tile, alpha=alpha)
        else:
            _, ctx_sf_fin = _state_call(ctx, mod_ctx, mw["w_state"][0], mw["wg1"][0], mw["bg1"][0],
                                        zero_state, tile=ctx_mix_tile, reverse=False)

        lat_sb, _ = _state_call(x, mod_lat, mw["w_state"][1], mw["wg1"][1], mw["bg1"][1],
                                ctx_sb_fin, tile=lat_mix_tile, reverse=True)
        x, _ = _mixer_call(x, mod_lat, mw, lat_sb, ctx_sf_fin, tile=lat_mix_tile, alpha=alpha)
        x = _ffn_call(x, mod_lat, *ffn_w, *ffn_ln, tile=lat_ffn_tile, gw=GRID_W, alpha=alpha)
        if with_ctx_out:
            ctx = _ffn_call(ctx_mixed, mod_ctx, *ffn_w, *ffn_ln, tile=ctx_len, gw=ctx_len, alpha=alpha)
    return x
```

```python
import functools

import jax
import jax.numpy as jnp
from jax import lax
from jax.experimental import pallas as pl
from jax.experimental.pallas import tpu as pltpu

EPS = 1e-6
LANES = 128
SUBLANES = 8
GRID_W = 64
FFN_CHUNK = 256
FFN_GRANULE = 512
BF16_ROWS = 16
MIX_TILE = 512
FFN_TILE = 1024
VMEM_LIMIT = 60 * 1024 * 1024

_F32 = jnp.float32
_BF16 = jnp.bfloat16


def _norm_rows(x):
    mu = jnp.mean(x, axis=-1, keepdims=True)
    xc = x - mu
    var = jnp.mean(xc * xc, axis=-1, keepdims=True)
    return xc * lax.rsqrt(var + EPS)


def _sigmoid(x):
    return 1.0 / (1.0 + jnp.exp(-x))


def _const_spec(shape):
    nd = len(shape)
    return pl.BlockSpec(shape, lambda *_: (0,) * nd, pipeline_mode=pl.Buffered(1))


def _ffn_kernel(*refs, tile, gw, n_tiles, n_gran, has_halo, alpha):
    if has_halo:
        (xm_ref, xp_ref, xn_ref, mod_ref, wup_ref, wcv_ref, wdn_ref, lnw_ref, lnb_ref,
         o_ref, h_scr, u0_scr, u1_scr, s_scr, acc_scr) = refs
    else:
        (xm_ref, mod_ref, wup_ref, wcv_ref, wdn_ref, lnw_ref, lnb_ref,
         o_ref, h_scr, u0_scr, u1_scr, s_scr, acc_scr) = refs
    u_bufs = (u0_scr, u1_scr)
    i = pl.program_id(1)
    n_chunks = wup_ref.shape[0]
    pad = gw if has_halo else 0
    n_rows = tile // gw

    shift = mod_ref[0, 3:4, :]
    scale1 = 1.0 + mod_ref[0, 4:5, :]
    gate = mod_ref[0, 5:6, :]

    def modulated(xv):
        return _norm_rows(xv) * scale1 + shift

    h_scr[pad:pad + tile, :] = modulated(xm_ref[0]).astype(_BF16)
    if has_halo:
        hp = jnp.where(i > 0, modulated(xp_ref[0]), 0.0)
        h_scr[0:gw, :] = hp.astype(_BF16)
        hn = jnp.where(i < n_tiles - 1, modulated(xn_ref[0]), 0.0)
        h_scr[pad + tile:pad + tile + gw, :] = hn.astype(_BF16)

    sub_id = lax.broadcasted_iota(jnp.int32, (SUBLANES, LANES), 0)

    def zero_row(v, row):
        lo = (row // SUBLANES) * SUBLANES
        slab = jnp.where(sub_id == row - lo, 0.0, v[lo:lo + SUBLANES])
        parts = [p for p in (v[:lo], slab, v[lo + SUBLANES:]) if p.shape[0]]
        return jnp.concatenate(parts, axis=0)

    ext = h_scr.shape[0]
    up_rows = ext // n_gran
    rows_per_gran = n_rows // n_gran

    gran_rows = rows_per_gran * gw

    def up_proj(c, buf, g):
        rows = pl.ds(pl.multiple_of(g * up_rows, BF16_ROWS), up_rows)
        u_bufs[buf][rows, :] = jnp.dot(h_scr[rows, :], wup_ref[c], preferred_element_type=_F32)

    def conv_cols(ub, wc, base, lo):
        def rows(off):
            return ub[pl.ds(base + off, gw), lo:lo + LANES]

        def taps(dc):
            k = dc + 1
            v = wc[3 + k:4 + k, lo:lo + LANES] * rows(pad)
            if has_halo:
                v = v + wc[k:k + 1, lo:lo + LANES] * rows(0)
                v = v + wc[6 + k:7 + k, lo:lo + LANES] * rows(2 * gw)
            return v
        left = zero_row(pltpu.roll(taps(-1), 1, axis=0), 0)
        right = zero_row(pltpu.roll(taps(1), gw - 1, axis=0), gw - 1)
        return taps(0) + left + right

    def conv_gate(c, buf, g):
        ub = u_bufs[buf]
        wc = wcv_ref[c]
        for r in range(rows_per_gran):
            base = pl.multiple_of(g * gran_rows + r * gw, gw)
            for j in range(FFN_CHUNK // LANES):
                a = conv_cols(ub, wc, base, j * LANES)
                gt = conv_cols(ub, wc, base, FFN_CHUNK + j * LANES)
                s_scr[pl.ds(base, gw), j * LANES:(j + 1) * LANES] = (a * gt * _sigmoid(gt)).astype(_BF16)

    def down_proj(c, g):
        rows = pl.ds(pl.multiple_of(g * gran_rows, gran_rows), gran_rows)
        acc_scr[rows, :] += jnp.dot(s_scr[rows, :], wdn_ref[c], preferred_element_type=_F32)

    def chunk_steps(c, buf, has_next):
        def step(g, carry):
            if has_next:
                up_proj(c + 1, 1 - buf, g)
            wrap = (g == 0).astype(jnp.int32)
            down_proj(jnp.maximum(c - wrap, 0), g - 1 + wrap * n_gran)
            conv_gate(c, buf, g)
            return carry
        lax.fori_loop(0, n_gran, step, 0)

    acc_scr[...] = jnp.zeros_like(acc_scr)
    s_scr[...] = jnp.zeros_like(s_scr)
    lax.fori_loop(0, n_gran, lambda g, carry: (up_proj(0, 0, g), carry)[1], 0)

    def body(k, carry):
        c = 2 * k
        chunk_steps(c, 0, True)
        chunk_steps(c + 1, 1, True)
        return carry

    n_pairs = (n_chunks - 1) // 2
    lax.fori_loop(0, n_pairs, body, 0)
    if n_chunks % 2 == 0:
        chunk_steps(n_chunks - 2, 0, True)
    chunk_steps(n_chunks - 1, (n_chunks - 1) % 2, False)
    down_proj(n_chunks - 1, n_gran - 1)

    y = alpha * xm_ref[0] + gate * acc_scr[...]
    o_ref[0] = _norm_rows(y) * lnw_ref[...] + lnb_ref[...]


def _ffn_call(x, mod, wup, wcv, wdn, lnw, lnb, *, tile, gw, alpha):
    b, l, d = x.shape
    n_tiles = l // tile
    has_halo = l > gw
    rows_per_tile = tile // gw
    n_grid_rows = l // gw
    ext = tile + (2 * gw if has_halo else 0)

    x_spec = pl.BlockSpec((1, tile, d), lambda bi, i: (bi, i, 0))
    in_specs = [x_spec]
    args = [x]
    if has_halo:
        in_specs.append(pl.BlockSpec(
            (1, gw, d), lambda bi, i: (bi, jnp.maximum(i * rows_per_tile - 1, 0), 0)))
        in_specs.append(pl.BlockSpec(
            (1, gw, d), lambda bi, i: (bi, jnp.minimum((i + 1) * rows_per_tile, n_grid_rows - 1), 0)))
        args += [x, x]
    in_specs += [
        pl.BlockSpec((1, 6, d), lambda bi, i: (bi, 0, 0)),
        _const_spec(wup.shape), _const_spec(wcv.shape), _const_spec(wdn.shape),
        _const_spec(lnw.shape), _const_spec(lnb.shape),
    ]
    args += [mod, wup, wcv, wdn, lnw, lnb]
    n_gran = max(1, tile // FFN_GRANULE)
    assert ext % (n_gran * BF16_ROWS) == 0 and rows_per_tile % n_gran == 0
    kern = functools.partial(_ffn_kernel, tile=tile, gw=gw, n_tiles=n_tiles, n_gran=n_gran,
                             has_halo=has_halo, alpha=alpha)
    return pl.pallas_call(
        kern,
        grid=(b, n_tiles),
        in_specs=in_specs,
        out_specs=x_spec,
        out_shape=jax.ShapeDtypeStruct(x.shape, _F32),
        scratch_shapes=[
            pltpu.VMEM((ext, d), _BF16),
            pltpu.VMEM((ext, 2 * FFN_CHUNK), _F32),
            pltpu.VMEM((ext, 2 * FFN_CHUNK), _F32),
            pltpu.VMEM((tile, FFN_CHUNK), _BF16),
            pltpu.VMEM((tile, d), _F32),
        ],
        compiler_params=pltpu.CompilerParams(
            dimension_semantics=("arbitrary", "arbitrary"),
            vmem_limit_bytes=VMEM_LIMIT),
        name="channel_mixer",
    )(*args)


def _prep_ffn_weights(w_up, conv_w, w_down):
    d, two_h = w_up.shape
    hid = two_h // 2
    n_chunks = hid // FFN_CHUNK
    wa = w_up[:, :hid].reshape(d, n_chunks, FFN_CHUNK)
    wg = w_up[:, hid:].reshape(d, n_chunks, FFN_CHUNK)
    wup = jnp.concatenate([wa, wg], axis=-1).transpose(1, 0, 2).astype(_BF16)
    cw = conv_w.reshape(9, two_h)
    ca = cw[:, :hid].reshape(9, n_chunks, FFN_CHUNK)
    cg = cw[:, hid:].reshape(9, n_chunks, FFN_CHUNK)
    wcv = jnp.concatenate([ca, cg], axis=-1).transpose(1, 0, 2)
    wdn = w_down.reshape(n_chunks, FFN_CHUNK, -1).astype(_BF16)
    return wup, wcv, wdn


GLA_HEADS = 4
GLA_DK = 32
GLA_DV = 64
GLA_QK = GLA_HEADS * GLA_DK
GLA_V = GLA_HEADS * GLA_DV
GLA_RANK = 16
GLA_CHUNK = 64
GLA_GATE_NORMALIZER = 16.0
GLA_LOG_DECAY_MIN = -1.0


def _log_decay(logit):
    log_sig = jnp.minimum(logit, 0.0) - jnp.log(1.0 + jnp.exp(-jnp.abs(logit)))
    return jnp.maximum(log_sig / GLA_GATE_NORMALIZER, GLA_LOG_DECAY_MIN)


def _chunk_cumsum(x, reverse):
    n = x.shape[0]
    pos = lax.broadcasted_iota(jnp.int32, x.shape, 0) % GLA_CHUNK
    s = 1
    while s < GLA_CHUNK:
        if reverse:
            x = x + jnp.where(pos < GLA_CHUNK - s, pltpu.roll(x, n - s, axis=0), 0.0)
        else:
            x = x + jnp.where(pos >= s, pltpu.roll(x, s, axis=0), 0.0)
        s *= 2
    return x


def _state_mask():
    r = lax.broadcasted_iota(jnp.int32, (GLA_V, GLA_QK), 0) // GLA_DV
    c = lax.broadcasted_iota(jnp.int32, (GLA_V, GLA_QK), 1) // GLA_DK
    return r == c


def _stack4(a):
    return jnp.concatenate([a, a, a, a], axis=0)


def _compact_state(s):
    return (s[0:GLA_DV] + s[GLA_DV:2 * GLA_DV]) + (s[2 * GLA_DV:3 * GLA_DV] + s[3 * GLA_DV:4 * GLA_DV])


def _state_update(state, v_bf, k_dec_bf, decay_row, mask):
    kv_t = lax.dot_general(v_bf, k_dec_bf, (((0,), (0,)), ((), ())), preferred_element_type=_F32)
    return state * decay_row + jnp.where(mask, kv_t, 0.0)


def _state_kernel(x_ref, mod_ref, w_ref, wg_ref, bg_ref, s0_ref, start_ref, fin_ref, st_scr,
                  *, tile, reverse):
    i = pl.program_id(1)
    n_tiles = pl.num_programs(1)
    chunks = tile // GLA_CHUNK

    @pl.when(i == 0)
    def _():
        st_scr[...] = s0_ref[0]

    shift = mod_ref[0, 0:1, :]
    scale1 = 1.0 + mod_ref[0, 1:2, :]
    h = (_norm_rows(x_ref[0]) * scale1 + shift).astype(_BF16)
    z = jnp.dot(h, w_ref[...], preferred_element_type=_F32)
    k = z[:, 0:GLA_QK]
    v_bf = z[:, GLA_QK:GLA_QK + GLA_V].astype(_BF16)
    low = z[:, GLA_QK + GLA_V:].astype(_BF16)
    g = _log_decay(jnp.dot(low, wg_ref[...], preferred_element_type=_F32) + bg_ref[...])
    run = _chunk_cumsum(g, reverse)
    mask = _state_mask()
    state = st_scr[...]
    order = range(chunks - 1, -1, -1) if reverse else range(chunks)
    for ci in order:
        rows = slice(ci * GLA_CHUNK, (ci + 1) * GLA_CHUNK)
        edge = ci * GLA_CHUNK if reverse else (ci + 1) * GLA_CHUNK - 1
        total = run[edge:edge + 1, :]
        k_dec = (k[rows] * jnp.exp(total - run[rows])).astype(_BF16)
        start_ref[0, ci] = _compact_state(state)
        state = _state_update(state, v_bf[rows], k_dec, jnp.exp(total), mask)
    st_scr[...] = state

    @pl.when(i == n_tiles - 1)
    def _():
        fin_ref[0] = state


def _state_call(x, mod, w, wg, bg, s0, *, tile, reverse):
    b, l, d = x.shape
    n_tiles = l // tile
    chunks = tile // GLA_CHUNK
    tmap = (lambda bi, i: (bi, n_tiles - 1 - i, 0)) if reverse else (lambda bi, i: (bi, i, 0))
    smap = (lambda bi, i: (bi, n_tiles - 1 - i, 0, 0)) if reverse else (lambda bi, i: (bi, i, 0, 0))
    st_spec = pl.BlockSpec((1, GLA_V, GLA_QK), lambda bi, i: (bi, 0, 0))
    kern = functools.partial(_state_kernel, tile=tile, reverse=reverse)
    return pl.pallas_call(
        kern,
        grid=(b, n_tiles),
        in_specs=[
            pl.BlockSpec((1, tile, d), tmap),
            pl.BlockSpec((1, 6, d), lambda bi, i: (bi, 0, 0)),
            _const_spec(w.shape), _const_spec(wg.shape), _const_spec(bg.shape),
            st_spec,
        ],
        out_specs=[
            pl.BlockSpec((1, chunks, GLA_DV, GLA_QK), smap),
            st_spec,
        ],
        out_shape=[
            jax.ShapeDtypeStruct((b, l // GLA_CHUNK, GLA_DV, GLA_QK), _F32),
            jax.ShapeDtypeStruct((b, GLA_V, GLA_QK), _F32),
        ],
        scratch_shapes=[pltpu.VMEM((GLA_V, GLA_QK), _F32)],
        compiler_params=pltpu.CompilerParams(
            dimension_semantics=("arbitrary", "arbitrary"),
            vmem_limit_bytes=VMEM_LIMIT),
        name="gla_state_rev" if reverse else "gla_state_fwd",
    )(x, mod, w, wg, bg, s0)


GROUP_W = 256
COL_Q, COL_K, COL_V, COL_R, COL_LOW, COL_C, COL_B, COL_D = 0, 128, 256, 512, 768, 896, 1408, 1664
IN_COLS = 2176
HALO = 16
POOL_REACH = 8
MIX_BLOCK = 128
CONV_WIDTH = 31
SGU_HEADS = 4
SGU_HD = 64


def _gelu_tanh(x):
    return 0.5 * x * (1.0 + jnp.tanh(0.7978845608028654 * (x + 0.044715 * (x * x * x))))


def _mixer_kernel(xm_ref, xp_ref, xn_ref, mod_ref, win_ref, wg_ref, bg_ref, gnw_ref,
                  pblk_ref, pscale_ref, sguw_ref, sgub_ref, slnw_ref, slnb_ref,
                  cw_ref, cb_ref, clnw_ref, clnb_ref, wout_ref, plw_ref, plb_ref,
                  sb_ref, sf0_ref,
                  o_ref, sfin_ref,
                  z_scr, zb_scr, y_scr, run_scr, o_scr, cat_scr, st_scr,
                  *, tile, seq_len, alpha):
    i = pl.program_id(1)
    n_tiles = pl.num_programs(1)
    n_blocks = tile // MIX_BLOCK

    @pl.when(i == 0)
    def _():
        st_scr[...] = sf0_ref[0]

    shift = mod_ref[0, 0:1, :]
    scale1 = 1.0 + mod_ref[0, 1:2, :]
    gate = mod_ref[0, 2:3, :]

    def modulated(xv):
        return _norm_rows(xv) * scale1 + shift

    hp = jnp.where(i > 0, modulated(xp_ref[0]), 0.0).astype(_BF16)
    hn = jnp.where(i < n_tiles - 1, modulated(xn_ref[0]), 0.0).astype(_BF16)
    zh = jnp.dot(jnp.concatenate([hp, hn], axis=0), win_ref[:, COL_B:],
                 preferred_element_type=_F32)

    def glu(zd):
        return zd[:, :GROUP_W] * _sigmoid(zd[:, GROUP_W:])

    zb_scr[0:HALO, :] = zh[0:HALO, 0:GROUP_W]
    zb_scr[HALO + tile:, :] = zh[HALO:, 0:GROUP_W]
    y_scr[0:HALO, :] = glu(zh[0:HALO, GROUP_W:])
    y_scr[HALO + tile:, :] = glu(zh[HALO:, GROUP_W:])

    def block_rows(blk):
        return slice(blk * MIX_BLOCK, (blk + 1) * MIX_BLOCK)

    def in_proj(blk):
        rows = block_rows(blk)
        ext_rows = slice(HALO + blk * MIX_BLOCK, HALO + (blk + 1) * MIX_BLOCK)
        h = modulated(xm_ref[0, rows, :]).astype(_BF16)
        z = jnp.dot(h, win_ref[...], preferred_element_type=_F32)
        z_scr[rows, :] = z
        zb_scr[ext_rows, :] = z[:, COL_B:COL_B + GROUP_W]
        y_scr[ext_rows, :] = glu(z[:, COL_D:COL_D + 2 * GROUP_W])
        low = z[:, COL_LOW:COL_LOW + LANES].astype(_BF16)
        g = _log_decay(jnp.dot(low, wg_ref[...], preferred_element_type=_F32) + bg_ref[...])
        run_scr[rows, 0:GLA_QK] = _chunk_cumsum(g[:, 0:GLA_QK], False)
        run_scr[rows, GLA_QK:] = _chunk_cumsum(g[:, GLA_QK:], True)

    lane = lax.broadcasted_iota(jnp.int32, (MIX_BLOCK, GROUP_W), 1)
    row = lax.broadcasted_iota(jnp.int32, (MIX_BLOCK, GROUP_W), 0)
    lane_group = lane // SGU_HD
    pool_half = jnp.left_shift(1, lane_group)

    def local_mixers(blk):
        r0 = blk * MIX_BLOCK
        rows = block_rows(blk)

        n = MIX_BLOCK + 2 * POOL_REACH
        xb = zb_scr[r0 + HALO - POOL_REACH:r0 + HALO - POOL_REACH + n, :]
        p2 = xb + pltpu.roll(xb, 1, axis=0)
        p4 = pltpu.roll(p2, 1, axis=0) + pltpu.roll(p2, n - 1, axis=0)
        p8 = pltpu.roll(p4, 2, axis=0) + pltpu.roll(p4, n - 2, axis=0)
        p16 = pltpu.roll(p8, 4, axis=0) + pltpu.roll(p8, n - 4, axis=0)
        inner = slice(POOL_REACH, POOL_REACH + MIX_BLOCK)
        win = jnp.where(lane_group == 0, p2[inner],
                        jnp.where(lane_group == 1, p4[inner],
                                  jnp.where(lane_group == 2, p8[inner], p16[inner])))
        tok = i * tile + r0 + row
        count = jnp.minimum(tok + pool_half, seq_len) - jnp.maximum(tok - pool_half, 0)
        pooled = win / count.astype(_F32) - xb[inner]
        yb = jnp.dot(pooled.astype(_BF16), pblk_ref[...], preferred_element_type=_F32) * pscale_ref[...]
        cat_scr[rows, GROUP_W:2 * GROUP_W] = yb.astype(_BF16)

        gz = _gelu_tanh(z_scr[rows, COL_C:COL_C + 2 * GROUP_W])
        vn = _norm_rows(gz[:, GROUP_W:]) * slnw_ref[...] + slnb_ref[...]
        vstack = jnp.concatenate(
            [jnp.where(lane_group == hd, vn, 0.0) for hd in range(SGU_HEADS)], axis=0).astype(_BF16)
        sg = jnp.dot(sguw_ref[...], vstack, preferred_element_type=_F32) + sgub_ref[...]
        cat_scr[rows, 2 * GROUP_W:3 * GROUP_W] = (gz[:, :GROUP_W] * sg).astype(_BF16)

        span = MIX_BLOCK + SUBLANES
        conv = None
        for b in range(SUBLANES):
            part = None
            for a in range(4):
                o = SUBLANES * a + b
                if 1 <= o <= CONV_WIDTH:
                    term = cw_ref[o - 1:o, :] * y_scr[r0 + SUBLANES * a:r0 + SUBLANES * a + span, :]
                    part = term if part is None else part + term
            part = part[b:b + MIX_BLOCK]
            conv = part if conv is None else conv + part
        yd = _norm_rows(conv + cb_ref[...]) * clnw_ref[...] + clnb_ref[...]
        cat_scr[rows, 3 * GROUP_W:4 * GROUP_W] = (yd * _sigmoid(yd)).astype(_BF16)

    mask_s = _state_mask()
    vr = lax.broadcasted_iota(jnp.int32, (GLA_V, GLA_V), 0) // GLA_DV
    vc = lax.broadcasted_iota(jnp.int32, (GLA_V, GLA_V), 1) // GLA_DV
    mask_v = vr == vc
    t_id = lax.broadcasted_iota(jnp.int32, (GLA_CHUNK, GLA_V), 0)
    s_id = lax.broadcasted_iota(jnp.int32, (GLA_CHUNK, GLA_V), 1) % GLA_CHUNK
    nt_dims = (((1,), (1,)), ((), ()))

    zero = jnp.zeros((), _BF16)

    def chunk_rows(ci):
        return slice(ci * GLA_CHUNK, (ci + 1) * GLA_CHUNK)

    def scores(ci):
        r = chunk_rows(ci)
        q = z_scr[r, COL_Q:COL_Q + GLA_QK] * (GLA_DK ** -0.5)
        k = z_scr[r, COL_K:COL_K + GLA_QK]
        cf = run_scr[r, 0:GLA_QK]
        rb = run_scr[r, GLA_QK:]
        qf = (q * jnp.exp(cf)).astype(_BF16)
        kf = (k * jnp.exp(-cf)).astype(_BF16)
        qb = (q * jnp.exp(rb)).astype(_BF16)
        kb = (k * jnp.exp(-rb)).astype(_BF16)
        a_f = lax.dot_general(qf, jnp.where(mask_s, _stack4(kf), zero), nt_dims,
                              preferred_element_type=_F32)
        a_b = lax.dot_general(qb, jnp.where(mask_s, _stack4(kb), zero), nt_dims,
                              preferred_element_type=_F32)
        a = (jnp.where(s_id <= t_id, a_f, 0.0) + jnp.where(s_id >= t_id, a_b, 0.0)).astype(_BF16)
        return a, qf, qb

    def outputs(ci, a, qf, qb):
        r = chunk_rows(ci)
        k = z_scr[r, COL_K:COL_K + GLA_QK]
        v = z_scr[r, COL_V:COL_V + GLA_V].astype(_BF16)
        cf = run_scr[r, 0:GLA_QK]
        total = cf[GLA_CHUNK - 1:GLA_CHUNK, :]
        k_dec = (k * jnp.exp(total - cf)).astype(_BF16)
        o = jnp.dot(a, jnp.where(mask_v, _stack4(v), zero), preferred_element_type=_F32)
        sf = st_scr[...]
        sb = jnp.where(mask_s, _stack4(sb_ref[0, ci]), 0.0)
        o = o + lax.dot_general(qf, sf.astype(_BF16), nt_dims, preferred_element_type=_F32)
        o = o + lax.dot_general(qb, sb.astype(_BF16), nt_dims, preferred_element_type=_F32)
        o_scr[r, :] = o
        st_scr[...] = _state_update(sf, v, k_dec, jnp.exp(total), mask_s)

    head_mean = jnp.where(mask_v, 1.0 / GLA_DV, 0.0).astype(_BF16)

    def readout(blk):
        rows = block_rows(blk)
        o = o_scr[rows, :]
        sq = o * o
        sq_hi = sq.astype(_BF16)
        sq_lo = (sq - sq_hi.astype(_F32)).astype(_BF16)
        ms = (jnp.dot(sq_hi, head_mean, preferred_element_type=_F32)
              + jnp.dot(sq_lo, head_mean, preferred_element_type=_F32))
        rg = z_scr[rows, COL_R:COL_R + GROUP_W]
        ya = o * lax.rsqrt(ms + EPS) * gnw_ref[...] * (rg * _sigmoid(rg))
        cat_scr[rows, 0:GROUP_W] = ya.astype(_BF16)

    def out_proj(blk):
        rows = block_rows(blk)
        y = jnp.dot(cat_scr[rows, :], wout_ref[...], preferred_element_type=_F32)
        o_ref[0, rows, :] = (_norm_rows(alpha * xm_ref[0, rows, :] + gate * y) * plw_ref[...]
                             + plb_ref[...])

    chunks_per_block = MIX_BLOCK // GLA_CHUNK
    n_gla_chunks = tile // GLA_CHUNK
    in_proj(0)
    if n_blocks > 1:
        in_proj(1)
    cur = scores(0)
    for blk in range(n_blocks):
        if blk + 2 < n_blocks:
            in_proj(blk + 2)
        local_mixers(blk)
        for ci in range(blk * chunks_per_block, (blk + 1) * chunks_per_block):
            nxt = scores(ci + 1) if ci + 1 < n_gla_chunks else None
            outputs(ci, *cur)
            cur = nxt
        readout(blk)
        out_proj(blk)

    @pl.when(i == n_tiles - 1)
    def _():
        sfin_ref[0] = st_scr[...]


def _mixer_call(x, mod, mw, sb_start, sf0, *, tile, alpha):
    b, l, d = x.shape
    n_tiles = l // tile
    halo_per_tile = tile // HALO
    n_halo_blocks = l // HALO
    consts = [mw[k] for k in ("w_in", "wg2", "bg2", "gnw", "pool_blk", "pool_scale", "sgu_w", "sgu_b",
                              "sgu_ln_w", "sgu_ln_b", "cm_w", "cm_b", "cm_ln_w", "cm_ln_b", "w_out",
                              "post_w", "post_b")]
    x_spec = pl.BlockSpec((1, tile, d), lambda bi, i: (bi, i, 0))
    st_spec = pl.BlockSpec((1, GLA_V, GLA_QK), lambda bi, i: (bi, 0, 0))
    in_specs = [
        x_spec,
        pl.BlockSpec((1, HALO, d), lambda bi, i: (bi, jnp.maximum(i * halo_per_tile - 1, 0), 0)),
        pl.BlockSpec((1, HALO, d), lambda bi, i: (bi, jnp.minimum((i + 1) * halo_per_tile, n_halo_blocks - 1), 0)),
        pl.BlockSpec((1, 6, d), lambda bi, i: (bi, 0, 0)),
    ] + [_const_spec(c.shape) for c in consts] + [
        pl.BlockSpec((1, tile // GLA_CHUNK, GLA_DV, GLA_QK), lambda bi, i: (bi, i, 0, 0)),
        st_spec,
    ]
    kern = functools.partial(_mixer_kernel, tile=tile, seq_len=l, alpha=alpha)
    return pl.pallas_call(
        kern,
        grid=(b, n_tiles),
        in_specs=in_specs,
        out_specs=[x_spec, st_spec],
        out_shape=[jax.ShapeDtypeStruct(x.shape, _F32),
                   jax.ShapeDtypeStruct((b, GLA_V, GLA_QK), _F32)],
        scratch_shapes=[
            pltpu.VMEM((tile, IN_COLS), _F32),
            pltpu.VMEM((tile + 2 * HALO, GROUP_W), _F32),
            pltpu.VMEM((tile + 2 * HALO, GROUP_W), _F32),
            pltpu.VMEM((tile, 2 * GLA_QK), _F32),
            pltpu.VMEM((tile, GLA_V), _F32),
            pltpu.VMEM((tile, 4 * GROUP_W), _BF16),
            pltpu.VMEM((GLA_V, GLA_QK), _F32),
        ],
        compiler_params=pltpu.CompilerParams(
            dimension_semantics=("arbitrary", "arbitrary"),
            vmem_limit_bytes=VMEM_LIMIT),
        name="token_mixer",
    )(x, x, x, mod, *consts, sb_start, sf0)


def _prep_mixer_weights(w_in, gla_w_gate, gla_b_gate, gla_norm_w, pool_w, pool_scale, sgu_w, sgu_b,
                        sgu_ln_w, sgu_ln_b, cm_conv_w, cm_conv_b, cm_ln_w, cm_ln_b, w_out,
                        post_w, post_b):
    d = w_in.shape[0]
    qkvr = w_in[:, 0:768]
    low = w_in[:, 768:800]
    zb = w_in[:, 800:1056]
    zc = w_in[:, 1056:1568]
    zd = w_in[:, 1568:2080]
    pad = jnp.zeros((d, LANES - 2 * GLA_RANK), _F32)
    w_main = jnp.concatenate([qkvr, low, pad, zc, zb, zd], axis=1).astype(_BF16)
    kv = w_in[:, 128:512]
    pad1 = jnp.zeros((d, LANES - GLA_RANK), _F32)
    w_state = [jnp.concatenate([kv, low[:, di * GLA_RANK:(di + 1) * GLA_RANK], pad1], axis=1).astype(_BF16)
               for di in range(2)]
    wg1 = [jnp.zeros((LANES, GLA_QK), _F32).at[0:GLA_RANK].set(gla_w_gate[di]).astype(_BF16)
           for di in range(2)]
    bg1 = [gla_b_gate[di][None, :] for di in range(2)]
    wg2 = jnp.zeros((LANES, 2 * GLA_QK), _F32)
    wg2 = wg2.at[0:GLA_RANK, 0:GLA_QK].set(gla_w_gate[0])
    wg2 = wg2.at[GLA_RANK:2 * GLA_RANK, GLA_QK:].set(gla_w_gate[1]).astype(_BF16)
    bg2 = jnp.concatenate([gla_b_gate[0], gla_b_gate[1]])[None, :]
    pool_blk = jnp.zeros((GROUP_W, GROUP_W), _F32)
    for gi in range(pool_w.shape[0]):
        sl = slice(gi * SGU_HD, (gi + 1) * SGU_HD)
        pool_blk = pool_blk.at[sl, sl].set(pool_w[gi])
    row = lambda a: a[None, :]
    return {
        "w_in": w_main, "w_state": w_state, "wg1": wg1, "bg1": bg1, "wg2": wg2, "bg2": bg2,
        "gnw": row(gla_norm_w), "pool_blk": pool_blk.astype(_BF16), "pool_scale": row(pool_scale),
        "sgu_w": jnp.concatenate([sgu_w[hd] for hd in range(SGU_HEADS)], axis=1).astype(_BF16),
        "sgu_b": jnp.repeat(sgu_b.T, SGU_HD, axis=1),
        "sgu_ln_w": row(sgu_ln_w), "sgu_ln_b": row(sgu_ln_b),
        "cm_w": jnp.concatenate([cm_conv_w, jnp.zeros((1, GROUP_W), _F32)], axis=0),
        "cm_b": row(cm_conv_b), "cm_ln_w": row(cm_ln_w), "cm_ln_b": row(cm_ln_b),
        "w_out": w_out.astype(_BF16), "post_w": row(post_w), "post_b": row(post_b),
    }


def _mod_kernel(c_ref, w_ref, b_ref, o_ref):
    cv = c_ref[...]
    o_ref[0] = jnp.dot(cv * _sigmoid(cv), w_ref[0], preferred_element_type=_F32,
                       precision=lax.Precision.HIGHEST) + b_ref[0]


def _mod_call(cond, w_mod, b_mod):
    depth, d, six_d = w_mod.shape
    rows = cond.shape[0]
    return pl.pallas_call(
        _mod_kernel,
        grid=(depth, six_d // d),
        in_specs=[
            pl.BlockSpec((rows, d), lambda li, j: (0, 0)),
            pl.BlockSpec((1, d, d), lambda li, j: (li, 0, j)),
            pl.BlockSpec((1, 1, d), lambda li, j: (li, 0, j)),
        ],
        out_specs=pl.BlockSpec((1, rows, d), lambda li, j: (li, 0, j)),
        out_shape=jax.ShapeDtypeStruct((depth, rows, six_d), _F32),
        compiler_params=pltpu.CompilerParams(
            dimension_semantics=("arbitrary", "arbitrary"),
            vmem_limit_bytes=VMEM_LIMIT),
        name="adaln_modulation",
    )(cond, w_mod, b_mod.reshape(depth, 1, six_d))


def _pick_tile(length, target):
    return min(length, target)


def kernel(x, c, ctx, c_ctx, w_mod, b_mod, w_in, gla_w_gate, gla_b_gate, gla_norm_w, pool_w, pool_scale, sgu_w, sgu_b, sgu_ln_w, sgu_ln_b, cm_conv_w, cm_conv_b, cm_ln_w, cm_ln_b, w_out, ffn_w_up, ffn_conv_w, ffn_w_down, post_ln_w, post_ln_b):
    batch, seq, d = x.shape
    ctx_len = ctx.shape[1]
    depth = w_mod.shape[0]
    alpha = (2 * depth) ** 0.25

    cond_rows = -(-(batch + 1) // SUBLANES) * SUBLANES
    cond = jnp.concatenate([c, c_ctx[None, :], jnp.zeros((cond_rows - batch - 1, d), _F32)], axis=0)
    mod = _mod_call(cond, w_mod, b_mod)

    lat_mix_tile = _pick_tile(seq, MIX_TILE)
    ctx_mix_tile = _pick_tile(ctx_len, MIX_TILE)
    lat_ffn_tile = _pick_tile(seq, FFN_TILE)
    zero_state = jnp.zeros((batch, GLA_V, GLA_QK), _F32)

    for li in range(depth):
        with_ctx_out = li < depth - 1
        mod_lat = mod[li, :batch].reshape(batch, 6, d)
        mod_ctx = jnp.broadcast_to(mod[li, batch].reshape(1, 6, d), (batch, 6, d))
        mw = _prep_mixer_weights(w_in[li], gla_w_gate[li], gla_b_gate[li], gla_norm_w[li], pool_w[li],
                                 pool_scale[li], sgu_w[li], sgu_b[li], sgu_ln_w[li], sgu_ln_b[li],
                                 cm_conv_w[li], cm_conv_b[li], cm_ln_w[li], cm_ln_b[li], w_out[li],
                                 post_ln_w[li, 0], post_ln_b[li, 0])
        ffn_w = _prep_ffn_weights(ffn_w_up[li], ffn_conv_w[li], ffn_w_down[li])
        ffn_ln = (post_ln_w[li, 1][None, :], post_ln_b[li, 1][None, :])

        ctx_sb, ctx_sb_fin = _state_call(ctx, mod_ctx, mw["w_state"][1], mw["wg1"][1], mw["bg1"][1],
                                         zero_state, tile=ctx_mix_tile, reverse=True)
        if with_ctx_out:
            ctx_mixed, ctx_sf_fin = _mixer_call(ctx, mod_ctx, mw, ctx_sb, zero_state,
                                                tile=ctx_mix_tile, alpha=alpha)
        else:
            _, ctx_sf_fin = _state_call(ctx, mod_ctx, mw["w_state"][0], mw["wg1"][0], mw["bg1"][0],
                                        zero_state, tile=ctx_mix_tile, reverse=False)

        lat_sb, _ = _state_call(x, mod_lat, mw["w_state"][1], mw["wg1"][1], mw["bg1"][1],
                                ctx_sb_fin, tile=lat_mix_tile, reverse=True)
        x, _ = _mixer_call(x, mod_lat, mw, lat_sb, ctx_sf_fin, tile=lat_mix_tile, alpha=alpha)
        x = _ffn_call(x, mod_lat, *ffn_w, *ffn_ln, tile=lat_ffn_tile, gw=GRID_W, alpha=alpha)
        if with_ctx_out:
            ctx = _ffn_call(ctx_mixed, mod_ctx, *ffn_w, *ffn_ln, tile=ctx_len, gw=ctx_len, alpha=alpha)
    return x
```

```python
import functools

import jax
import jax.numpy as jnp
from jax import lax
from jax.experimental import pallas as pl
from jax.experimental.pallas import tpu as pltpu

EPS = 1e-6
LANES = 128
SUBLANES = 8
GRID_W = 64
FFN_CHUNK = 256
FFN_GRANULE = 512
BF16_ROWS = 16
MIX_TILE = 512
FFN_TILE = 1024
VMEM_LIMIT = 60 * 1024 * 1024

_F32 = jnp.float32
_BF16 = jnp.bfloat16


def _norm_rows(x):
    mu = jnp.mean(x, axis=-1, keepdims=True)
    xc = x - mu
    var = jnp.mean(xc * xc, axis=-1, keepdims=True)
    return xc * lax.rsqrt(var + EPS)


def _sigmoid(x):
    return 1.0 / (1.0 + jnp.exp(-x))


def _const_spec(shape):
    nd = len(shape)
    return pl.BlockSpec(shape, lambda *_: (0,) * nd, pipeline_mode=pl.Buffered(1))


def _ffn_kernel(*refs, tile, gw, n_tiles, n_gran, has_halo, alpha):
    if has_halo:
        (xm_ref, xp_ref, xn_ref, mod_ref, wup_ref, wcv_ref, wdn_ref, lnw_ref, lnb_ref,
         o_ref, h_scr, u0_scr, u1_scr, s_scr, acc_scr) = refs
    else:
        (xm_ref, mod_ref, wup_ref, wcv_ref, wdn_ref, lnw_ref, lnb_ref,
         o_ref, h_scr, u0_scr, u1_scr, s_scr, acc_scr) = refs
    u_bufs = (u0_scr, u1_scr)
    i = pl.program_id(1)
    n_chunks = wup_ref.shape[0]
    pad = gw if has_halo else 0
    n_rows = tile // gw

    shift = mod_ref[0, 3:4, :]
    scale1 = 1.0 + mod_ref[0, 4:5, :]
    gate = mod_ref[0, 5:6, :]

    def modulated(xv):
        return _norm_rows(xv) * scale1 + shift

    h_scr[pad:pad + tile, :] = modulated(xm_ref[0]).astype(_BF16)
    if has_halo:
        hp = jnp.where(i > 0, modulated(xp_ref[0]), 0.0)
        h_scr[0:gw, :] = hp.astype(_BF16)
        hn = jnp.where(i < n_tiles - 1, modulated(xn_ref[0]), 0.0)
        h_scr[pad + tile:pad + tile + gw, :] = hn.astype(_BF16)

    ext = h_scr.shape[0]
    up_rows = ext // n_gran
    rows_per_gran = n_rows // n_gran

    gran_rows = rows_per_gran * gw

    def up_proj(c, buf, g):
        rows = pl.ds(pl.multiple_of(g * up_rows, BF16_ROWS), up_rows)
        u = jnp.dot(h_scr[rows, :], wup_ref[c], preferred_element_type=_F32)
        u_bufs[buf][rows, :] = u.astype(_BF16)

    edge = jnp.zeros((1, LANES), _BF16)

    def shift_tokens(v, down):
        if down:
            return jnp.concatenate([edge, v[:-1]], axis=0)
        return jnp.concatenate([v[1:], edge], axis=0)

    def conv_cols(ub, c, base, lo):
        def tap(k, off):
            x = ub[pl.ds(base + off, gw), lo:lo + LANES].reshape(gw // BF16_ROWS, BF16_ROWS, LANES)
            return x * wcv_ref[c, k, :, lo:lo + LANES][None]

        def col_taps(dc):
            k = dc + 1
            v = tap(3 + k, pad)
            if has_halo:
                v = v + tap(k, 0) + tap(6 + k, 2 * gw)
            return v.reshape(gw, LANES)
        return col_taps(0) + shift_tokens(col_taps(-1), True) + shift_tokens(col_taps(1), False)

    def conv_gate(c, buf, g):
        ub = u_bufs[buf]
        for r in range(rows_per_gran):
            base = pl.multiple_of(g * gran_rows + r * gw, gw)
            for j in range(FFN_CHUNK // LANES):
                a = conv_cols(ub, c, base, j * LANES)
                gt = conv_cols(ub, c, base, FFN_CHUNK + j * LANES)
                s_scr[pl.ds(base, gw), j * LANES:(j + 1) * LANES] = a * gt * _sigmoid(gt)

    def down_proj(c, g):
        rows = pl.ds(pl.multiple_of(g * gran_rows, gran_rows), gran_rows)
        acc_scr[rows, :] += jnp.dot(s_scr[rows, :], wdn_ref[c], preferred_element_type=_F32)

    def chunk_steps(c, buf, has_next):
        def step(g, carry):
            if has_next:
                up_proj(c + 1, 1 - buf, g)
            wrap = (g == 0).astype(jnp.int32)
            down_proj(jnp.maximum(c - wrap, 0), g - 1 + wrap * n_gran)
            conv_gate(c, buf, g)
            return carry
        lax.fori_loop(0, n_gran, step, 0)

    acc_scr[...] = jnp.zeros_like(acc_scr)
    s_scr[...] = jnp.zeros_like(s_scr)
    lax.fori_loop(0, n_gran, lambda g, carry: (up_proj(0, 0, g), carry)[1], 0)

    def body(k, carry):
        c = 2 * k
        chunk_steps(c, 0, True)
        chunk_steps(c + 1, 1, True)
        return carry

    n_pairs = (n_chunks - 1) // 2
    lax.fori_loop(0, n_pairs, body, 0)
    if n_chunks % 2 == 0:
        chunk_steps(n_chunks - 2, 0, True)
    chunk_steps(n_chunks - 1, (n_chunks - 1) % 2, False)
    down_proj(n_chunks - 1, n_gran - 1)

    y = alpha * xm_ref[0] + gate * acc_scr[...]
    o_ref[0] = _norm_rows(y) * lnw_ref[...] + lnb_ref[...]


def _ffn_call(x, mod, wup, wcv, wdn, lnw, lnb, *, tile, gw, alpha):
    b, l, d = x.shape
    n_tiles = l // tile
    has_halo = l > gw
    rows_per_tile = tile // gw
    n_grid_rows = l // gw
    ext = tile + (2 * gw if has_halo else 0)

    x_spec = pl.BlockSpec((1, tile, d), lambda bi, i: (bi, i, 0))
    in_specs = [x_spec]
    args = [x]
    if has_halo:
        in_specs.append(pl.BlockSpec(
            (1, gw, d), lambda bi, i: (bi, jnp.maximum(i * rows_per_tile - 1, 0), 0)))
        in_specs.append(pl.BlockSpec(
            (1, gw, d), lambda bi, i: (bi, jnp.minimum((i + 1) * rows_per_tile, n_grid_rows - 1), 0)))
        args += [x, x]
    in_specs += [
        pl.BlockSpec((1, 6, d), lambda bi, i: (bi, 0, 0)),
        _const_spec(wup.shape), _const_spec(wcv.shape), _const_spec(wdn.shape),
        _const_spec(lnw.shape), _const_spec(lnb.shape),
    ]
    args += [mod, wup, wcv, wdn, lnw, lnb]
    n_gran = max(1, tile // FFN_GRANULE)
    assert ext % (n_gran * BF16_ROWS) == 0 and rows_per_tile % n_gran == 0
    kern = functools.partial(_ffn_kernel, tile=tile, gw=gw, n_tiles=n_tiles, n_gran=n_gran,
                             has_halo=has_halo, alpha=alpha)
    return pl.pallas_call(
        kern,
        grid=(b, n_tiles),
        in_specs=in_specs,
        out_specs=x_spec,
        out_shape=jax.ShapeDtypeStruct(x.shape, _F32),
        scratch_shapes=[
            pltpu.VMEM((ext, d), _BF16),
            pltpu.VMEM((ext, 2 * FFN_CHUNK), _BF16),
            pltpu.VMEM((ext, 2 * FFN_CHUNK), _BF16),
            pltpu.VMEM((tile, FFN_CHUNK), _BF16),
            pltpu.VMEM((tile, d), _F32),
        ],
        compiler_params=pltpu.CompilerParams(
            dimension_semantics=("arbitrary", "arbitrary"),
            vmem_limit_bytes=VMEM_LIMIT),
        name="channel_mixer",
    )(*args)


def _prep_ffn_weights(w_up, conv_w, w_down):
    d, two_h = w_up.shape
    hid = two_h // 2
    n_chunks = hid // FFN_CHUNK
    wa = w_up[:, :hid].reshape(d, n_chunks, FFN_CHUNK)
    wg = w_up[:, hid:].reshape(d, n_chunks, FFN_CHUNK)
    wup = jnp.concatenate([wa, wg], axis=-1).transpose(1, 0, 2).astype(_BF16)
    cw = conv_w.reshape(9, two_h)
    ca = cw[:, :hid].reshape(9, n_chunks, FFN_CHUNK)
    cg = cw[:, hid:].reshape(9, n_chunks, FFN_CHUNK)
    wcv = jnp.concatenate([ca, cg], axis=-1).transpose(1, 0, 2).astype(_BF16)
    wcv = jnp.broadcast_to(wcv[:, :, None, :], (n_chunks, 9, BF16_ROWS, 2 * FFN_CHUNK))
    wdn = w_down.reshape(n_chunks, FFN_CHUNK, -1).astype(_BF16)
    return wup, wcv, wdn


GLA_HEADS = 4
GLA_DK = 32
GLA_DV = 64
GLA_QK = GLA_HEADS * GLA_DK
GLA_V = GLA_HEADS * GLA_DV
GLA_RANK = 16
GLA_CHUNK = 64
GLA_GATE_NORMALIZER = 16.0
GLA_LOG_DECAY_MIN = -1.0


def _log_decay(logit):
    log_sig = jnp.minimum(logit, 0.0) - jnp.log(1.0 + jnp.exp(-jnp.abs(logit)))
    return jnp.maximum(log_sig / GLA_GATE_NORMALIZER, GLA_LOG_DECAY_MIN)


def _chunk_cumsum(x, reverse):
    n = x.shape[0]
    pos = lax.broadcasted_iota(jnp.int32, x.shape, 0) % GLA_CHUNK
    s = 1
    while s < GLA_CHUNK:
        if reverse:
            x = x + jnp.where(pos < GLA_CHUNK - s, pltpu.roll(x, n - s, axis=0), 0.0)
        else:
            x = x + jnp.where(pos >= s, pltpu.roll(x, s, axis=0), 0.0)
        s *= 2
    return x


def _state_mask():
    r = lax.broadcasted_iota(jnp.int32, (GLA_V, GLA_QK), 0) // GLA_DV
    c = lax.broadcasted_iota(jnp.int32, (GLA_V, GLA_QK), 1) // GLA_DK
    return r == c


def _stack4(a):
    return jnp.concatenate([a, a, a, a], axis=0)


def _compact_state(s):
    return (s[0:GLA_DV] + s[GLA_DV:2 * GLA_DV]) + (s[2 * GLA_DV:3 * GLA_DV] + s[3 * GLA_DV:4 * GLA_DV])


def _state_update(state, v_bf, k_dec_bf, decay_row, mask):
    kv_t = lax.dot_general(v_bf, k_dec_bf, (((0,), (0,)), ((), ())), preferred_element_type=_F32)
    return state * decay_row + jnp.where(mask, kv_t, 0.0)


def _state_kernel(x_ref, mod_ref, w_ref, wg_ref, bg_ref, s0_ref, start_ref, fin_ref, st_scr,
                  *, tile, reverse):
    i = pl.program_id(1)
    n_tiles = pl.num_programs(1)
    chunks = tile // GLA_CHUNK

    @pl.when(i == 0)
    def _():
        st_scr[...] = s0_ref[0]

    shift = mod_ref[0, 0:1, :]
    scale1 = 1.0 + mod_ref[0, 1:2, :]
    h = (_norm_rows(x_ref[0]) * scale1 + shift).astype(_BF16)
    z = jnp.dot(h, w_ref[...], preferred_element_type=_F32)
    k = z[:, 0:GLA_QK]
    v_bf = z[:, GLA_QK:GLA_QK + GLA_V].astype(_BF16)
    low = z[:, GLA_QK + GLA_V:].astype(_BF16)
    g = _log_decay(jnp.dot(low, wg_ref[...], preferred_element_type=_F32) + bg_ref[...])
    run = _chunk_cumsum(g, reverse)
    mask = _state_mask()
    state = st_scr[...]
    order = range(chunks - 1, -1, -1) if reverse else range(chunks)
    for ci in order:
        rows = slice(ci * GLA_CHUNK, (ci + 1) * GLA_CHUNK)
        edge = ci * GLA_CHUNK if reverse else (ci + 1) * GLA_CHUNK - 1
        total = run[edge:edge + 1, :]
        k_dec = (k[rows] * jnp.exp(total - run[rows])).astype(_BF16)
        start_ref[0, ci] = _compact_state(state)
        state = _state_update(state, v_bf[rows], k_dec, jnp.exp(total), mask)
    st_scr[...] = state

    @pl.when(i == n_tiles - 1)
    def _():
        fin_ref[0] = state


def _state_call(x, mod, w, wg, bg, s0, *, tile, reverse):
    b, l, d = x.shape
    n_tiles = l // tile
    chunks = tile // GLA_CHUNK
    tmap = (lambda bi, i: (bi, n_tiles - 1 - i, 0)) if reverse else (lambda bi, i: (bi, i, 0))
    smap = (lambda bi, i: (bi, n_tiles - 1 - i, 0, 0)) if reverse else (lambda bi, i: (bi, i, 0, 0))
    st_spec = pl.BlockSpec((1, GLA_V, GLA_QK), lambda bi, i: (bi, 0, 0))
    kern = functools.partial(_state_kernel, tile=tile, reverse=reverse)
    return pl.pallas_call(
        kern,
        grid=(b, n_tiles),
        in_specs=[
            pl.BlockSpec((1, tile, d), tmap),
            pl.BlockSpec((1, 6, d), lambda bi, i: (bi, 0, 0)),
            _const_spec(w.shape), _const_spec(wg.shape), _const_spec(bg.shape),
            st_spec,
        ],
        out_specs=[
            pl.BlockSpec((1, chunks, GLA_DV, GLA_QK), smap),
            st_spec,
        ],
        out_shape=[
            jax.ShapeDtypeStruct((b, l // GLA_CHUNK, GLA_DV, GLA_QK), _F32),
            jax.ShapeDtypeStruct((b, GLA_V, GLA_QK), _F32),
        ],
        scratch_shapes=[pltpu.VMEM((GLA_V, GLA_QK), _F32)],
        compiler_params=pltpu.CompilerParams(
            dimension_semantics=("arbitrary", "arbitrary"),
            vmem_limit_bytes=VMEM_LIMIT),
        name="gla_state_rev" if reverse else "gla_state_fwd",
    )(x, mod, w, wg, bg, s0)


GROUP_W = 256
COL_Q, COL_K, COL_V, COL_R, COL_LOW, COL_C, COL_B, COL_D = 0, 128, 256, 512, 768, 896, 1408, 1664
IN_COLS = 2176
HALO = 16
POOL_REACH = 8
MIX_BLOCK = 128
CONV_WIDTH = 31
SGU_HEADS = 4
SGU_HD = 64


def _gelu_tanh(x):
    return 0.5 * x * (1.0 + jnp.tanh(0.7978845608028654 * (x + 0.044715 * (x * x * x))))


def _mixer_kernel(xm_ref, xp_ref, xn_ref, mod_ref, win_ref, wg_ref, bg_ref, gnw_ref,
                  pblk_ref, pscale_ref, sguw_ref, sgub_ref, slnw_ref, slnb_ref,
                  cw_ref, cb_ref, clnw_ref, clnb_ref, wout_ref, plw_ref, plb_ref,
                  sb_ref, sf0_ref,
                  o_ref, sfin_ref,
                  z_scr, zb_scr, y_scr, run_scr, o_scr, cat_scr, st_scr,
                  *, tile, seq_len, alpha):
    i = pl.program_id(1)
    n_tiles = pl.num_programs(1)
    n_blocks = tile // MIX_BLOCK

    @pl.when(i == 0)
    def _():
        st_scr[...] = sf0_ref[0]

    shift = mod_ref[0, 0:1, :]
    scale1 = 1.0 + mod_ref[0, 1:2, :]
    gate = mod_ref[0, 2:3, :]

    def modulated(xv):
        return _norm_rows(xv) * scale1 + shift

    hp = jnp.where(i > 0, modulated(xp_ref[0]), 0.0).astype(_BF16)
    hn = jnp.where(i < n_tiles - 1, modulated(xn_ref[0]), 0.0).astype(_BF16)
    zh = jnp.dot(jnp.concatenate([hp, hn], axis=0), win_ref[:, COL_B:],
                 preferred_element_type=_F32)

    def glu(zd):
        return zd[:, :GROUP_W] * _sigmoid(zd[:, GROUP_W:])

    zb_scr[0:HALO, :] = zh[0:HALO, 0:GROUP_W]
    zb_scr[HALO + tile:, :] = zh[HALO:, 0:GROUP_W]
    y_scr[0:HALO, :] = glu(zh[0:HALO, GROUP_W:])
    y_scr[HALO + tile:, :] = glu(zh[HALO:, GROUP_W:])

    def block_rows(blk):
        return slice(blk * MIX_BLOCK, (blk + 1) * MIX_BLOCK)

    def in_proj(blk):
        rows = block_rows(blk)
        ext_rows = slice(HALO + blk * MIX_BLOCK, HALO + (blk + 1) * MIX_BLOCK)
        h = modulated(xm_ref[0, rows, :]).astype(_BF16)
        z = jnp.dot(h, win_ref[...], preferred_element_type=_F32)
        z_scr[rows, :] = z
        zb_scr[ext_rows, :] = z[:, COL_B:COL_B + GROUP_W]
        y_scr[ext_rows, :] = glu(z[:, COL_D:COL_D + 2 * GROUP_W])
        low = z[:, COL_LOW:COL_LOW + LANES].astype(_BF16)
        g = _log_decay(jnp.dot(low, wg_ref[...], preferred_element_type=_F32) + bg_ref[...])
        run_scr[rows, 0:GLA_QK] = _chunk_cumsum(g[:, 0:GLA_QK], False)
        run_scr[rows, GLA_QK:] = _chunk_cumsum(g[:, GLA_QK:], True)

    lane = lax.broadcasted_iota(jnp.int32, (MIX_BLOCK, GROUP_W), 1)
    row = lax.broadcasted_iota(jnp.int32, (MIX_BLOCK, GROUP_W), 0)
    lane_group = lane // SGU_HD
    pool_half = jnp.left_shift(1, lane_group)

    def local_mixers(blk):
        r0 = blk * MIX_BLOCK
        rows = block_rows(blk)

        n = MIX_BLOCK + 2 * POOL_REACH
        xb = zb_scr[r0 + HALO - POOL_REACH:r0 + HALO - POOL_REACH + n, :]
        p2 = xb + pltpu.roll(xb, 1, axis=0)
        p4 = pltpu.roll(p2, 1, axis=0) + pltpu.roll(p2, n - 1, axis=0)
        p8 = pltpu.roll(p4, 2, axis=0) + pltpu.roll(p4, n - 2, axis=0)
        p16 = pltpu.roll(p8, 4, axis=0) + pltpu.roll(p8, n - 4, axis=0)
        inner = slice(POOL_REACH, POOL_REACH + MIX_BLOCK)
        win = jnp.where(lane_group == 0, p2[inner],
                        jnp.where(lane_group == 1, p4[inner],
                                  jnp.where(lane_group == 2, p8[inner], p16[inner])))
        tok = i * tile + r0 + row
        count = jnp.minimum(tok + pool_half, seq_len) - jnp.maximum(tok - pool_half, 0)
        pooled = win / count.astype(_F32) - xb[inner]
        yb = jnp.dot(pooled.astype(_BF16), pblk_ref[...], preferred_element_type=_F32) * pscale_ref[...]
        cat_scr[rows, GROUP_W:2 * GROUP_W] = yb.astype(_BF16)

        gz = _gelu_tanh(z_scr[rows, COL_C:COL_C + 2 * GROUP_W])
        vn = _norm_rows(gz[:, GROUP_W:]) * slnw_ref[...] + slnb_ref[...]
        vstack = jnp.concatenate(
            [jnp.where(lane_group == hd, vn, 0.0) for hd in range(SGU_HEADS)], axis=0).astype(_BF16)
        sg = jnp.dot(sguw_ref[...], vstack, preferred_element_type=_F32) + sgub_ref[...]
        cat_scr[rows, 2 * GROUP_W:3 * GROUP_W] = (gz[:, :GROUP_W] * sg).astype(_BF16)

        span = MIX_BLOCK + SUBLANES
        conv = None
        for b in range(SUBLANES):
            part = None
            for a in range(4):
                o = SUBLANES * a + b
                if 1 <= o <= CONV_WIDTH:
                    term = cw_ref[o - 1:o, :] * y_scr[r0 + SUBLANES * a:r0 + SUBLANES * a + span, :]
                    part = term if part is None else part + term
            part = part[b:b + MIX_BLOCK]
            conv = part if conv is None else conv + part
        yd = _norm_rows(conv + cb_ref[...]) * clnw_ref[...] + clnb_ref[...]
        cat_scr[rows, 3 * GROUP_W:4 * GROUP_W] = (yd * _sigmoid(yd)).astype(_BF16)

    mask_s = _state_mask()
    vr = lax.broadcasted_iota(jnp.int32, (GLA_V, GLA_V), 0) // GLA_DV
    vc = lax.broadcasted_iota(jnp.int32, (GLA_V, GLA_V), 1) // GLA_DV
    mask_v = vr == vc
    t_id = lax.broadcasted_iota(jnp.int32, (GLA_CHUNK, GLA_V), 0)
    s_id = lax.broadcasted_iota(jnp.int32, (GLA_CHUNK, GLA_V), 1) % GLA_CHUNK
    nt_dims = (((1,), (1,)), ((), ()))

    zero = jnp.zeros((), _BF16)

    def chunk_rows(ci):
        return slice(ci * GLA_CHUNK, (ci + 1) * GLA_CHUNK)

    def scores(ci):
        r = chunk_rows(ci)
        q = z_scr[r, COL_Q:COL_Q + GLA_QK] * (GLA_DK ** -0.5)
        k = z_scr[r, COL_K:COL_K + GLA_QK]
        cf = run_scr[r, 0:GLA_QK]
        rb = run_scr[r, GLA_QK:]
        qf = (q * jnp.exp(cf)).astype(_BF16)
        kf = (k * jnp.exp(-cf)).astype(_BF16)
        qb = (q * jnp.exp(rb)).astype(_BF16)
        kb = (k * jnp.exp(-rb)).astype(_BF16)
        a_f = lax.dot_general(qf, jnp.where(mask_s, _stack4(kf), zero), nt_dims,
                              preferred_element_type=_F32)
        a_b = lax.dot_general(qb, jnp.where(mask_s, _stack4(kb), zero), nt_dims,
                              preferred_element_type=_F32)
        a = (jnp.where(s_id <= t_id, a_f, 0.0) + jnp.where(s_id >= t_id, a_b, 0.0)).astype(_BF16)
        return a, qf, qb

    def outputs(ci, a, qf, qb):
        r = chunk_rows(ci)
        k = z_scr[r, COL_K:COL_K + GLA_QK]
        v = z_scr[r, COL_V:COL_V + GLA_V].astype(_BF16)
        cf = run_scr[r, 0:GLA_QK]
        total = cf[GLA_CHUNK - 1:GLA_CHUNK, :]
        k_dec = (k * jnp.exp(total - cf)).astype(_BF16)
        o = jnp.dot(a, jnp.where(mask_v, _stack4(v), zero), preferred_element_type=_F32)
        sf = st_scr[...]
        sb = jnp.where(mask_s, _stack4(sb_ref[0, ci]), 0.0)
        o = o + lax.dot_general(qf, sf.astype(_BF16), nt_dims, preferred_element_type=_F32)
        o = o + lax.dot_general(qb, sb.astype(_BF16), nt_dims, preferred_element_type=_F32)
        o_scr[r, :] = o
        st_scr[...] = _state_update(sf, v, k_dec, jnp.exp(total), mask_s)

    head_mean = jnp.where(mask_v, 1.0 / GLA_DV, 0.0).astype(_BF16)

    def readout(blk):
        rows = block_rows(blk)
        o = o_scr[rows, :]
        sq = o * o
        sq_hi = sq.astype(_BF16)
        sq_lo = (sq - sq_hi.astype(_F32)).astype(_BF16)
        ms = (jnp.dot(sq_hi, head_mean, preferred_element_type=_F32)
              + jnp.dot(sq_lo, head_mean, preferred_element_type=_F32))
        rg = z_scr[rows, COL_R:COL_R + GROUP_W]
        ya = o * lax.rsqrt(ms + EPS) * gnw_ref[...] * (rg * _sigmoid(rg))
        cat_scr[rows, 0:GROUP_W] = ya.astype(_BF16)

    def out_proj(blk):
        rows = block_rows(blk)
        y = jnp.dot(cat_scr[rows, :], wout_ref[...], preferred_element_type=_F32)
        o_ref[0, rows, :] = (_norm_rows(alpha * xm_ref[0, rows, :] + gate * y) * plw_ref[...]
                             + plb_ref[...])

    chunks_per_block = MIX_BLOCK // GLA_CHUNK
    n_gla_chunks = tile // GLA_CHUNK
    in_proj(0)
    if n_blocks > 1:
        in_proj(1)
    cur = scores(0)
    for blk in range(n_blocks):
        if blk + 2 < n_blocks:
            in_proj(blk + 2)
        local_mixers(blk)
        for ci in range(blk * chunks_per_block, (blk + 1) * chunks_per_block):
            nxt = scores(ci + 1) if ci + 1 < n_gla_chunks else None
            outputs(ci, *cur)
            cur = nxt
        readout(blk)
        out_proj(blk)

    @pl.when(i == n_tiles - 1)
    def _():
        sfin_ref[0] = st_scr[...]


def _mixer_call(x, mod, mw, sb_start, sf0, *, tile, alpha):
    b, l, d = x.shape
    n_tiles = l // tile
    halo_per_tile = tile // HALO
    n_halo_blocks = l // HALO
    consts = [mw[k] for k in ("w_in", "wg2", "bg2", "gnw", "pool_blk", "pool_scale", "sgu_w", "sgu_b",
                              "sgu_ln_w", "sgu_ln_b", "cm_w", "cm_b", "cm_ln_w", "cm_ln_b", "w_out",
                              "post_w", "post_b")]
    x_spec = pl.BlockSpec((1, tile, d), lambda bi, i: (bi, i, 0))
    st_spec = pl.BlockSpec((1, GLA_V, GLA_QK), lambda bi, i: (bi, 0, 0))
    in_specs = [
        x_spec,
        pl.BlockSpec((1, HALO, d), lambda bi, i: (bi, jnp.maximum(i * halo_per_tile - 1, 0), 0)),
        pl.BlockSpec((1, HALO, d), lambda bi, i: (bi, jnp.minimum((i + 1) * halo_per_tile, n_halo_blocks - 1), 0)),
        pl.BlockSpec((1, 6, d), lambda bi, i: (bi, 0, 0)),
    ] + [_const_spec(c.shape) for c in consts] + [
        pl.BlockSpec((1, tile // GLA_CHUNK, GLA_DV, GLA_QK), lambda bi, i: (bi, i, 0, 0)),
        st_spec,
    ]
    kern = functools.partial(_mixer_kernel, tile=tile, seq_len=l, alpha=alpha)
    return pl.pallas_call(
        kern,
        grid=(b, n_tiles),
        in_specs=in_specs,
        out_specs=[x_spec, st_spec],
        out_shape=[jax.ShapeDtypeStruct(x.shape, _F32),
                   jax.ShapeDtypeStruct((b, GLA_V, GLA_QK), _F32)],
        scratch_shapes=[
            pltpu.VMEM((tile, IN_COLS), _F32),
            pltpu.VMEM((tile + 2 * HALO, GROUP_W), _F32),
            pltpu.VMEM((tile + 2 * HALO, GROUP_W), _F32),
            pltpu.VMEM((tile, 2 * GLA_QK), _F32),
            pltpu.VMEM((tile, GLA_V), _F32),
            pltpu.VMEM((tile, 4 * GROUP_W), _BF16),
            pltpu.VMEM((GLA_V, GLA_QK), _F32),
        ],
        compiler_params=pltpu.CompilerParams(
            dimension_semantics=("arbitrary", "arbitrary"),
            vmem_limit_bytes=VMEM_LIMIT),
        name="token_mixer",
    )(x, x, x, mod, *consts, sb_start, sf0)


def _prep_mixer_weights(w_in, gla_w_gate, gla_b_gate, gla_norm_w, pool_w, pool_scale, sgu_w, sgu_b,
                        sgu_ln_w, sgu_ln_b, cm_conv_w, cm_conv_b, cm_ln_w, cm_ln_b, w_out,
                        post_w, post_b):
    d = w_in.shape[0]
    qkvr = w_in[:, 0:768]
    low = w_in[:, 768:800]
    zb = w_in[:, 800:1056]
    zc = w_in[:, 1056:1568]
    zd = w_in[:, 1568:2080]
    pad = jnp.zeros((d, LANES - 2 * GLA_RANK), _F32)
    w_main = jnp.concatenate([qkvr, low, pad, zc, zb, zd], axis=1).astype(_BF16)
    kv = w_in[:, 128:512]
    pad1 = jnp.zeros((d, LANES - GLA_RANK), _F32)
    w_state = [jnp.concatenate([kv, low[:, di * GLA_RANK:(di + 1) * GLA_RANK], pad1], axis=1).astype(_BF16)
               for di in range(2)]
    wg1 = [jnp.zeros((LANES, GLA_QK), _F32).at[0:GLA_RANK].set(gla_w_gate[di]).astype(_BF16)
           for di in range(2)]
    bg1 = [gla_b_gate[di][None, :] for di in range(2)]
    wg2 = jnp.zeros((LANES, 2 * GLA_QK), _F32)
    wg2 = wg2.at[0:GLA_RANK, 0:GLA_QK].set(gla_w_gate[0])
    wg2 = wg2.at[GLA_RANK:2 * GLA_RANK, GLA_QK:].set(gla_w_gate[1]).astype(_BF16)
    bg2 = jnp.concatenate([gla_b_gate[0], gla_b_gate[1]])[None, :]
    pool_blk = jnp.zeros((GROUP_W, GROUP_W), _F32)
    for gi in range(pool_w.shape[0]):
        sl = slice(gi * SGU_HD, (gi + 1) * SGU_HD)
        pool_blk = pool_blk.at[sl, sl].set(pool_w[gi])
    row = lambda a: a[None, :]
    return {
        "w_in": w_main, "w_state": w_state, "wg1": wg1, "bg1": bg1, "wg2": wg2, "bg2": bg2,
        "gnw": row(gla_norm_w), "pool_blk": pool_blk.astype(_BF16), "pool_scale": row(pool_scale),
        "sgu_w": jnp.concatenate([sgu_w[hd] for hd in range(SGU_HEADS)], axis=1).astype(_BF16),
        "sgu_b": jnp.repeat(sgu_b.T, SGU_HD, axis=1),
        "sgu_ln_w": row(sgu_ln_w), "sgu_ln_b": row(sgu_ln_b),
        "cm_w": jnp.concatenate([cm_conv_w, jnp.zeros((1, GROUP_W), _F32)], axis=0),
        "cm_b": row(cm_conv_b), "cm_ln_w": row(cm_ln_w), "cm_ln_b": row(cm_ln_b),
        "w_out": w_out.astype(_BF16), "post_w": row(post_w), "post_b": row(post_b),
    }


def _mod_kernel(c_ref, w_ref, b_ref, o_ref):
    cv = c_ref[...]
    o_ref[0] = jnp.dot(cv * _sigmoid(cv), w_ref[0], preferred_element_type=_F32,
                       precision=lax.Precision.HIGHEST) + b_ref[0]


def _mod_call(cond, w_mod, b_mod):
    depth, d, six_d = w_mod.shape
    rows = cond.shape[0]
    return pl.pallas_call(
        _mod_kernel,
        grid=(depth, six_d // d),
        in_specs=[
            pl.BlockSpec((rows, d), lambda li, j: (0, 0)),
            pl.BlockSpec((1, d, d), lambda li, j: (li, 0, j)),
            pl.BlockSpec((1, 1, d), lambda li, j: (li, 0, j)),
        ],
        out_specs=pl.BlockSpec((1, rows, d), lambda li, j: (li, 0, j)),
        out_shape=jax.ShapeDtypeStruct((depth, rows, six_d), _F32),
        compiler_params=pltpu.CompilerParams(
            dimension_semantics=("arbitrary", "arbitrary"),
            vmem_limit_bytes=VMEM_LIMIT),
        name="adaln_modulation",
    )(cond, w_mod, b_mod.reshape(depth, 1, six_d))


def _pick_tile(length, target):
    return min(length, target)


def kernel(x, c, ctx, c_ctx, w_mod, b_mod, w_in, gla_w_gate, gla_b_gate, gla_norm_w, pool_w, pool_scale, sgu_w, sgu_b, sgu_ln_w, sgu_ln_b, cm_conv_w, cm_conv_b, cm_ln_w, cm_ln_b, w_out, ffn_w_up, ffn_conv_w, ffn_w_down, post_ln_w, post_ln_b):
    batch, seq, d = x.shape
    ctx_len = ctx.shape[1]
    depth = w_mod.shape[0]
    alpha = (2 * depth) ** 0.25

    cond_rows = -(-(batch + 1) // SUBLANES) * SUBLANES
    cond = jnp.concatenate([c, c_ctx[None, :], jnp.zeros((cond_rows - batch - 1, d), _F32)], axis=0)
    mod = _mod_call(cond, w_mod, b_mod)

    lat_mix_tile = _pick_tile(seq, MIX_TILE)
    ctx_mix_tile = _pick_tile(ctx_len, MIX_TILE)
    lat_ffn_tile = _pick_tile(seq, FFN_TILE)
    zero_state = jnp.zeros((batch, GLA_V, GLA_QK), _F32)

    for li in range(depth):
        with_ctx_out = li < depth - 1
        mod_lat = mod[li, :batch].reshape(batch, 6, d)
        mod_ctx = jnp.broadcast_to(mod[li, batch].reshape(1, 6, d), (batch, 6, d))
        mw = _prep_mixer_weights(w_in[li], gla_w_gate[li], gla_b_gate[li], gla_norm_w[li], pool_w[li],
                                 pool_scale[li], sgu_w[li], sgu_b[li], sgu_ln_w[li], sgu_ln_b[li],
                                 cm_conv_w[li], cm_conv_b[li], cm_ln_w[li], cm_ln_b[li], w_out[li],
                                 post_ln_w[li, 0], post_ln_b[li, 0])
        ffn_w = _prep_ffn_weights(ffn_w_up[li], ffn_conv_w[li], ffn_w_down[li])
        ffn_ln = (post_ln_w[li, 1][None, :], post_ln_b[li, 1][None, :])

        ctx_sb, ctx_sb_fin = _state_call(ctx, mod_ctx, mw["w_state"][1], mw["wg1"][1], mw["bg1"][1],
                                         zero_state, tile=ctx_mix_tile, reverse=True)
        if with_ctx_out:
            ctx_mixed, ctx_sf_fin = _mixer_call(ctx, mod_ctx, mw, ctx_sb, zero_state,
                                                tile=ctx_mix_tile, alpha=alpha)
        else:
            _, ctx_sf_fin = _state_call(ctx, mod_ctx, mw["w_state"][0], mw["wg1"][0], mw["bg1"][0],
                                        zero_state, tile=ctx_mix_tile, reverse=False)

        lat_sb, _ = _state_call(x, mod_lat, mw["w_state"][1], mw["wg1"][1], mw["bg1"][1],
                                ctx_sb_fin, tile=lat_mix_tile, reverse=True)
        x, _ = _mixer_call(x, mod_lat, mw, lat_sb, ctx_sf_fin, tile=lat_mix_tile, alpha=alpha)
        x = _ffn_call(x, mod_lat, *ffn_w, *ffn_ln, tile=lat_ffn_tile, gw=GRID_W, alpha=alpha)
        if with_ctx_out:
            ctx = _ffn_call(ctx_mixed, mod_ctx, *ffn_w, *ffn_ln, tile=ctx_len, gw=ctx_len, alpha=alpha)
    return x
```

```python
import functools

import jax
import jax.numpy as jnp
from jax import lax
from jax.experimental import pallas as pl
from jax.experimental.pallas import tpu as pltpu

EPS = 1e-6
LANES = 128
SUBLANES = 8
GRID_W = 64
FFN_CHUNK = 256
FFN_GRANULE = 512
BF16_ROWS = 16
MIX_TILE = 512
FFN_TILE = 1024
VMEM_LIMIT = 60 * 1024 * 1024

_F32 = jnp.float32
_BF16 = jnp.bfloat16


def _norm_rows(x):
    mu = jnp.mean(x, axis=-1, keepdims=True)
    xc = x - mu
    var = jnp.mean(xc * xc, axis=-1, keepdims=True)
    return xc * lax.rsqrt(var + EPS)


def _sigmoid(x):
    return 1.0 / (1.0 + jnp.exp(-x))


def _const_spec(shape):
    nd = len(shape)
    return pl.BlockSpec(shape, lambda *_: (0,) * nd, pipeline_mode=pl.Buffered(1))


def _ffn_kernel(*refs, tile, gw, n_tiles, n_gran, has_halo, alpha):
    if has_halo:
        (xm_ref, xp_ref, xn_ref, mod_ref, wup_ref, wcv_ref, wdn_ref, lnw_ref, lnb_ref,
         o_ref, h_scr, u0_scr, u1_scr, s_scr, acc_scr) = refs
    else:
        (xm_ref, mod_ref, wup_ref, wcv_ref, wdn_ref, lnw_ref, lnb_ref,
         o_ref, h_scr, u0_scr, u1_scr, s_scr, acc_scr) = refs
    u_bufs = (u0_scr, u1_scr)
    i = pl.program_id(1)
    n_chunks = wup_ref.shape[0]
    pad = gw if has_halo else 0
    n_rows = tile // gw

    shift = mod_ref[0, 3:4, :]
    scale1 = 1.0 + mod_ref[0, 4:5, :]
    gate = mod_ref[0, 5:6, :]

    def modulated(xv):
        return _norm_rows(xv) * scale1 + shift

    ext = h_scr.shape[0]
    up_rows = ext // n_gran
    rows_per_gran = n_rows // n_gran
    gran_rows = rows_per_gran * gw

    def modulate_granule(g):
        rows = slice(g * gran_rows, (g + 1) * gran_rows)
        h_scr[pad + g * gran_rows:pad + (g + 1) * gran_rows, :] = modulated(xm_ref[0, rows, :]).astype(_BF16)
        if has_halo and g == 0:
            hp = jnp.where(i > 0, modulated(xp_ref[0]), 0.0)
            h_scr[0:gw, :] = hp.astype(_BF16)
        if has_halo and g == n_gran - 1:
            hn = jnp.where(i < n_tiles - 1, modulated(xn_ref[0]), 0.0)
            h_scr[pad + tile:pad + tile + gw, :] = hn.astype(_BF16)

    def up_proj(c, buf, g):
        rows = pl.ds(pl.multiple_of(g * up_rows, BF16_ROWS), up_rows)
        u_bufs[buf][rows, :] = jnp.dot(h_scr[rows, :], wup_ref[c], preferred_element_type=_F32)

    edge = jnp.zeros((1, LANES), _F32)

    def shift_tokens(v, down):
        if down:
            return jnp.concatenate([edge, v[:-1]], axis=0)
        return jnp.concatenate([v[1:], edge], axis=0)

    def conv_cols(ub, wc, base, lo):
        def tap(k, off):
            return wc[k:k + 1, lo:lo + LANES] * ub[pl.ds(base + off, gw), lo:lo + LANES]

        def col_taps(dc):
            k = dc + 1
            v = tap(3 + k, pad)
            if has_halo:
                v = v + tap(k, 0) + tap(6 + k, 2 * gw)
            return v
        return col_taps(0) + shift_tokens(col_taps(-1), True) + shift_tokens(col_taps(1), False)

    def conv_gate(c, buf, g):
        ub = u_bufs[buf]
        wc = wcv_ref[c]
        for r in range(rows_per_gran):
            base = pl.multiple_of(g * gran_rows + r * gw, gw)
            for j in range(FFN_CHUNK // LANES):
                a = conv_cols(ub, wc, base, j * LANES)
                gt = conv_cols(ub, wc, base, FFN_CHUNK + j * LANES)
                s_scr[pl.ds(base, gw), j * LANES:(j + 1) * LANES] = (a * gt * _sigmoid(gt)).astype(_BF16)

    def down_proj(c, g):
        rows = pl.ds(pl.multiple_of(g * gran_rows, gran_rows), gran_rows)
        acc_scr[rows, :] += jnp.dot(s_scr[rows, :], wdn_ref[c], preferred_element_type=_F32)

    def chunk_steps(c, buf, has_next):
        def step(g, carry):
            if has_next:
                up_proj(c + 1, 1 - buf, g)
            wrap = (g == 0).astype(jnp.int32)
            down_proj(jnp.maximum(c - wrap, 0), g - 1 + wrap * n_gran)
            conv_gate(c, buf, g)
            return carry
        lax.fori_loop(0, n_gran, step, 0)

    acc_scr[...] = jnp.zeros_like(acc_scr)
    s_scr[...] = jnp.zeros_like(s_scr)
    done = 0
    for g in range(n_gran):
        need = min(n_gran, -(-((g + 1) * up_rows - pad) // gran_rows))
        for gm in range(done, need):
            modulate_granule(gm)
        done = max(done, need)
        up_proj(0, 0, g)

    def body(k, carry):
        c = 2 * k
        chunk_steps(c, 0, True)
        chunk_steps(c + 1, 1, True)
        return carry

    n_pairs = (n_chunks - 1) // 2
    lax.fori_loop(0, n_pairs, body, 0)
    if n_chunks % 2 == 0:
        chunk_steps(n_chunks - 2, 0, True)
    chunk_steps(n_chunks - 1, (n_chunks - 1) % 2, False)
    down_proj(n_chunks - 1, n_gran - 1)

    for g in range(n_gran):
        rows = slice(g * gran_rows, (g + 1) * gran_rows)
        y = alpha * xm_ref[0, rows, :] + gate * acc_scr[rows, :]
        o_ref[0, rows, :] = _norm_rows(y) * lnw_ref[...] + lnb_ref[...]


def _ffn_call(x, mod, wup, wcv, wdn, lnw, lnb, *, tile, gw, alpha):
    b, l, d = x.shape
    n_tiles = l // tile
    has_halo = l > gw
    rows_per_tile = tile // gw
    n_grid_rows = l // gw
    ext = tile + (2 * gw if has_halo else 0)

    x_spec = pl.BlockSpec((1, tile, d), lambda bi, i: (bi, i, 0))
    in_specs = [x_spec]
    args = [x]
    if has_halo:
        in_specs.append(pl.BlockSpec(
            (1, gw, d), lambda bi, i: (bi, jnp.maximum(i * rows_per_tile - 1, 0), 0)))
        in_specs.append(pl.BlockSpec(
            (1, gw, d), lambda bi, i: (bi, jnp.minimum((i + 1) * rows_per_tile, n_grid_rows - 1), 0)))
        args += [x, x]
    in_specs += [
        pl.BlockSpec((1, 6, d), lambda bi, i: (bi, 0, 0)),
        _const_spec(wup.shape), _const_spec(wcv.shape), _const_spec(wdn.shape),
        _const_spec(lnw.shape), _const_spec(lnb.shape),
    ]
    args += [mod, wup, wcv, wdn, lnw, lnb]
    n_gran = max(1, tile // FFN_GRANULE)
    assert ext % (n_gran * BF16_ROWS) == 0 and rows_per_tile % n_gran == 0
    kern = functools.partial(_ffn_kernel, tile=tile, gw=gw, n_tiles=n_tiles, n_gran=n_gran,
                             has_halo=has_halo, alpha=alpha)
    return pl.pallas_call(
        kern,
        grid=(b, n_tiles),
        in_specs=in_specs,
        out_specs=x_spec,
        out_shape=jax.ShapeDtypeStruct(x.shape, _F32),
        scratch_shapes=[
            pltpu.VMEM((ext, d), _BF16),
            pltpu.VMEM((ext, 2 * FFN_CHUNK), _F32),
            pltpu.VMEM((ext, 2 * FFN_CHUNK), _F32),
            pltpu.VMEM((tile, FFN_CHUNK), _BF16),
            pltpu.VMEM((tile, d), _F32),
        ],
        compiler_params=pltpu.CompilerParams(
            dimension_semantics=("arbitrary", "arbitrary"),
            vmem_limit_bytes=VMEM_LIMIT),
        name="channel_mixer",
    )(*args)


def _prep_ffn_weights(w_up, conv_w, w_down):
    d, two_h = w_up.shape
    hid = two_h // 2
    n_chunks = hid // FFN_CHUNK
    wa = w_up[:, :hid].reshape(d, n_chunks, FFN_CHUNK)
    wg = w_up[:, hid:].reshape(d, n_chunks, FFN_CHUNK)
    wup = jnp.concatenate([wa, wg], axis=-1).transpose(1, 0, 2).astype(_BF16)
    cw = conv_w.reshape(9, two_h)
    ca = cw[:, :hid].reshape(9, n_chunks, FFN_CHUNK)
    cg = cw[:, hid:].reshape(9, n_chunks, FFN_CHUNK)
    wcv = jnp.concatenate([ca, cg], axis=-1).transpose(1, 0, 2)
    wdn = w_down.reshape(n_chunks, FFN_CHUNK, -1).astype(_BF16)
    return wup, wcv, wdn


GLA_HEADS = 4
GLA_DK = 32
GLA_DV = 64
GLA_QK = GLA_HEADS * GLA_DK
GLA_V = GLA_HEADS * GLA_DV
GLA_RANK = 16
GLA_CHUNK = 64
GLA_GATE_NORMALIZER = 16.0
GLA_LOG_DECAY_MIN = -1.0


def _log_decay(logit):
    log_sig = jnp.minimum(logit, 0.0) - jnp.log(1.0 + jnp.exp(-jnp.abs(logit)))
    return jnp.maximum(log_sig / GLA_GATE_NORMALIZER, GLA_LOG_DECAY_MIN)


def _chunk_cumsum(x, reverse):
    n = x.shape[0]
    pos = lax.broadcasted_iota(jnp.int32, x.shape, 0) % GLA_CHUNK
    s = 1
    while s < GLA_CHUNK:
        if reverse:
            x = x + jnp.where(pos < GLA_CHUNK - s, pltpu.roll(x, n - s, axis=0), 0.0)
        else:
            x = x + jnp.where(pos >= s, pltpu.roll(x, s, axis=0), 0.0)
        s *= 2
    return x


def _state_mask():
    r = lax.broadcasted_iota(jnp.int32, (GLA_V, GLA_QK), 0) // GLA_DV
    c = lax.broadcasted_iota(jnp.int32, (GLA_V, GLA_QK), 1) // GLA_DK
    return r == c


def _stack4(a):
    return jnp.concatenate([a, a, a, a], axis=0)


def _compact_state(s):
    return (s[0:GLA_DV] + s[GLA_DV:2 * GLA_DV]) + (s[2 * GLA_DV:3 * GLA_DV] + s[3 * GLA_DV:4 * GLA_DV])


def _state_update(state, v_bf, k_dec_bf, decay_row, mask):
    kv_t = lax.dot_general(v_bf, k_dec_bf, (((0,), (0,)), ((), ())), preferred_element_type=_F32)
    return state * decay_row + jnp.where(mask, kv_t, 0.0)


def _state_kernel(x_ref, mod_ref, w_ref, wg_ref, bg_ref, s0_ref, start_ref, fin_ref, st_scr,
                  *, tile, reverse):
    i = pl.program_id(1)
    n_tiles = pl.num_programs(1)
    chunks = tile // GLA_CHUNK

    @pl.when(i == 0)
    def _():
        st_scr[...] = s0_ref[0]

    shift = mod_ref[0, 0:1, :]
    scale1 = 1.0 + mod_ref[0, 1:2, :]
    h = (_norm_rows(x_ref[0]) * scale1 + shift).astype(_BF16)
    z = jnp.dot(h, w_ref[...], preferred_element_type=_F32)
    k = z[:, 0:GLA_QK]
    v_bf = z[:, GLA_QK:GLA_QK + GLA_V].astype(_BF16)
    low = z[:, GLA_QK + GLA_V:].astype(_BF16)
    g = _log_decay(jnp.dot(low, wg_ref[...], preferred_element_type=_F32) + bg_ref[...])
    run = _chunk_cumsum(g, reverse)
    mask = _state_mask()
    state = st_scr[...]
    order = range(chunks - 1, -1, -1) if reverse else range(chunks)
    for ci in order:
        rows = slice(ci * GLA_CHUNK, (ci + 1) * GLA_CHUNK)
        edge = ci * GLA_CHUNK if reverse else (ci + 1) * GLA_CHUNK - 1
        total = run[edge:edge + 1, :]
        k_dec = (k[rows] * jnp.exp(total - run[rows])).astype(_BF16)
        start_ref[0, ci] = _compact_state(state)
        state = _state_update(state, v_bf[rows], k_dec, jnp.exp(total), mask)
    st_scr[...] = state

    @pl.when(i == n_tiles - 1)
    def _():
        fin_ref[0] = state


def _state_call(x, mod, w, wg, bg, s0, *, tile, reverse):
    b, l, d = x.shape
    n_tiles = l // tile
    chunks = tile // GLA_CHUNK
    tmap = (lambda bi, i: (bi, n_tiles - 1 - i, 0)) if reverse else (lambda bi, i: (bi, i, 0))
    smap = (lambda bi, i: (bi, n_tiles - 1 - i, 0, 0)) if reverse else (lambda bi, i: (bi, i, 0, 0))
    st_spec = pl.BlockSpec((1, GLA_V, GLA_QK), lambda bi, i: (bi, 0, 0))
    kern = functools.partial(_state_kernel, tile=tile, reverse=reverse)
    return pl.pallas_call(
        kern,
        grid=(b, n_tiles),
        in_specs=[
            pl.BlockSpec((1, tile, d), tmap),
            pl.BlockSpec((1, 6, d), lambda bi, i: (bi, 0, 0)),
            _const_spec(w.shape), _const_spec(wg.shape), _const_spec(bg.shape),
            st_spec,
        ],
        out_specs=[
            pl.BlockSpec((1, chunks, GLA_DV, GLA_QK), smap),
            st_spec,
        ],
        out_shape=[
            jax.ShapeDtypeStruct((b, l // GLA_CHUNK, GLA_DV, GLA_QK), _F32),
            jax.ShapeDtypeStruct((b, GLA_V, GLA_QK), _F32),
        ],
        scratch_shapes=[pltpu.VMEM((GLA_V, GLA_QK), _F32)],
        compiler_params=pltpu.CompilerParams(
            dimension_semantics=("arbitrary", "arbitrary"),
            vmem_limit_bytes=VMEM_LIMIT),
        name="gla_state_rev" if reverse else "gla_state_fwd",
    )(x, mod, w, wg, bg, s0)


GROUP_W = 256
COL_Q, COL_K, COL_V, COL_R, COL_LOW, COL_C, COL_B, COL_D = 0, 128, 256, 512, 768, 896, 1408, 1664
IN_COLS = 2176
HALO = 16
POOL_REACH = 8
MIX_BLOCK = 128
CONV_WIDTH = 31
SGU_HEADS = 4
SGU_HD = 64


def _gelu_tanh(x):
    return 0.5 * x * (1.0 + jnp.tanh(0.7978845608028654 * (x + 0.044715 * (x * x * x))))


def _mixer_kernel(xm_ref, xp_ref, xn_ref, mod_ref, win_ref, wg_ref, bg_ref, gnw_ref,
                  pblk_ref, pscale_ref, sguw_ref, sgub_ref, slnw_ref, slnb_ref,
                  cw_ref, cb_ref, clnw_ref, clnb_ref, wout_ref, plw_ref, plb_ref,
                  sb_ref, sf0_ref,
                  o_ref, sfin_ref,
                  z_scr, zb_scr, y_scr, run_scr, o_scr, cat_scr, st_scr,
                  *, tile, seq_len, alpha):
    i = pl.program_id(1)
    n_tiles = pl.num_programs(1)
    n_blocks = tile // MIX_BLOCK

    @pl.when(i == 0)
    def _():
        st_scr[...] = sf0_ref[0]

    shift = mod_ref[0, 0:1, :]
    scale1 = 1.0 + mod_ref[0, 1:2, :]
    gate = mod_ref[0, 2:3, :]

    def modulated(xv):
        return _norm_rows(xv) * scale1 + shift

    hp = jnp.where(i > 0, modulated(xp_ref[0]), 0.0).astype(_BF16)
    hn = jnp.where(i < n_tiles - 1, modulated(xn_ref[0]), 0.0).astype(_BF16)
    zh = jnp.dot(jnp.concatenate([hp, hn], axis=0), win_ref[:, COL_B:],
                 preferred_element_type=_F32)

    def glu(zd):
        return zd[:, :GROUP_W] * _sigmoid(zd[:, GROUP_W:])

    zb_scr[0:HALO, :] = zh[0:HALO, 0:GROUP_W]
    zb_scr[HALO + tile:, :] = zh[HALO:, 0:GROUP_W]
    y_scr[0:HALO, :] = glu(zh[0:HALO, GROUP_W:])
    y_scr[HALO + tile:, :] = glu(zh[HALO:, GROUP_W:])

    def block_rows(blk):
        return slice(blk * MIX_BLOCK, (blk + 1) * MIX_BLOCK)

    def in_proj(blk):
        rows = block_rows(blk)
        ext_rows = slice(HALO + blk * MIX_BLOCK, HALO + (blk + 1) * MIX_BLOCK)
        h = modulated(xm_ref[0, rows, :]).astype(_BF16)
        z = jnp.dot(h, win_ref[...], preferred_element_type=_F32)
        z_scr[rows, :] = z
        zb_scr[ext_rows, :] = z[:, COL_B:COL_B + GROUP_W]
        y_scr[ext_rows, :] = glu(z[:, COL_D:COL_D + 2 * GROUP_W])
        low = z[:, COL_LOW:COL_LOW + LANES].astype(_BF16)
        g = _log_decay(jnp.dot(low, wg_ref[...], preferred_element_type=_F32) + bg_ref[...])
        run_scr[rows, 0:GLA_QK] = _chunk_cumsum(g[:, 0:GLA_QK], False)
        run_scr[rows, GLA_QK:] = _chunk_cumsum(g[:, GLA_QK:], True)

    lane = lax.broadcasted_iota(jnp.int32, (MIX_BLOCK, GROUP_W), 1)
    row = lax.broadcasted_iota(jnp.int32, (MIX_BLOCK, GROUP_W), 0)
    lane_group = lane // SGU_HD
    pool_half = jnp.left_shift(1, lane_group)

    def local_mixers(blk):
        r0 = blk * MIX_BLOCK
        rows = block_rows(blk)

        n = MIX_BLOCK + 2 * POOL_REACH
        xb = zb_scr[r0 + HALO - POOL_REACH:r0 + HALO - POOL_REACH + n, :]
        p2 = xb + pltpu.roll(xb, 1, axis=0)
        p4 = pltpu.roll(p2, 1, axis=0) + pltpu.roll(p2, n - 1, axis=0)
        p8 = pltpu.roll(p4, 2, axis=0) + pltpu.roll(p4, n - 2, axis=0)
        p16 = pltpu.roll(p8, 4, axis=0) + pltpu.roll(p8, n - 4, axis=0)
        inner = slice(POOL_REACH, POOL_REACH + MIX_BLOCK)
        win = jnp.where(lane_group == 0, p2[inner],
                        jnp.where(lane_group == 1, p4[inner],
                                  jnp.where(lane_group == 2, p8[inner], p16[inner])))
        tok = i * tile + r0 + row
        count = jnp.minimum(tok + pool_half, seq_len) - jnp.maximum(tok - pool_half, 0)
        pooled = win / count.astype(_F32) - xb[inner]
        yb = jnp.dot(pooled.astype(_BF16), pblk_ref[...], preferred_element_type=_F32) * pscale_ref[...]
        cat_scr[rows, GROUP_W:2 * GROUP_W] = yb.astype(_BF16)

        gz = _gelu_tanh(z_scr[rows, COL_C:COL_C + 2 * GROUP_W])
        vn = _norm_rows(gz[:, GROUP_W:]) * slnw_ref[...] + slnb_ref[...]
        vstack = jnp.concatenate(
            [jnp.where(lane_group == hd, vn, 0.0) for hd in range(SGU_HEADS)], axis=0).astype(_BF16)
        sg = jnp.dot(sguw_ref[...], vstack, preferred_element_type=_F32) + sgub_ref[...]
        cat_scr[rows, 2 * GROUP_W:3 * GROUP_W] = (gz[:, :GROUP_W] * sg).astype(_BF16)

        span = MIX_BLOCK + SUBLANES
        halves = []
        for lo in range(0, GROUP_W, LANES):
            conv = None
            for b in range(SUBLANES):
                part = None
                for a in range(4):
                    o = SUBLANES * a + b
                    if 1 <= o <= CONV_WIDTH:
                        term = (cw_ref[o - 1:o, lo:lo + LANES]
                                * y_scr[r0 + SUBLANES * a:r0 + SUBLANES * a + span, lo:lo + LANES])
                        part = term if part is None else part + term
                part = part[b:b + MIX_BLOCK]
                conv = part if conv is None else conv + part
            halves.append(conv)
        conv = jnp.concatenate(halves, axis=1)
        yd = _norm_rows(conv + cb_ref[...]) * clnw_ref[...] + clnb_ref[...]
        cat_scr[rows, 3 * GROUP_W:4 * GROUP_W] = (yd * _sigmoid(yd)).astype(_BF16)

    mask_s = _state_mask()
    vr = lax.broadcasted_iota(jnp.int32, (GLA_V, GLA_V), 0) // GLA_DV
    vc = lax.broadcasted_iota(jnp.int32, (GLA_V, GLA_V), 1) // GLA_DV
    mask_v = vr == vc
    t_id = lax.broadcasted_iota(jnp.int32, (GLA_CHUNK, GLA_V), 0)
    s_id = lax.broadcasted_iota(jnp.int32, (GLA_CHUNK, GLA_V), 1) % GLA_CHUNK
    nt_dims = (((1,), (1,)), ((), ()))

    zero = jnp.zeros((), _BF16)

    def chunk_rows(ci):
        return slice(ci * GLA_CHUNK, (ci + 1) * GLA_CHUNK)

    def scores(ci):
        r = chunk_rows(ci)
        q = z_scr[r, COL_Q:COL_Q + GLA_QK] * (GLA_DK ** -0.5)
        k = z_scr[r, COL_K:COL_K + GLA_QK]
        cf = run_scr[r, 0:GLA_QK]
        rb = run_scr[r, GLA_QK:]
        qf = (q * jnp.exp(cf)).astype(_BF16)
        kf = (k * jnp.exp(-cf)).astype(_BF16)
        qb = (q * jnp.exp(rb)).astype(_BF16)
        kb = (k * jnp.exp(-rb)).astype(_BF16)
        a_f = lax.dot_general(qf, jnp.where(mask_s, _stack4(kf), zero), nt_dims,
                              preferred_element_type=_F32)
        a_b = lax.dot_general(qb, jnp.where(mask_s, _stack4(kb), zero), nt_dims,
                              preferred_element_type=_F32)
        a = (jnp.where(s_id <= t_id, a_f, 0.0) + jnp.where(s_id >= t_id, a_b, 0.0)).astype(_BF16)
        return a, qf, qb

    def outputs(ci, a, qf, qb):
        r = chunk_rows(ci)
        k = z_scr[r, COL_K:COL_K + GLA_QK]
        v = z_scr[r, COL_V:COL_V + GLA_V].astype(_BF16)
        cf = run_scr[r, 0:GLA_QK]
        total = cf[GLA_CHUNK - 1:GLA_CHUNK, :]
        k_dec = (k * jnp.exp(total - cf)).astype(_BF16)
        o = jnp.dot(a, jnp.where(mask_v, _stack4(v), zero), preferred_element_type=_F32)
        sf = st_scr[...]
        sb = jnp.where(mask_s, _stack4(sb_ref[0, ci]), 0.0)
        o = o + lax.dot_general(qf, sf.astype(_BF16), nt_dims, preferred_element_type=_F32)
        o = o + lax.dot_general(qb, sb.astype(_BF16), nt_dims, preferred_element_type=_F32)
        o_scr[r, :] = o
        st_scr[...] = _state_update(sf, v, k_dec, jnp.exp(total), mask_s)

    head_mean = jnp.where(mask_v, 1.0 / GLA_DV, 0.0).astype(_BF16)

    def readout(blk):
        rows = block_rows(blk)
        o = o_scr[rows, :]
        sq = o * o
        sq_hi = sq.astype(_BF16)
        sq_lo = (sq - sq_hi.astype(_F32)).astype(_BF16)
        ms = (jnp.dot(sq_hi, head_mean, preferred_element_type=_F32)
              + jnp.dot(sq_lo, head_mean, preferred_element_type=_F32))
        rg = z_scr[rows, COL_R:COL_R + GROUP_W]
        ya = o * lax.rsqrt(ms + EPS) * gnw_ref[...] * (rg * _sigmoid(rg))
        cat_scr[rows, 0:GROUP_W] = ya.astype(_BF16)

    def out_proj(blk):
        rows = block_rows(blk)
        y = jnp.dot(cat_scr[rows, :], wout_ref[...], preferred_element_type=_F32)
        o_ref[0, rows, :] = (_norm_rows(alpha * xm_ref[0, rows, :] + gate * y) * plw_ref[...]
                             + plb_ref[...])

    chunks_per_block = MIX_BLOCK // GLA_CHUNK
    n_gla_chunks = tile // GLA_CHUNK
    in_proj(0)
    if n_blocks > 1:
        in_proj(1)
    cur = scores(0)
    for blk in range(n_blocks):
        if blk + 2 < n_blocks:
            in_proj(blk + 2)
        local_mixers(blk)
        for ci in range(blk * chunks_per_block, (blk + 1) * chunks_per_block):
            nxt = scores(ci + 1) if ci + 1 < n_gla_chunks else None
            outputs(ci, *cur)
            cur = nxt
        readout(blk)
        out_proj(blk)

    @pl.when(i == n_tiles - 1)
    def _():
        sfin_ref[0] = st_scr[...]


def _mixer_call(x, mod, mw, sb_start, sf0, *, tile, alpha):
    b, l, d = x.shape
    n_tiles = l // tile
    halo_per_tile = tile // HALO
    n_halo_blocks = l // HALO
    consts = [mw[k] for k in ("w_in", "wg2", "bg2", "gnw", "pool_blk", "pool_scale", "sgu_w", "sgu_b",
                              "sgu_ln_w", "sgu_ln_b", "cm_w", "cm_b", "cm_ln_w", "cm_ln_b", "w_out",
                              "post_w", "post_b")]
    x_spec = pl.BlockSpec((1, tile, d), lambda bi, i: (bi, i, 0))
    st_spec = pl.BlockSpec((1, GLA_V, GLA_QK), lambda bi, i: (bi, 0, 0))
    in_specs = [
        x_spec,
        pl.BlockSpec((1, HALO, d), lambda bi, i: (bi, jnp.maximum(i * halo_per_tile - 1, 0), 0)),
        pl.BlockSpec((1, HALO, d), lambda bi, i: (bi, jnp.minimum((i + 1) * halo_per_tile, n_halo_blocks - 1), 0)),
        pl.BlockSpec((1, 6, d), lambda bi, i: (bi, 0, 0)),
    ] + [_const_spec(c.shape) for c in consts] + [
        pl.BlockSpec((1, tile // GLA_CHUNK, GLA_DV, GLA_QK), lambda bi, i: (bi, i, 0, 0)),
        st_spec,
    ]
    kern = functools.partial(_mixer_kernel, tile=tile, seq_len=l, alpha=alpha)
    return pl.pallas_call(
        kern,
        grid=(b, n_tiles),
        in_specs=in_specs,
        out_specs=[x_spec, st_spec],
        out_shape=[jax.ShapeDtypeStruct(x.shape, _F32),
                   jax.ShapeDtypeStruct((b, GLA_V, GLA_QK), _F32)],
        scratch_shapes=[
            pltpu.VMEM((tile, IN_COLS), _F32),
            pltpu.VMEM((tile + 2 * HALO, GROUP_W), _F32),
            pltpu.VMEM((tile + 2 * HALO, GROUP_W), _F32),
            pltpu.VMEM((tile, 2 * GLA_QK), _F32),
            pltpu.VMEM((tile, GLA_V), _F32),
            pltpu.VMEM((tile, 4 * GROUP_W), _BF16),
            pltpu.VMEM((GLA_V, GLA_QK), _F32),
        ],
        compiler_params=pltpu.CompilerParams(
            dimension_semantics=("arbitrary", "arbitrary"),
            vmem_limit_bytes=VMEM_LIMIT),
        name="token_mixer",
    )(x, x, x, mod, *consts, sb_start, sf0)


def _prep_mixer_weights(w_in, gla_w_gate, gla_b_gate, gla_norm_w, pool_w, pool_scale, sgu_w, sgu_b,
                        sgu_ln_w, sgu_ln_b, cm_conv_w, cm_conv_b, cm_ln_w, cm_ln_b, w_out,
                        post_w, post_b):
    d = w_in.shape[0]
    qkvr = w_in[:, 0:768]
    low = w_in[:, 768:800]
    zb = w_in[:, 800:1056]
    zc = w_in[:, 1056:1568]
    zd = w_in[:, 1568:2080]
    pad = jnp.zeros((d, LANES - 2 * GLA_RANK), _F32)
    w_main = jnp.concatenate([qkvr, low, pad, zc, zb, zd], axis=1).astype(_BF16)
    kv = w_in[:, 128:512]
    pad1 = jnp.zeros((d, LANES - GLA_RANK), _F32)
    w_state = [jnp.concatenate([kv, low[:, di * GLA_RANK:(di + 1) * GLA_RANK], pad1], axis=1).astype(_BF16)
               for di in range(2)]
    wg1 = [jnp.zeros((LANES, GLA_QK), _F32).at[0:GLA_RANK].set(gla_w_gate[di]).astype(_BF16)
           for di in range(2)]
    bg1 = [gla_b_gate[di][None, :] for di in range(2)]
    wg2 = jnp.zeros((LANES, 2 * GLA_QK), _F32)
    wg2 = wg2.at[0:GLA_RANK, 0:GLA_QK].set(gla_w_gate[0])
    wg2 = wg2.at[GLA_RANK:2 * GLA_RANK, GLA_QK:].set(gla_w_gate[1]).astype(_BF16)
    bg2 = jnp.concatenate([gla_b_gate[0], gla_b_gate[1]])[None, :]
    pool_blk = jnp.zeros((GROUP_W, GROUP_W), _F32)
    for gi in range(pool_w.shape[0]):
        sl = slice(gi * SGU_HD, (gi + 1) * SGU_HD)
        pool_blk = pool_blk.at[sl, sl].set(pool_w[gi])
    row = lambda a: a[None, :]
    return {
        "w_in": w_main, "w_state": w_state, "wg1": wg1, "bg1": bg1, "wg2": wg2, "bg2": bg2,
        "gnw": row(gla_norm_w), "pool_blk": pool_blk.astype(_BF16), "pool_scale": row(pool_scale),
        "sgu_w": jnp.concatenate([sgu_w[hd] for hd in range(SGU_HEADS)], axis=1).astype(_BF16),
        "sgu_b": jnp.repeat(sgu_b.T, SGU_HD, axis=1),
        "sgu_ln_w": row(sgu_ln_w), "sgu_ln_b": row(sgu_ln_b),
        "cm_w": jnp.concatenate([cm_conv_w, jnp.zeros((1, GROUP_W), _F32)], axis=0),
        "cm_b": row(cm_conv_b), "cm_ln_w": row(cm_ln_w), "cm_ln_b": row(cm_ln_b),
        "w_out": w_out.astype(_BF16), "post_w": row(post_w), "post_b": row(post_b),
    }


def _mod_kernel(c_ref, w_ref, b_ref, o_ref):
    cv = c_ref[...]
    o_ref[0] = jnp.dot(cv * _sigmoid(cv), w_ref[0], preferred_element_type=_F32,
                       precision=lax.Precision.HIGHEST) + b_ref[0]


def _mod_call(cond, w_mod, b_mod):
    depth, d, six_d = w_mod.shape
    rows = cond.shape[0]
    return pl.pallas_call(
        _mod_kernel,
        grid=(depth, six_d // d),
        in_specs=[
            pl.BlockSpec((rows, d), lambda li, j: (0, 0)),
            pl.BlockSpec((1, d, d), lambda li, j: (li, 0, j)),
            pl.BlockSpec((1, 1, d), lambda li, j: (li, 0, j)),
        ],
        out_specs=pl.BlockSpec((1, rows, d), lambda li, j: (li, 0, j)),
        out_shape=jax.ShapeDtypeStruct((depth, rows, six_d), _F32),
        compiler_params=pltpu.CompilerParams(
            dimension_semantics=("arbitrary", "arbitrary"),
            vmem_limit_bytes=VMEM_LIMIT),
        name="adaln_modulation",
    )(cond, w_mod, b_mod.reshape(depth, 1, six_d))


def _pick_tile(length, target):
    return min(length, target)


def kernel(x, c, ctx, c_ctx, w_mod, b_mod, w_in, gla_w_gate, gla_b_gate, gla_norm_w, pool_w, pool_scale, sgu_w, sgu_b, sgu_ln_w, sgu_ln_b, cm_conv_w, cm_conv_b, cm_ln_w, cm_ln_b, w_out, ffn_w_up, ffn_conv_w, ffn_w_down, post_ln_w, post_ln_b):
    batch, seq, d = x.shape
    ctx_len = ctx.shape[1]
    depth = w_mod.shape[0]
    alpha = (2 * depth) ** 0.25

    cond_rows = -(-(batch + 1) // SUBLANES) * SUBLANES
    cond = jnp.concatenate([c, c_ctx[None, :], jnp.zeros((cond_rows - batch - 1, d), _F32)], axis=0)
    mod = _mod_call(cond, w_mod, b_mod)

    lat_mix_tile = _pick_tile(seq, MIX_TILE)
    ctx_mix_tile = _pick_tile(ctx_len, MIX_TILE)
    lat_ffn_tile = _pick_tile(seq, FFN_TILE)
    zero_state = jnp.zeros((batch, GLA_V, GLA_QK), _F32)

    for li in range(depth):
        with_ctx_out = li < depth - 1
        mod_lat = mod[li, :batch].reshape(batch, 6, d)
        mod_ctx = jnp.broadcast_to(mod[li, batch].reshape(1, 6, d), (batch, 6, d))
        mw = _prep_mixer_weights(w_in[li], gla_w_gate[li], gla_b_gate[li], gla_norm_w[li], pool_w[li],
                                 pool_scale[li], sgu_w[li], sgu_b[li], sgu_ln_w[li], sgu_ln_b[li],
                                 cm_conv_w[li], cm_conv_b[li], cm_ln_w[li], cm_ln_b[li], w_out[li],
                                 post_ln_w[li, 0], post_ln_b[li, 0])
        ffn_w = _prep_ffn_weights(ffn_w_up[li], ffn_conv_w[li], ffn_w_down[li])
        ffn_ln = (post_ln_w[li, 1][None, :], post_ln_b[li, 1][None, :])

        ctx_sb, ctx_sb_fin = _state_call(ctx, mod_ctx, mw["w_state"][1], mw["wg1"][1], mw["bg1"][1],
                                         zero_state, tile=ctx_mix_tile, reverse=True)
        if with_ctx_out:
            ctx_mixed, ctx_sf_fin = _mixer_call(ctx, mod_ctx, mw, ctx_sb, zero_state,
                                                tile=ctx_mix_tile, alpha=alpha)
        else:
            _, ctx_sf_fin = _state_call(ctx, mod_ctx, mw["w_state"][0], mw["wg1"][0], mw["bg1"][0],
                                        zero_state, tile=ctx_mix_tile, reverse=False)

        lat_sb, _ = _state_call(x, mod_lat, mw["w_state"][1], mw["wg1"][1], mw["bg1"][1],
                                ctx_sb_fin, tile=lat_mix_tile, reverse=True)
        x, _ = _mixer_call(x, mod_lat, mw, lat_sb, ctx_sf_fin, tile=lat_mix_tile, alpha=alpha)
        x = _ffn_call(x, mod_lat, *ffn_w, *ffn_ln, tile=lat_ffn_tile, gw=GRID_W, alpha=alpha)
        if with_ctx_out:
            ctx = _ffn_call(ctx_mixed, mod_ctx, *ffn_w, *ffn_ln, tile=ctx_len, gw=ctx_len, alpha=alpha)
    return x
```

```python
import functools

import jax
import jax.numpy as jnp
from jax import lax
from jax.experimental import pallas as pl
from jax.experimental.pallas import tpu as pltpu

EPS = 1e-6
LANES = 128
SUBLANES = 8
GRID_W = 64
FFN_CHUNK = 256
FFN_GRANULE = 512
BF16_ROWS = 16
MIX_TILE = 512
FFN_TILE = 1024
VMEM_LIMIT = 60 * 1024 * 1024

_F32 = jnp.float32
_BF16 = jnp.bfloat16


def _norm_rows(x):
    mu = jnp.mean(x, axis=-1, keepdims=True)
    xc = x - mu
    var = jnp.mean(xc * xc, axis=-1, keepdims=True)
    return xc * lax.rsqrt(var + EPS)


def _sigmoid(x):
    return 1.0 / (1.0 + jnp.exp(-x))


def _pick_spec(arr, *lead):
    rest = arr.shape[len(lead):]
    index = tuple(lead) + (0,) * len(rest)
    return pl.BlockSpec((None,) * len(lead) + rest, lambda *_: index, pipeline_mode=pl.Buffered(1))


def _mod_spec(li, ctx_row):
    if ctx_row is None:
        return lambda mod: pl.BlockSpec((None, 1) + mod.shape[2:], lambda bi, i: (li, bi, 0, 0))
    return lambda mod: pl.BlockSpec((None, 1) + mod.shape[2:], lambda bi, i: (li, ctx_row, 0, 0))


def _ffn_kernel(*refs, tile, gw, n_tiles, n_gran, has_halo, alpha):
    if has_halo:
        (xm_ref, xp_ref, xn_ref, mod_ref, wup_ref, wcv_ref, wdn_ref, lnw_ref, lnb_ref,
         o_ref, h_scr, u0_scr, u1_scr, s_scr, acc_scr) = refs
    else:
        (xm_ref, mod_ref, wup_ref, wcv_ref, wdn_ref, lnw_ref, lnb_ref,
         o_ref, h_scr, u0_scr, u1_scr, s_scr, acc_scr) = refs
    u_bufs = (u0_scr, u1_scr)
    i = pl.program_id(1)
    n_chunks = wup_ref.shape[0]
    pad = gw if has_halo else 0
    n_rows = tile // gw

    shift = mod_ref[0, 3:4, :]
    scale1 = 1.0 + mod_ref[0, 4:5, :]
    gate = mod_ref[0, 5:6, :]

    def modulated(xv):
        return _norm_rows(xv) * scale1 + shift

    ext = h_scr.shape[0]
    up_rows = ext // n_gran
    rows_per_gran = n_rows // n_gran
    gran_rows = rows_per_gran * gw

    def modulate_granule(g):
        rows = slice(g * gran_rows, (g + 1) * gran_rows)
        h_scr[pad + g * gran_rows:pad + (g + 1) * gran_rows, :] = modulated(xm_ref[0, rows, :]).astype(_BF16)
        if has_halo and g == 0:
            hp = jnp.where(i > 0, modulated(xp_ref[0]), 0.0)
            h_scr[0:gw, :] = hp.astype(_BF16)
        if has_halo and g == n_gran - 1:
            hn = jnp.where(i < n_tiles - 1, modulated(xn_ref[0]), 0.0)
            h_scr[pad + tile:pad + tile + gw, :] = hn.astype(_BF16)

    def up_proj(c, buf, g):
        rows = pl.ds(pl.multiple_of(g * up_rows, BF16_ROWS), up_rows)
        u_bufs[buf][rows, :] = jnp.dot(h_scr[rows, :], wup_ref[c], preferred_element_type=_F32)

    edge = jnp.zeros((1, LANES), _F32)

    def shift_tokens(v, down):
        if down:
            return jnp.concatenate([edge, v[:-1]], axis=0)
        return jnp.concatenate([v[1:], edge], axis=0)

    def conv_cols(ub, wc, base, lo):
        def tap(k, off):
            return wc[k:k + 1, lo:lo + LANES] * ub[pl.ds(base + off, gw), lo:lo + LANES]

        def col_taps(dc):
            k = dc + 1
            v = tap(3 + k, pad)
            if has_halo:
                v = v + tap(k, 0) + tap(6 + k, 2 * gw)
            return v
        return col_taps(0) + shift_tokens(col_taps(-1), True) + shift_tokens(col_taps(1), False)

    def conv_gate(c, buf, g):
        ub = u_bufs[buf]
        wc = wcv_ref[c]
        for r in range(rows_per_gran):
            base = pl.multiple_of(g * gran_rows + r * gw, gw)
            for j in range(FFN_CHUNK // LANES):
                a = conv_cols(ub, wc, base, j * LANES)
                gt = conv_cols(ub, wc, base, FFN_CHUNK + j * LANES)
                s_scr[pl.ds(base, gw), j * LANES:(j + 1) * LANES] = (a * gt * _sigmoid(gt)).astype(_BF16)

    def down_proj(c, g):
        rows = pl.ds(pl.multiple_of(g * gran_rows, gran_rows), gran_rows)
        acc_scr[rows, :] += jnp.dot(s_scr[rows, :], wdn_ref[c], preferred_element_type=_F32)

    def chunk_steps(c, buf, has_next):
        def step(g, carry):
            if has_next:
                up_proj(c + 1, 1 - buf, g)
            wrap = jnp.where(g == 0, 1, 0)
            down_proj(jnp.maximum(c - wrap, 0), g - 1 + wrap * n_gran)
            conv_gate(c, buf, g)
            return carry
        lax.fori_loop(0, n_gran, step, 0)

    acc_scr[...] = jnp.zeros_like(acc_scr)
    s_scr[...] = jnp.zeros_like(s_scr)
    done = 0
    for g in range(n_gran):
        need = min(n_gran, -(-((g + 1) * up_rows - pad) // gran_rows))
        for gm in range(done, need):
            modulate_granule(gm)
        done = max(done, need)
        up_proj(0, 0, g)

    def body(k, carry):
        c = 2 * k
        chunk_steps(c, 0, True)
        chunk_steps(c + 1, 1, True)
        return carry

    n_pairs = (n_chunks - 1) // 2
    lax.fori_loop(0, n_pairs, body, 0)
    if n_chunks % 2 == 0:
        chunk_steps(n_chunks - 2, 0, True)
    chunk_steps(n_chunks - 1, (n_chunks - 1) % 2, False)
    down_proj(n_chunks - 1, n_gran - 1)

    for g in range(n_gran):
        rows = slice(g * gran_rows, (g + 1) * gran_rows)
        y = alpha * xm_ref[0, rows, :] + gate * acc_scr[rows, :]
        o_ref[0, rows, :] = _norm_rows(y) * lnw_ref[...] + lnb_ref[...]


def _ffn_call(x, mod, fw, *, li, ctx_row, tile, gw, alpha):
    wup, wcv, wdn, post_w, post_b = fw
    b, l, d = x.shape
    n_tiles = l // tile
    has_halo = l > gw
    rows_per_tile = tile // gw
    n_grid_rows = l // gw
    ext = tile + (2 * gw if has_halo else 0)

    x_spec = pl.BlockSpec((1, tile, d), lambda bi, i: (bi, i, 0))
    in_specs = [x_spec]
    args = [x]
    if has_halo:
        in_specs.append(pl.BlockSpec(
            (1, gw, d), lambda bi, i: (bi, jnp.maximum(i * rows_per_tile - 1, 0), 0)))
        in_specs.append(pl.BlockSpec(
            (1, gw, d), lambda bi, i: (bi, jnp.minimum((i + 1) * rows_per_tile, n_grid_rows - 1), 0)))
        args += [x, x]
    in_specs += [
        _mod_spec(li, ctx_row)(mod),
        _pick_spec(wup, li), _pick_spec(wcv, li), _pick_spec(wdn, li),
        _pick_spec(post_w, li, 1), _pick_spec(post_b, li, 1),
    ]
    args += [mod, wup, wcv, wdn, post_w, post_b]
    n_gran = max(1, tile // FFN_GRANULE)
    assert ext % (n_gran * BF16_ROWS) == 0 and rows_per_tile % n_gran == 0
    kern = functools.partial(_ffn_kernel, tile=tile, gw=gw, n_tiles=n_tiles, n_gran=n_gran,
                             has_halo=has_halo, alpha=alpha)
    return pl.pallas_call(
        kern,
        grid=(b, n_tiles),
        in_specs=in_specs,
        out_specs=x_spec,
        out_shape=jax.ShapeDtypeStruct(x.shape, _F32),
        scratch_shapes=[
            pltpu.VMEM((ext, d), _BF16),
            pltpu.VMEM((ext, 2 * FFN_CHUNK), _F32),
            pltpu.VMEM((ext, 2 * FFN_CHUNK), _F32),
            pltpu.VMEM((tile, FFN_CHUNK), _BF16),
            pltpu.VMEM((tile, d), _F32),
        ],
        compiler_params=pltpu.CompilerParams(
            dimension_semantics=("arbitrary", "arbitrary"),
            vmem_limit_bytes=VMEM_LIMIT),
        name="channel_mixer",
    )(*args)


def _prep_ffn_weights(w_up, conv_w, w_down):
    depth, d, two_h = w_up.shape
    n_chunks = two_h // 2 // FFN_CHUNK
    wup = w_up.reshape(depth, d, 2, n_chunks, FFN_CHUNK).transpose(0, 3, 1, 2, 4)
    wup = wup.reshape(depth, n_chunks, d, 2 * FFN_CHUNK).astype(_BF16)
    wcv = conv_w.reshape(depth, 9, 2, n_chunks, FFN_CHUNK).transpose(0, 3, 1, 2, 4)
    wcv = wcv.reshape(depth, n_chunks, 9, 2 * FFN_CHUNK)
    wdn = w_down.reshape(depth, n_chunks, FFN_CHUNK, w_down.shape[-1]).astype(_BF16)
    return wup, wcv, wdn


GLA_HEADS = 4
GLA_DK = 32
GLA_DV = 64
GLA_QK = GLA_HEADS * GLA_DK
GLA_V = GLA_HEADS * GLA_DV
GLA_RANK = 16
GLA_CHUNK = 64
GLA_GATE_NORMALIZER = 16.0
GLA_LOG_DECAY_MIN = -1.0


def _log_decay(logit):
    log_sig = jnp.minimum(logit, 0.0) - jnp.log(1.0 + jnp.exp(-jnp.abs(logit)))
    return jnp.maximum(log_sig / GLA_GATE_NORMALIZER, GLA_LOG_DECAY_MIN)


def _chunk_cumsum(x, reverse):
    n = x.shape[0]
    pos = lax.broadcasted_iota(jnp.int32, x.shape, 0) % GLA_CHUNK
    s = 1
    while s < GLA_CHUNK:
        if reverse:
            x = x + jnp.where(pos < GLA_CHUNK - s, pltpu.roll(x, n - s, axis=0), 0.0)
        else:
            x = x + jnp.where(pos >= s, pltpu.roll(x, s, axis=0), 0.0)
        s *= 2
    return x


def _state_mask():
    r = lax.broadcasted_iota(jnp.int32, (GLA_V, GLA_QK), 0) // GLA_DV
    c = lax.broadcasted_iota(jnp.int32, (GLA_V, GLA_QK), 1) // GLA_DK
    return r == c


def _stack4(a):
    return jnp.concatenate([a, a, a, a], axis=0)


def _compact_state(s):
    return (s[0:GLA_DV] + s[GLA_DV:2 * GLA_DV]) + (s[2 * GLA_DV:3 * GLA_DV] + s[3 * GLA_DV:4 * GLA_DV])


def _state_update(state, v_bf, k_dec_bf, decay_row, mask):
    kv_t = lax.dot_general(v_bf, k_dec_bf, (((0,), (0,)), ((), ())), preferred_element_type=_F32)
    return state * decay_row + jnp.where(mask, kv_t, 0.0)


def _state_kernel(x_ref, mod_ref, w_ref, wg_ref, bg_ref, s0_ref, start_ref, fin_ref, st_scr,
                  *, tile, reverse):
    i = pl.program_id(1)
    n_tiles = pl.num_programs(1)
    chunks = tile // GLA_CHUNK

    @pl.when(i == 0)
    def _():
        st_scr[...] = s0_ref[0]

    shift = mod_ref[0, 0:1, :]
    scale1 = 1.0 + mod_ref[0, 1:2, :]
    h = (_norm_rows(x_ref[0]) * scale1 + shift).astype(_BF16)
    z = jnp.dot(h, w_ref[...], preferred_element_type=_F32)
    k = z[:, 0:GLA_QK]
    v_bf = z[:, GLA_QK:GLA_QK + GLA_V].astype(_BF16)
    low = z[:, GLA_QK + GLA_V:].astype(_BF16)
    g = _log_decay(jnp.dot(low, wg_ref[...], preferred_element_type=_F32) + bg_ref[...])
    run = _chunk_cumsum(g, reverse)
    mask = _state_mask()
    state = st_scr[...]
    order = range(chunks - 1, -1, -1) if reverse else range(chunks)
    for ci in order:
        rows = slice(ci * GLA_CHUNK, (ci + 1) * GLA_CHUNK)
        edge = ci * GLA_CHUNK if reverse else (ci + 1) * GLA_CHUNK - 1
        total = run[edge:edge + 1, :]
        k_dec = (k[rows] * jnp.exp(total - run[rows])).astype(_BF16)
        start_ref[0, ci] = _compact_state(state)
        state = _state_update(state, v_bf[rows], k_dec, jnp.exp(total), mask)
    st_scr[...] = state

    @pl.when(i == n_tiles - 1)
    def _():
        fin_ref[0] = state


def _state_call(x, mod, mw, s0, *, li, ctx_row, tile, reverse):
    di = 1 if reverse else 0
    w, wg, bg = mw["w_state"], mw["wg1"], mw["bg1"]
    b, l, d = x.shape
    n_tiles = l // tile
    chunks = tile // GLA_CHUNK
    tmap = (lambda bi, i: (bi, n_tiles - 1 - i, 0)) if reverse else (lambda bi, i: (bi, i, 0))
    smap = (lambda bi, i: (bi, n_tiles - 1 - i, 0, 0)) if reverse else (lambda bi, i: (bi, i, 0, 0))
    st_spec = pl.BlockSpec((1, GLA_V, GLA_QK), lambda bi, i: (bi, 0, 0))
    kern = functools.partial(_state_kernel, tile=tile, reverse=reverse)
    return pl.pallas_call(
        kern,
        grid=(b, n_tiles),
        in_specs=[
            pl.BlockSpec((1, tile, d), tmap),
            _mod_spec(li, ctx_row)(mod),
            _pick_spec(w, li, di), _pick_spec(wg, li, di), _pick_spec(bg, li, di),
            st_spec,
        ],
        out_specs=[
            pl.BlockSpec((1, chunks, GLA_DV, GLA_QK), smap),
            st_spec,
        ],
        out_shape=[
            jax.ShapeDtypeStruct((b, l // GLA_CHUNK, GLA_DV, GLA_QK), _F32),
            jax.ShapeDtypeStruct((b, GLA_V, GLA_QK), _F32),
        ],
        scratch_shapes=[pltpu.VMEM((GLA_V, GLA_QK), _F32)],
        compiler_params=pltpu.CompilerParams(
            dimension_semantics=("arbitrary", "arbitrary"),
            vmem_limit_bytes=VMEM_LIMIT),
        name="gla_state_rev" if reverse else "gla_state_fwd",
    )(x, mod, w, wg, bg, s0)


GROUP_W = 256
COL_Q, COL_K, COL_V, COL_R, COL_LOW, COL_C, COL_B, COL_D = 0, 128, 256, 512, 768, 896, 1408, 1664
IN_COLS = 2176
HALO = 16
POOL_REACH = 8
MIX_BLOCK = 128
CONV_WIDTH = 31
SGU_HEADS = 4
SGU_HD = 64


def _gelu_tanh(x):
    return 0.5 * x * (1.0 + jnp.tanh(0.7978845608028654 * (x + 0.044715 * (x * x * x))))


def _mixer_kernel(xm_ref, xp_ref, xn_ref, mod_ref, win_ref, wg_ref, bg_ref, gnw_ref,
                  pblk_ref, pscale_ref, sguw_ref, sgub_ref, slnw_ref, slnb_ref,
                  cw_ref, cb_ref, clnw_ref, clnb_ref, wout_ref, plw_ref, plb_ref,
                  sb_ref, sf0_ref,
                  o_ref, sfin_ref,
                  z_scr, zb_scr, y_scr, run_scr, o_scr, cat_scr, st_scr,
                  *, tile, seq_len, alpha):
    i = pl.program_id(1)
    n_tiles = pl.num_programs(1)
    n_blocks = tile // MIX_BLOCK

    @pl.when(i == 0)
    def _():
        st_scr[...] = sf0_ref[0]

    shift = mod_ref[0, 0:1, :]
    scale1 = 1.0 + mod_ref[0, 1:2, :]
    gate = mod_ref[0, 2:3, :]

    def modulated(xv):
        return _norm_rows(xv) * scale1 + shift

    hp = jnp.where(i > 0, modulated(xp_ref[0]), 0.0).astype(_BF16)
    hn = jnp.where(i < n_tiles - 1, modulated(xn_ref[0]), 0.0).astype(_BF16)
    zh = jnp.dot(jnp.concatenate([hp, hn], axis=0), win_ref[:, COL_B:],
                 preferred_element_type=_F32)

    def glu(zd):
        return zd[:, :GROUP_W] * _sigmoid(zd[:, GROUP_W:])

    zb_scr[0:HALO, :] = zh[0:HALO, 0:GROUP_W]
    zb_scr[HALO + tile:, :] = zh[HALO:, 0:GROUP_W]
    y_scr[0:HALO, :] = glu(zh[0:HALO, GROUP_W:])
    y_scr[HALO + tile:, :] = glu(zh[HALO:, GROUP_W:])

    def block_rows(blk):
        return slice(blk * MIX_BLOCK, (blk + 1) * MIX_BLOCK)

    def in_proj(blk):
        rows = block_rows(blk)
        ext_rows = slice(HALO + blk * MIX_BLOCK, HALO + (blk + 1) * MIX_BLOCK)
        h = modulated(xm_ref[0, rows, :]).astype(_BF16)
        z = jnp.dot(h, win_ref[...], preferred_element_type=_F32)
        z_scr[rows, :] = z
        zb_scr[ext_rows, :] = z[:, COL_B:COL_B + GROUP_W]
        y_scr[ext_rows, :] = glu(z[:, COL_D:COL_D + 2 * GROUP_W])
        low = z[:, COL_LOW:COL_LOW + LANES].astype(_BF16)
        g = _log_decay(jnp.dot(low, wg_ref[...], preferred_element_type=_F32) + bg_ref[...])
        run_scr[rows, 0:GLA_QK] = _chunk_cumsum(g[:, 0:GLA_QK], False)
        run_scr[rows, GLA_QK:] = _chunk_cumsum(g[:, GLA_QK:], True)

    lane = lax.broadcasted_iota(jnp.int32, (MIX_BLOCK, GROUP_W), 1)
    row = lax.broadcasted_iota(jnp.int32, (MIX_BLOCK, GROUP_W), 0)
    lane_group = lane // SGU_HD
    pool_half = jnp.left_shift(1, lane_group)

    def local_mixers(blk):
        r0 = blk * MIX_BLOCK
        rows = block_rows(blk)

        n = MIX_BLOCK + 2 * POOL_REACH
        xb = zb_scr[r0 + HALO - POOL_REACH:r0 + HALO - POOL_REACH + n, :]
        p2 = xb + pltpu.roll(xb, 1, axis=0)
        p4 = pltpu.roll(p2, 1, axis=0) + pltpu.roll(p2, n - 1, axis=0)
        p8 = pltpu.roll(p4, 2, axis=0) + pltpu.roll(p4, n - 2, axis=0)
        p16 = pltpu.roll(p8, 4, axis=0) + pltpu.roll(p8, n - 4, axis=0)
        inner = slice(POOL_REACH, POOL_REACH + MIX_BLOCK)
        win = jnp.where(lane_group == 0, p2[inner],
                        jnp.where(lane_group == 1, p4[inner],
                                  jnp.where(lane_group == 2, p8[inner], p16[inner])))
        tok = i * tile + r0 + row
        count = jnp.minimum(tok + pool_half, seq_len) - jnp.maximum(tok - pool_half, 0)
        pooled = win / count.astype(_F32) - xb[inner]
        yb = jnp.dot(pooled.astype(_BF16), pblk_ref[...], preferred_element_type=_F32) * pscale_ref[...]
        cat_scr[rows, GROUP_W:2 * GROUP_W] = yb.astype(_BF16)

        gz = _gelu_tanh(z_scr[rows, COL_C:COL_C + 2 * GROUP_W])
        vn = _norm_rows(gz[:, GROUP_W:]) * slnw_ref[...] + slnb_ref[...]
        vstack = jnp.concatenate(
            [jnp.where(lane_group == hd, vn, 0.0) for hd in range(SGU_HEADS)], axis=0).astype(_BF16)
        sg = jnp.dot(sguw_ref[...], vstack, preferred_element_type=_F32) + sgub_ref[...]
        cat_scr[rows, 2 * GROUP_W:3 * GROUP_W] = (gz[:, :GROUP_W] * sg).astype(_BF16)

        span = MIX_BLOCK + SUBLANES
        halves = []
        for lo in range(0, GROUP_W, LANES):
            conv = None
            for b in range(SUBLANES):
                part = None
                for a in range(4):
                    o = SUBLANES * a + b
                    if 1 <= o <= CONV_WIDTH:
                        term = (cw_ref[o - 1:o, lo:lo + LANES]
                                * y_scr[r0 + SUBLANES * a:r0 + SUBLANES * a + span, lo:lo + LANES])
                        part = term if part is None else part + term
                part = part[b:b + MIX_BLOCK]
                conv = part if conv is None else conv + part
            halves.append(conv)
        conv = jnp.concatenate(halves, axis=1)
        yd = _norm_rows(conv + cb_ref[...]) * clnw_ref[...] + clnb_ref[...]
        cat_scr[rows, 3 * GROUP_W:4 * GROUP_W] = (yd * _sigmoid(yd)).astype(_BF16)

    mask_s = _state_mask()
    vr = lax.broadcasted_iota(jnp.int32, (GLA_V, GLA_V), 0) // GLA_DV
    vc = lax.broadcasted_iota(jnp.int32, (GLA_V, GLA_V), 1) // GLA_DV
    mask_v = vr == vc
    t_id = lax.broadcasted_iota(jnp.int32, (GLA_CHUNK, GLA_V), 0)
    s_id = lax.broadcasted_iota(jnp.int32, (GLA_CHUNK, GLA_V), 1) % GLA_CHUNK
    nt_dims = (((1,), (1,)), ((), ()))

    zero = jnp.zeros((), _BF16)

    def chunk_rows(ci):
        return slice(ci * GLA_CHUNK, (ci + 1) * GLA_CHUNK)

    def scores(ci):
        r = chunk_rows(ci)
        q = z_scr[r, COL_Q:COL_Q + GLA_QK] * (GLA_DK ** -0.5)
        k = z_scr[r, COL_K:COL_K + GLA_QK]
        cf = run_scr[r, 0:GLA_QK]
        rb = run_scr[r, GLA_QK:]
        qf = (q * jnp.exp(cf)).astype(_BF16)
        kf = (k * jnp.exp(-cf)).astype(_BF16)
        qb = (q * jnp.exp(rb)).astype(_BF16)
        kb = (k * jnp.exp(-rb)).astype(_BF16)
        a_f = lax.dot_general(qf, jnp.where(mask_s, _stack4(kf), zero), nt_dims,
                              preferred_element_type=_F32)
        a_b = lax.dot_general(qb, jnp.where(mask_s, _stack4(kb), zero), nt_dims,
                              preferred_element_type=_F32)
        a = (jnp.where(s_id <= t_id, a_f, 0.0) + jnp.where(s_id >= t_id, a_b, 0.0)).astype(_BF16)
        return a, qf, qb

    def outputs(ci, a, qf, qb):
        r = chunk_rows(ci)
        k = z_scr[r, COL_K:COL_K + GLA_QK]
        v = z_scr[r, COL_V:COL_V + GLA_V].astype(_BF16)
        cf = run_scr[r, 0:GLA_QK]
        total = cf[GLA_CHUNK - 1:GLA_CHUNK, :]
        k_dec = (k * jnp.exp(total - cf)).astype(_BF16)
        o = jnp.dot(a, jnp.where(mask_v, _stack4(v), zero), preferred_element_type=_F32)
        sf = st_scr[...]
        sb = jnp.where(mask_s, _stack4(sb_ref[0, ci]), 0.0)
        o = o + lax.dot_general(qf, sf.astype(_BF16), nt_dims, preferred_element_type=_F32)
        o = o + lax.dot_general(qb, sb.astype(_BF16), nt_dims, preferred_element_type=_F32)
        o_scr[r, :] = o
        st_scr[...] = _state_update(sf, v, k_dec, jnp.exp(total), mask_s)

    head_mean = jnp.where(mask_v, 1.0 / GLA_DV, 0.0).astype(_BF16)

    def readout(blk):
        rows = block_rows(blk)
        o = o_scr[rows, :]
        sq = o * o
        sq_hi = sq.astype(_BF16)
        sq_lo = (sq - sq_hi.astype(_F32)).astype(_BF16)
        ms = (jnp.dot(sq_hi, head_mean, preferred_element_type=_F32)
              + jnp.dot(sq_lo, head_mean, preferred_element_type=_F32))
        rg = z_scr[rows, COL_R:COL_R + GROUP_W]
        ya = o * lax.rsqrt(ms + EPS) * gnw_ref[...] * (rg * _sigmoid(rg))
        cat_scr[rows, 0:GROUP_W] = ya.astype(_BF16)

    def out_proj(blk):
        rows = block_rows(blk)
        y = jnp.dot(cat_scr[rows, :], wout_ref[...], preferred_element_type=_F32)
        o_ref[0, rows, :] = (_norm_rows(alpha * xm_ref[0, rows, :] + gate * y) * plw_ref[...]
                             + plb_ref[...])

    chunks_per_block = MIX_BLOCK // GLA_CHUNK
    n_gla_chunks = tile // GLA_CHUNK
    in_proj(0)
    if n_blocks > 1:
        in_proj(1)
    cur = scores(0)
    for blk in range(n_blocks):
        if blk + 2 < n_blocks:
            in_proj(blk + 2)
        local_mixers(blk)
        for ci in range(blk * chunks_per_block, (blk + 1) * chunks_per_block):
            nxt = scores(ci + 1) if ci + 1 < n_gla_chunks else None
            outputs(ci, *cur)
            cur = nxt
        readout(blk)
        out_proj(blk)

    @pl.when(i == n_tiles - 1)
    def _():
        sfin_ref[0] = st_scr[...]


def _mixer_call(x, mod, mw, sb_start, sf0, *, li, ctx_row, tile, alpha):
    b, l, d = x.shape
    n_tiles = l // tile
    halo_per_tile = tile // HALO
    n_halo_blocks = l // HALO
    consts = [mw[k] for k in ("w_in", "wg2", "bg2", "gnw", "pool_blk", "pool_scale", "sgu_w", "sgu_b",
                              "sgu_ln_w", "sgu_ln_b", "cm_w", "cm_b", "cm_ln_w", "cm_ln_b", "w_out")]
    const_specs = [_pick_spec(c, li) for c in consts]
    consts += [mw["post_w"], mw["post_b"]]
    const_specs += [_pick_spec(mw["post_w"], li, 0), _pick_spec(mw["post_b"], li, 0)]
    x_spec = pl.BlockSpec((1, tile, d), lambda bi, i: (bi, i, 0))
    st_spec = pl.BlockSpec((1, GLA_V, GLA_QK), lambda bi, i: (bi, 0, 0))
    in_specs = [
        x_spec,
        pl.BlockSpec((1, HALO, d), lambda bi, i: (bi, jnp.maximum(i * halo_per_tile - 1, 0), 0)),
        pl.BlockSpec((1, HALO, d), lambda bi, i: (bi, jnp.minimum((i + 1) * halo_per_tile, n_halo_blocks - 1), 0)),
        _mod_spec(li, ctx_row)(mod),
    ] + const_specs + [
        pl.BlockSpec((1, tile // GLA_CHUNK, GLA_DV, GLA_QK), lambda bi, i: (bi, i, 0, 0)),
        st_spec,
    ]
    kern = functools.partial(_mixer_kernel, tile=tile, seq_len=l, alpha=alpha)
    return pl.pallas_call(
        kern,
        grid=(b, n_tiles),
        in_specs=in_specs,
        out_specs=[x_spec, st_spec],
        out_shape=[jax.ShapeDtypeStruct(x.shape, _F32),
                   jax.ShapeDtypeStruct((b, GLA_V, GLA_QK), _F32)],
        scratch_shapes=[
            pltpu.VMEM((tile, IN_COLS), _F32),
            pltpu.VMEM((tile + 2 * HALO, GROUP_W), _F32),
            pltpu.VMEM((tile + 2 * HALO, GROUP_W), _F32),
            pltpu.VMEM((tile, 2 * GLA_QK), _F32),
            pltpu.VMEM((tile, GLA_V), _F32),
            pltpu.VMEM((tile, 4 * GROUP_W), _BF16),
            pltpu.VMEM((GLA_V, GLA_QK), _F32),
        ],
        compiler_params=pltpu.CompilerParams(
            dimension_semantics=("arbitrary", "arbitrary"),
            vmem_limit_bytes=VMEM_LIMIT),
        name="token_mixer",
    )(x, x, x, mod, *consts, sb_start, sf0)


def _prep_mixer_weights(w_in, gla_w_gate, gla_b_gate, gla_norm_w, pool_w, pool_scale, sgu_w, sgu_b,
                        sgu_ln_w, sgu_ln_b, cm_conv_w, cm_conv_b, cm_ln_w, cm_ln_b, w_out,
                        post_w, post_b):
    depth, d, _ = w_in.shape
    n_pool = pool_w.shape[1]
    qkvr = w_in[:, :, 0:768]
    low = w_in[:, :, 768:800]
    zb = w_in[:, :, 800:1056]
    zc = w_in[:, :, 1056:1568]
    zd = w_in[:, :, 1568:2080]
    pad = jnp.zeros((depth, d, LANES - 2 * GLA_RANK), _F32)
    w_main = jnp.concatenate([qkvr, low, pad, zc, zb, zd], axis=2).astype(_BF16)
    kv = w_in[:, :, 128:512]
    pad1 = jnp.zeros((depth, d, LANES - GLA_RANK), _F32)
    w_state = jnp.stack(
        [jnp.concatenate([kv, low[:, :, di * GLA_RANK:(di + 1) * GLA_RANK], pad1], axis=2) for di in range(2)],
        axis=1).astype(_BF16)
    wg1 = jnp.pad(gla_w_gate, ((0, 0), (0, 0), (0, LANES - GLA_RANK), (0, 0))).astype(_BF16)
    zero = jnp.zeros((depth, GLA_RANK, GLA_QK), _F32)
    wg2 = jnp.concatenate([
        jnp.concatenate([gla_w_gate[:, 0], zero], axis=2),
        jnp.concatenate([zero, gla_w_gate[:, 1]], axis=2),
        jnp.zeros((depth, LANES - 2 * GLA_RANK, 2 * GLA_QK), _F32)], axis=1).astype(_BF16)
    eye = jnp.eye(n_pool, dtype=_F32)
    pool_blk = (pool_w[:, :, :, None, :] * eye[None, :, None, :, None]).reshape(depth, GROUP_W, GROUP_W)
    row = lambda a: a[:, None, :]
    return {
        "w_in": w_main, "w_state": w_state, "wg1": wg1, "bg1": gla_b_gate[:, :, None, :],
        "wg2": wg2, "bg2": gla_b_gate.reshape(depth, 1, 2 * GLA_QK),
        "gnw": row(gla_norm_w), "pool_blk": pool_blk.astype(_BF16), "pool_scale": row(pool_scale),
        "sgu_w": sgu_w.transpose(0, 2, 1, 3).reshape(depth, MIX_BLOCK, SGU_HEADS * MIX_BLOCK).astype(_BF16),
        "sgu_b": jnp.repeat(sgu_b.transpose(0, 2, 1), SGU_HD, axis=2),
        "sgu_ln_w": row(sgu_ln_w), "sgu_ln_b": row(sgu_ln_b),
        "cm_w": jnp.pad(cm_conv_w, ((0, 0), (0, 1), (0, 0))),
        "cm_b": row(cm_conv_b), "cm_ln_w": row(cm_ln_w), "cm_ln_b": row(cm_ln_b),
        "w_out": w_out.astype(_BF16),
        "post_w": post_w[:, :, None, :], "post_b": post_b[:, :, None, :],
    }


def _mod_kernel(c_ref, w_ref, b_ref, o_ref):
    cv = c_ref[...]

    def split(a):
        hi = a.astype(_BF16)
        return hi, (a - hi.astype(_F32)).astype(_BF16)

    a_hi, a_lo = split(cv * _sigmoid(cv))
    w_hi, w_lo = split(w_ref[0])
    dot = functools.partial(jnp.dot, preferred_element_type=_F32)
    o_ref[0] = (dot(a_hi, w_lo) + dot(a_lo, w_hi)) + dot(a_hi, w_hi) + b_ref[0]


def _mod_call(cond, w_mod, b_mod):
    depth, d, six_d = w_mod.shape
    rows = cond.shape[0]
    return pl.pallas_call(
        _mod_kernel,
        grid=(depth, six_d // d),
        in_specs=[
            pl.BlockSpec((rows, d), lambda li, j: (0, 0)),
            pl.BlockSpec((1, d, d), lambda li, j: (li, 0, j)),
            pl.BlockSpec((1, 1, d), lambda li, j: (li, 0, j)),
        ],
        out_specs=pl.BlockSpec((1, rows, d), lambda li, j: (li, 0, j)),
        out_shape=jax.ShapeDtypeStruct((depth, rows, six_d), _F32),
        compiler_params=pltpu.CompilerParams(
            dimension_semantics=("arbitrary", "arbitrary"),
            vmem_limit_bytes=VMEM_LIMIT),
        name="adaln_modulation",
    )(cond, w_mod, b_mod.reshape(depth, 1, six_d))


def _pick_tile(length, target):
    return min(length, target)


def kernel(x, c, ctx, c_ctx, w_mod, b_mod, w_in, gla_w_gate, gla_b_gate, gla_norm_w, pool_w, pool_scale, sgu_w, sgu_b, sgu_ln_w, sgu_ln_b, cm_conv_w, cm_conv_b, cm_ln_w, cm_ln_b, w_out, ffn_w_up, ffn_conv_w, ffn_w_down, post_ln_w, post_ln_b):
    batch, seq, d = x.shape
    ctx_len = ctx.shape[1]
    depth = w_mod.shape[0]
    alpha = (2 * depth) ** 0.25

    cond_rows = -(-(batch + 1) // SUBLANES) * SUBLANES
    cond = jnp.concatenate([c, c_ctx[None, :], jnp.zeros((cond_rows - batch - 1, d), _F32)], axis=0)
    mod = _mod_call(cond, w_mod, b_mod).reshape(depth, cond_rows, 6, d)

    lat_mix_tile = _pick_tile(seq, MIX_TILE)
    ctx_mix_tile = _pick_tile(ctx_len, MIX_TILE)
    lat_ffn_tile = _pick_tile(seq, FFN_TILE)
    zero_state = jnp.zeros((batch, GLA_V, GLA_QK), _F32)

    mw = _prep_mixer_weights(w_in, gla_w_gate, gla_b_gate, gla_norm_w, pool_w, pool_scale, sgu_w, sgu_b,
                             sgu_ln_w, sgu_ln_b, cm_conv_w, cm_conv_b, cm_ln_w, cm_ln_b, w_out,
                             post_ln_w, post_ln_b)
    fw = _prep_ffn_weights(ffn_w_up, ffn_conv_w, ffn_w_down) + (mw["post_w"], mw["post_b"])
    lat = dict(ctx_row=None)
    cx = dict(ctx_row=batch)

    for li in range(depth):
        with_ctx_out = li < depth - 1
        ctx_sb, ctx_sb_fin = _state_call(ctx, mod, mw, zero_state, li=li, tile=ctx_mix_tile,
                                         reverse=True, **cx)
        if with_ctx_out:
            ctx_mixed, ctx_sf_fin = _mixer_call(ctx, mod, mw, ctx_sb, zero_state, li=li,
                                                tile=ctx_mix_tile, alpha=alpha, **cx)
        else:
            _, ctx_sf_fin = _state_call(ctx, mod, mw, zero_state, li=li, tile=ctx_mix_tile,
                                        reverse=False, **cx)

        lat_sb, _ = _state_call(x, mod, mw, ctx_sb_fin, li=li, tile=lat_mix_tile, reverse=True, **lat)
        x, _ = _mixer_call(x, mod, mw, lat_sb, ctx_sf_fin, li=li, tile=lat_mix_tile, alpha=alpha, **lat)
        x = _ffn_call(x, mod, fw, li=li, tile=lat_ffn_tile, gw=GRID_W, alpha=alpha, **lat)
        if with_ctx_out:
            ctx = _ffn_call(ctx_mixed, mod, fw, li=li, tile=ctx_len, gw=ctx_len, alpha=alpha, **cx)
    return x
```

```python
import functools

import jax
import jax.numpy as jnp
from jax import lax
from jax.experimental import pallas as pl
from jax.experimental.pallas import tpu as pltpu

EPS = 1e-6
LANES = 128
SUBLANES = 8
GRID_W = 64
FFN_CHUNK = 256
FFN_GRANULE = 512
BF16_ROWS = 16
MIX_TILE = 512
FFN_TILE = 1024
VMEM_LIMIT = 60 * 1024 * 1024

_F32 = jnp.float32
_BF16 = jnp.bfloat16


def _norm_rows(x):
    mu = jnp.mean(x, axis=-1, keepdims=True)
    xc = x - mu
    var = jnp.mean(xc * xc, axis=-1, keepdims=True)
    return xc * lax.rsqrt(var + EPS)


def _sigmoid(x):
    return 1.0 / (1.0 + jnp.exp(-x))


def _pick_spec(arr, *lead):
    rest = arr.shape[len(lead):]
    index = tuple(lead) + (0,) * len(rest)
    return pl.BlockSpec((None,) * len(lead) + rest, lambda *_: index, pipeline_mode=pl.Buffered(1))


def _mod_spec(li, ctx_row):
    if ctx_row is None:
        return lambda mod: pl.BlockSpec((None, 1) + mod.shape[2:], lambda bi, i: (li, bi, 0, 0))
    return lambda mod: pl.BlockSpec((None, 1) + mod.shape[2:], lambda bi, i: (li, ctx_row, 0, 0))


def _ffn_kernel(*refs, tile, gw, n_tiles, n_gran, has_halo, alpha):
    if has_halo:
        (xm_ref, xp_ref, xn_ref, mod_ref, wup_ref, wcv_ref, wdn_ref, lnw_ref, lnb_ref,
         o_ref, h_scr, u0_scr, u1_scr, s_scr, acc_scr) = refs
    else:
        (xm_ref, mod_ref, wup_ref, wcv_ref, wdn_ref, lnw_ref, lnb_ref,
         o_ref, h_scr, u0_scr, u1_scr, s_scr, acc_scr) = refs
    u_bufs = (u0_scr, u1_scr)
    i = pl.program_id(1)
    n_chunks = wdn_ref.shape[0]
    pad = gw if has_halo else 0
    n_rows = tile // gw

    shift = mod_ref[0, 3:4, :]
    scale1 = 1.0 + mod_ref[0, 4:5, :]
    gate = mod_ref[0, 5:6, :]

    def modulated(xv):
        return _norm_rows(xv) * scale1 + shift

    ext = h_scr.shape[0]
    up_rows = ext // n_gran
    rows_per_gran = n_rows // n_gran
    gran_rows = rows_per_gran * gw

    def modulate_granule(g):
        rows = slice(g * gran_rows, (g + 1) * gran_rows)
        h_scr[pad + g * gran_rows:pad + (g + 1) * gran_rows, :] = modulated(xm_ref[0, rows, :]).astype(_BF16)
        if has_halo and g == 0:
            hp = jnp.where(i > 0, modulated(xp_ref[0]), 0.0)
            h_scr[0:gw, :] = hp.astype(_BF16)
        if has_halo and g == n_gran - 1:
            hn = jnp.where(i < n_tiles - 1, modulated(xn_ref[0]), 0.0)
            h_scr[pad + tile:pad + tile + gw, :] = hn.astype(_BF16)

    hidden = n_chunks * FFN_CHUNK

    def chunk_cols(c, half):
        return pl.ds(pl.multiple_of(half * hidden + c * FFN_CHUNK, FFN_CHUNK), FFN_CHUNK)

    def up_proj(c, buf, g):
        rows = pl.ds(pl.multiple_of(g * up_rows, BF16_ROWS), up_rows)
        for half in range(2):
            u_bufs[buf][rows, half * FFN_CHUNK:(half + 1) * FFN_CHUNK] = jnp.dot(
                h_scr[rows, :], wup_ref[:, chunk_cols(c, half)], preferred_element_type=_F32)

    edge = jnp.zeros((1, LANES), _F32)

    def shift_tokens(v, down):
        if down:
            return jnp.concatenate([edge, v[:-1]], axis=0)
        return jnp.concatenate([v[1:], edge], axis=0)

    def conv_cols(ub, wc, base, half, lo):
        ulo = half * FFN_CHUNK + lo

        def tap(k, off):
            return wc[k:k + 1, lo:lo + LANES] * ub[pl.ds(base + off, gw), ulo:ulo + LANES]

        def col_taps(dc):
            k = dc + 1
            v = tap(3 + k, pad)
            if has_halo:
                v = v + tap(k, 0) + tap(6 + k, 2 * gw)
            return v
        return col_taps(0) + shift_tokens(col_taps(-1), True) + shift_tokens(col_taps(1), False)

    def conv_gate(c, buf, g):
        ub = u_bufs[buf]
        wc_a = wcv_ref[:, chunk_cols(c, 0)]
        wc_g = wcv_ref[:, chunk_cols(c, 1)]
        for r in range(rows_per_gran):
            base = pl.multiple_of(g * gran_rows + r * gw, gw)
            for j in range(FFN_CHUNK // LANES):
                a = conv_cols(ub, wc_a, base, 0, j * LANES)
                gt = conv_cols(ub, wc_g, base, 1, j * LANES)
                s_scr[pl.ds(base, gw), j * LANES:(j + 1) * LANES] = (a * gt * _sigmoid(gt)).astype(_BF16)

    def down_proj(c, g):
        rows = pl.ds(pl.multiple_of(g * gran_rows, gran_rows), gran_rows)
        acc_scr[rows, :] += jnp.dot(s_scr[rows, :], wdn_ref[c], preferred_element_type=_F32)

    def chunk_steps(c, buf, has_next):
        def step(g, carry):
            if has_next:
                up_proj(c + 1, 1 - buf, g)
            wrap = jnp.where(g == 0, 1, 0)
            down_proj(jnp.maximum(c - wrap, 0), g - 1 + wrap * n_gran)
            conv_gate(c, buf, g)
            return carry
        lax.fori_loop(0, n_gran, step, 0)

    acc_scr[...] = jnp.zeros_like(acc_scr)
    s_scr[...] = jnp.zeros_like(s_scr)
    done = 0
    for g in range(n_gran):
        need = min(n_gran, -(-((g + 1) * up_rows - pad) // gran_rows))
        for gm in range(done, need):
            modulate_granule(gm)
        done = max(done, need)
        up_proj(0, 0, g)

    def body(k, carry):
        c = 2 * k
        chunk_steps(c, 0, True)
        chunk_steps(c + 1, 1, True)
        return carry

    n_pairs = (n_chunks - 1) // 2
    lax.fori_loop(0, n_pairs, body, 0)
    if n_chunks % 2 == 0:
        chunk_steps(n_chunks - 2, 0, True)
    chunk_steps(n_chunks - 1, (n_chunks - 1) % 2, False)
    down_proj(n_chunks - 1, n_gran - 1)

    for g in range(n_gran):
        rows = slice(g * gran_rows, (g + 1) * gran_rows)
        y = alpha * xm_ref[0, rows, :] + gate * acc_scr[rows, :]
        o_ref[0, rows, :] = _norm_rows(y) * lnw_ref[...] + lnb_ref[...]


def _ffn_call(x, mod, fw, *, li, ctx_row, tile, gw, alpha):
    wup, wcv, wdn, post_w, post_b = fw
    b, l, d = x.shape
    n_tiles = l // tile
    has_halo = l > gw
    rows_per_tile = tile // gw
    n_grid_rows = l // gw
    ext = tile + (2 * gw if has_halo else 0)

    x_spec = pl.BlockSpec((1, tile, d), lambda bi, i: (bi, i, 0))
    in_specs = [x_spec]
    args = [x]
    if has_halo:
        in_specs.append(pl.BlockSpec(
            (1, gw, d), lambda bi, i: (bi, jnp.maximum(i * rows_per_tile - 1, 0), 0)))
        in_specs.append(pl.BlockSpec(
            (1, gw, d), lambda bi, i: (bi, jnp.minimum((i + 1) * rows_per_tile, n_grid_rows - 1), 0)))
        args += [x, x]
    in_specs += [
        _mod_spec(li, ctx_row)(mod),
        _pick_spec(wup, li), _pick_spec(wcv, li), _pick_spec(wdn, li),
        _pick_spec(post_w, li, 1), _pick_spec(post_b, li, 1),
    ]
    args += [mod, wup, wcv, wdn, post_w, post_b]
    n_gran = max(1, tile // FFN_GRANULE)
    assert ext % (n_gran * BF16_ROWS) == 0 and rows_per_tile % n_gran == 0
    kern = functools.partial(_ffn_kernel, tile=tile, gw=gw, n_tiles=n_tiles, n_gran=n_gran,
                             has_halo=has_halo, alpha=alpha)
    return pl.pallas_call(
        kern,
        grid=(b, n_tiles),
        in_specs=in_specs,
        out_specs=x_spec,
        out_shape=jax.ShapeDtypeStruct(x.shape, _F32),
        scratch_shapes=[
            pltpu.VMEM((ext, d), _BF16),
            pltpu.VMEM((ext, 2 * FFN_CHUNK), _F32),
            pltpu.VMEM((ext, 2 * FFN_CHUNK), _F32),
            pltpu.VMEM((tile, FFN_CHUNK), _BF16),
            pltpu.VMEM((tile, d), _F32),
        ],
        compiler_params=pltpu.CompilerParams(
            dimension_semantics=("arbitrary", "arbitrary"),
            vmem_limit_bytes=VMEM_LIMIT),
        name="channel_mixer",
    )(*args)


def _prep_ffn_weights(w_up, conv_w, w_down):
    depth, d, two_h = w_up.shape
    n_chunks = two_h // 2 // FFN_CHUNK
    wup = w_up.astype(_BF16)
    wcv = conv_w.reshape(depth, 9, two_h)
    wdn = w_down.reshape(depth, n_chunks, FFN_CHUNK, w_down.shape[-1]).astype(_BF16)
    return wup, wcv, wdn


GLA_HEADS = 4
GLA_DK = 32
GLA_DV = 64
GLA_QK = GLA_HEADS * GLA_DK
GLA_V = GLA_HEADS * GLA_DV
GLA_RANK = 16
GLA_CHUNK = 64
GLA_GATE_NORMALIZER = 16.0
GLA_LOG_DECAY_MIN = -1.0


def _log_decay(logit):
    log_sig = jnp.minimum(logit, 0.0) - jnp.log(1.0 + jnp.exp(-jnp.abs(logit)))
    return jnp.maximum(log_sig / GLA_GATE_NORMALIZER, GLA_LOG_DECAY_MIN)


def _chunk_cumsum(x, reverse):
    n = x.shape[0]
    pos = lax.broadcasted_iota(jnp.int32, x.shape, 0) % GLA_CHUNK
    s = 1
    while s < GLA_CHUNK:
        if reverse:
            x = x + jnp.where(pos < GLA_CHUNK - s, pltpu.roll(x, n - s, axis=0), 0.0)
        else:
            x = x + jnp.where(pos >= s, pltpu.roll(x, s, axis=0), 0.0)
        s *= 2
    return x


def _state_mask():
    r = lax.broadcasted_iota(jnp.int32, (GLA_V, GLA_QK), 0) // GLA_DV
    c = lax.broadcasted_iota(jnp.int32, (GLA_V, GLA_QK), 1) // GLA_DK
    return r == c


def _stack4(a):
    return jnp.concatenate([a, a, a, a], axis=0)


def _compact_state(s):
    return (s[0:GLA_DV] + s[GLA_DV:2 * GLA_DV]) + (s[2 * GLA_DV:3 * GLA_DV] + s[3 * GLA_DV:4 * GLA_DV])


def _state_update(state, v_bf, k_dec_bf, decay_row, mask):
    kv_t = lax.dot_general(v_bf, k_dec_bf, (((0,), (0,)), ((), ())), preferred_element_type=_F32)
    return state * decay_row + jnp.where(mask, kv_t, 0.0)


def _state_kernel(x_ref, mod_ref, w_ref, wg_ref, bg_ref, s0_ref, start_ref, fin_ref, st_scr,
                  *, tile, reverse):
    i = pl.program_id(1)
    n_tiles = pl.num_programs(1)
    chunks = tile // GLA_CHUNK

    @pl.when(i == 0)
    def _():
        st_scr[...] = s0_ref[0]

    shift = mod_ref[0, 0:1, :]
    scale1 = 1.0 + mod_ref[0, 1:2, :]
    h = (_norm_rows(x_ref[0]) * scale1 + shift).astype(_BF16)
    z = jnp.dot(h, w_ref[...], preferred_element_type=_F32)
    k = z[:, 0:GLA_QK]
    v_bf = z[:, GLA_QK:GLA_QK + GLA_V].astype(_BF16)
    low = z[:, GLA_QK + GLA_V:].astype(_BF16)
    g = _log_decay(jnp.dot(low, wg_ref[...], preferred_element_type=_F32) + bg_ref[...])
    run = _chunk_cumsum(g, reverse)
    mask = _state_mask()
    state = st_scr[...]
    order = range(chunks - 1, -1, -1) if reverse else range(chunks)
    for ci in order:
        rows = slice(ci * GLA_CHUNK, (ci + 1) * GLA_CHUNK)
        edge = ci * GLA_CHUNK if reverse else (ci + 1) * GLA_CHUNK - 1
        total = run[edge:edge + 1, :]
        k_dec = (k[rows] * jnp.exp(total - run[rows])).astype(_BF16)
        start_ref[0, ci] = _compact_state(state)
        state = _state_update(state, v_bf[rows], k_dec, jnp.exp(total), mask)
    st_scr[...] = state

    @pl.when(i == n_tiles - 1)
    def _():
        fin_ref[0] = state


def _state_call(x, mod, mw, s0, *, li, ctx_row, tile, reverse):
    di = 1 if reverse else 0
    w, wg, bg = mw["w_state"], mw["wg1"], mw["bg1"]
    b, l, d = x.shape
    n_tiles = l // tile
    chunks = tile // GLA_CHUNK
    tmap = (lambda bi, i: (bi, n_tiles - 1 - i, 0)) if reverse else (lambda bi, i: (bi, i, 0))
    smap = (lambda bi, i: (bi, n_tiles - 1 - i, 0, 0)) if reverse else (lambda bi, i: (bi, i, 0, 0))
    st_spec = pl.BlockSpec((1, GLA_V, GLA_QK), lambda bi, i: (bi, 0, 0))
    kern = functools.partial(_state_kernel, tile=tile, reverse=reverse)
    return pl.pallas_call(
        kern,
        grid=(b, n_tiles),
        in_specs=[
            pl.BlockSpec((1, tile, d), tmap),
            _mod_spec(li, ctx_row)(mod),
            _pick_spec(w, li, di), _pick_spec(wg, li, di), _pick_spec(bg, li, di),
            st_spec,
        ],
        out_specs=[
            pl.BlockSpec((1, chunks, GLA_DV, GLA_QK), smap),
            st_spec,
        ],
        out_shape=[
            jax.ShapeDtypeStruct((b, l // GLA_CHUNK, GLA_DV, GLA_QK), _F32),
            jax.ShapeDtypeStruct((b, GLA_V, GLA_QK), _F32),
        ],
        scratch_shapes=[pltpu.VMEM((GLA_V, GLA_QK), _F32)],
        compiler_params=pltpu.CompilerParams(
            dimension_semantics=("arbitrary", "arbitrary"),
            vmem_limit_bytes=VMEM_LIMIT),
        name="gla_state_rev" if reverse else "gla_state_fwd",
    )(x, mod, w, wg, bg, s0)


GROUP_W = 256
COL_Q, COL_K, COL_V, COL_R, COL_LOW, COL_C, COL_B, COL_D = 0, 128, 256, 512, 768, 896, 1408, 1664
IN_COLS = 2176
HALO = 16
POOL_REACH = 8
MIX_BLOCK = 128
CONV_WIDTH = 31
SGU_HEADS = 4
SGU_HD = 64


def _gelu_tanh(x):
    return 0.5 * x * (1.0 + jnp.tanh(0.7978845608028654 * (x + 0.044715 * (x * x * x))))


def _mixer_kernel(xm_ref, xp_ref, xn_ref, mod_ref, win_ref, wg_ref, bg_ref, gnw_ref,
                  pblk_ref, pscale_ref, sguw_ref, sgub_ref, slnw_ref, slnb_ref,
                  cw_ref, cb_ref, clnw_ref, clnb_ref, wout_ref, plw_ref, plb_ref,
                  sb_ref, sf0_ref,
                  o_ref, sfin_ref,
                  z_scr, zb_scr, y_scr, run_scr, o_scr, cat_scr, st_scr,
                  *, tile, seq_len, alpha):
    i = pl.program_id(1)
    n_tiles = pl.num_programs(1)
    n_blocks = tile // MIX_BLOCK

    @pl.when(i == 0)
    def _():
        st_scr[...] = sf0_ref[0]

    shift = mod_ref[0, 0:1, :]
    scale1 = 1.0 + mod_ref[0, 1:2, :]
    gate = mod_ref[0, 2:3, :]

    def modulated(xv):
        return _norm_rows(xv) * scale1 + shift

    hp = jnp.where(i > 0, modulated(xp_ref[0]), 0.0).astype(_BF16)
    hn = jnp.where(i < n_tiles - 1, modulated(xn_ref[0]), 0.0).astype(_BF16)
    zh = jnp.dot(jnp.concatenate([hp, hn], axis=0), win_ref[:, COL_B:],
                 preferred_element_type=_F32)

    def glu(zd):
        return zd[:, :GROUP_W] * _sigmoid(zd[:, GROUP_W:])

    zb_scr[0:HALO, :] = zh[0:HALO, 0:GROUP_W]
    zb_scr[HALO + tile:, :] = zh[HALO:, 0:GROUP_W]
    y_scr[0:HALO, :] = glu(zh[0:HALO, GROUP_W:])
    y_scr[HALO + tile:, :] = glu(zh[HALO:, GROUP_W:])

    def block_rows(blk):
        return slice(blk * MIX_BLOCK, (blk + 1) * MIX_BLOCK)

    def in_proj(blk):
        rows = block_rows(blk)
        ext_rows = slice(HALO + blk * MIX_BLOCK, HALO + (blk + 1) * MIX_BLOCK)
        h = modulated(xm_ref[0, rows, :]).astype(_BF16)
        z = jnp.dot(h, win_ref[...], preferred_element_type=_F32)
        z_scr[rows, :] = z
        zb_scr[ext_rows, :] = z[:, COL_B:COL_B + GROUP_W]
        y_scr[ext_rows, :] = glu(z[:, COL_D:COL_D + 2 * GROUP_W])
        low = z[:, COL_LOW:COL_LOW + LANES].astype(_BF16)
        g = _log_decay(jnp.dot(low, wg_ref[...], preferred_element_type=_F32) + bg_ref[...])
        run_scr[rows, 0:GLA_QK] = _chunk_cumsum(g[:, 0:GLA_QK], False)
        run_scr[rows, GLA_QK:] = _chunk_cumsum(g[:, GLA_QK:], True)

    lane = lax.broadcasted_iota(jnp.int32, (MIX_BLOCK, GROUP_W), 1)
    row = lax.broadcasted_iota(jnp.int32, (MIX_BLOCK, GROUP_W), 0)
    lane_group = lane // SGU_HD
    pool_half = jnp.left_shift(1, lane_group)

    def local_mixers(blk):
        r0 = blk * MIX_BLOCK
        rows = block_rows(blk)

        n = MIX_BLOCK + 2 * POOL_REACH
        xb = zb_scr[r0 + HALO - POOL_REACH:r0 + HALO - POOL_REACH + n, :]
        p2 = xb + pltpu.roll(xb, 1, axis=0)
        p4 = pltpu.roll(p2, 1, axis=0) + pltpu.roll(p2, n - 1, axis=0)
        p8 = pltpu.roll(p4, 2, axis=0) + pltpu.roll(p4, n - 2, axis=0)
        p16 = pltpu.roll(p8, 4, axis=0) + pltpu.roll(p8, n - 4, axis=0)
        inner = slice(POOL_REACH, POOL_REACH + MIX_BLOCK)
        win = jnp.where(lane_group == 0, p2[inner],
                        jnp.where(lane_group == 1, p4[inner],
                                  jnp.where(lane_group == 2, p8[inner], p16[inner])))
        tok = i * tile + r0 + row
        count = jnp.minimum(tok + pool_half, seq_len) - jnp.maximum(tok - pool_half, 0)
        pooled = win / count.astype(_F32) - xb[inner]
        yb = jnp.dot(pooled.astype(_BF16), pblk_ref[...], preferred_element_type=_F32) * pscale_ref[...]
        cat_scr[rows, GROUP_W:2 * GROUP_W] = yb.astype(_BF16)

        gz = _gelu_tanh(z_scr[rows, COL_C:COL_C + 2 * GROUP_W])
        vn = _norm_rows(gz[:, GROUP_W:]) * slnw_ref[...] + slnb_ref[...]
        vstack = jnp.concatenate(
            [jnp.where(lane_group == hd, vn, 0.0) for hd in range(SGU_HEADS)], axis=0).astype(_BF16)
        sg = jnp.dot(sguw_ref[...], vstack, preferred_element_type=_F32) + sgub_ref[...]
        cat_scr[rows, 2 * GROUP_W:3 * GROUP_W] = (gz[:, :GROUP_W] * sg).astype(_BF16)

        span = MIX_BLOCK + SUBLANES
        halves = []
        for lo in range(0, GROUP_W, LANES):
            conv = None
            for b in range(SUBLANES):
                part = None
                for a in range(4):
                    o = SUBLANES * a + b
                    if 1 <= o <= CONV_WIDTH:
                        term = (cw_ref[o - 1:o, lo:lo + LANES]
                                * y_scr[r0 + SUBLANES * a:r0 + SUBLANES * a + span, lo:lo + LANES])
                        part = term if part is None else part + term
                part = part[b:b + MIX_BLOCK]
                conv = part if conv is None else conv + part
            halves.append(conv)
        conv = jnp.concatenate(halves, axis=1)
        yd = _norm_rows(conv + cb_ref[...]) * clnw_ref[...] + clnb_ref[...]
        cat_scr[rows, 3 * GROUP_W:4 * GROUP_W] = (yd * _sigmoid(yd)).astype(_BF16)

    mask_s = _state_mask()
    vr = lax.broadcasted_iota(jnp.int32, (GLA_V, GLA_V), 0) // GLA_DV
    vc = lax.broadcasted_iota(jnp.int32, (GLA_V, GLA_V), 1) // GLA_DV
    mask_v = vr == vc
    t_id = lax.broadcasted_iota(jnp.int32, (GLA_CHUNK, GLA_V), 0)
    s_id = lax.broadcasted_iota(jnp.int32, (GLA_CHUNK, GLA_V), 1) % GLA_CHUNK
    nt_dims = (((1,), (1,)), ((), ()))

    zero = jnp.zeros((), _BF16)

    def chunk_rows(ci):
        return slice(ci * GLA_CHUNK, (ci + 1) * GLA_CHUNK)

    def scores(ci):
        r = chunk_rows(ci)
        q = z_scr[r, COL_Q:COL_Q + GLA_QK] * (GLA_DK ** -0.5)
        k = z_scr[r, COL_K:COL_K + GLA_QK]
        cf = run_scr[r, 0:GLA_QK]
        rb = run_scr[r, GLA_QK:]
        qf = (q * jnp.exp(cf)).astype(_BF16)
        kf = (k * jnp.exp(-cf)).astype(_BF16)
        qb = (q * jnp.exp(rb)).astype(_BF16)
        kb = (k * jnp.exp(-rb)).astype(_BF16)
        a_f = lax.dot_general(qf, jnp.where(mask_s, _stack4(kf), zero), nt_dims,
                              preferred_element_type=_F32)
        a_b = lax.dot_general(qb, jnp.where(mask_s, _stack4(kb), zero), nt_dims,
                              preferred_element_type=_F32)
        a = (jnp.where(s_id <= t_id, a_f, 0.0) + jnp.where(s_id >= t_id, a_b, 0.0)).astype(_BF16)
        return a, qf, qb

    def outputs(ci, a, qf, qb):
        r = chunk_rows(ci)
        k = z_scr[r, COL_K:COL_K + GLA_QK]
        v = z_scr[r, COL_V:COL_V + GLA_V].astype(_BF16)
        cf = run_scr[r, 0:GLA_QK]
        total = cf[GLA_CHUNK - 1:GLA_CHUNK, :]
        k_dec = (k * jnp.exp(total - cf)).astype(_BF16)
        o = jnp.dot(a, jnp.where(mask_v, _stack4(v), zero), preferred_element_type=_F32)
        sf = st_scr[...]
        sb = jnp.where(mask_s, _stack4(sb_ref[0, ci]), 0.0)
        o = o + lax.dot_general(qf, sf.astype(_BF16), nt_dims, preferred_element_type=_F32)
        o = o + lax.dot_general(qb, sb.astype(_BF16), nt_dims, preferred_element_type=_F32)
        o_scr[r, :] = o
        st_scr[...] = _state_update(sf, v, k_dec, jnp.exp(total), mask_s)

    head_mean = jnp.where(mask_v, 1.0 / GLA_DV, 0.0).astype(_BF16)

    def readout(blk):
        rows = block_rows(blk)
        o = o_scr[rows, :]
        sq = o * o
        sq_hi = sq.astype(_BF16)
        sq_lo = (sq - sq_hi.astype(_F32)).astype(_BF16)
        ms = (jnp.dot(sq_hi, head_mean, preferred_element_type=_F32)
              + jnp.dot(sq_lo, head_mean, preferred_element_type=_F32))
        rg = z_scr[rows, COL_R:COL_R + GROUP_W]
        ya = o * lax.rsqrt(ms + EPS) * gnw_ref[...] * (rg * _sigmoid(rg))
        cat_scr[rows, 0:GROUP_W] = ya.astype(_BF16)

    def out_proj(blk):
        rows = block_rows(blk)
        y = jnp.dot(cat_scr[rows, :], wout_ref[...], preferred_element_type=_F32)
        o_ref[0, rows, :] = (_norm_rows(alpha * xm_ref[0, rows, :] + gate * y) * plw_ref[...]
                             + plb_ref[...])

    chunks_per_block = MIX_BLOCK // GLA_CHUNK
    n_gla_chunks = tile // GLA_CHUNK
    in_proj(0)
    if n_blocks > 1:
        in_proj(1)
    cur = scores(0)
    for blk in range(n_blocks):
        if blk + 2 < n_blocks:
            in_proj(blk + 2)
        local_mixers(blk)
        for ci in range(blk * chunks_per_block, (blk + 1) * chunks_per_block):
            nxt = scores(ci + 1) if ci + 1 < n_gla_chunks else None
            outputs(ci, *cur)
            cur = nxt
        readout(blk)
        out_proj(blk)

    @pl.when(i == n_tiles - 1)
    def _():
        sfin_ref[0] = st_scr[...]


def _mixer_call(x, mod, mw, sb_start, sf0, *, li, ctx_row, tile, alpha):
    b, l, d = x.shape
    n_tiles = l // tile
    halo_per_tile = tile // HALO
    n_halo_blocks = l // HALO
    consts = [mw[k] for k in ("w_in", "wg2", "bg2", "gnw", "pool_blk", "pool_scale", "sgu_w", "sgu_b",
                              "sgu_ln_w", "sgu_ln_b", "cm_w", "cm_b", "cm_ln_w", "cm_ln_b", "w_out")]
    const_specs = [_pick_spec(c, li) for c in consts]
    consts += [mw["post_w"], mw["post_b"]]
    const_specs += [_pick_spec(mw["post_w"], li, 0), _pick_spec(mw["post_b"], li, 0)]
    x_spec = pl.BlockSpec((1, tile, d), lambda bi, i: (bi, i, 0))
    st_spec = pl.BlockSpec((1, GLA_V, GLA_QK), lambda bi, i: (bi, 0, 0))
    in_specs = [
        x_spec,
        pl.BlockSpec((1, HALO, d), lambda bi, i: (bi, jnp.maximum(i * halo_per_tile - 1, 0), 0)),
        pl.BlockSpec((1, HALO, d), lambda bi, i: (bi, jnp.minimum((i + 1) * halo_per_tile, n_halo_blocks - 1), 0)),
        _mod_spec(li, ctx_row)(mod),
    ] + const_specs + [
        pl.BlockSpec((1, tile // GLA_CHUNK, GLA_DV, GLA_QK), lambda bi, i: (bi, i, 0, 0)),
        st_spec,
    ]
    kern = functools.partial(_mixer_kernel, tile=tile, seq_len=l, alpha=alpha)
    return pl.pallas_call(
        kern,
        grid=(b, n_tiles),
        in_specs=in_specs,
        out_specs=[x_spec, st_spec],
        out_shape=[jax.ShapeDtypeStruct(x.shape, _F32),
                   jax.ShapeDtypeStruct((b, GLA_V, GLA_QK), _F32)],
        scratch_shapes=[
            pltpu.VMEM((tile, IN_COLS), _F32),
            pltpu.VMEM((tile + 2 * HALO, GROUP_W), _F32),
            pltpu.VMEM((tile + 2 * HALO, GROUP_W), _F32),
            pltpu.VMEM((tile, 2 * GLA_QK), _F32),
            pltpu.VMEM((tile, GLA_V), _F32),
            pltpu.VMEM((tile, 4 * GROUP_W), _BF16),
            pltpu.VMEM((GLA_V, GLA_QK), _F32),
        ],
        compiler_params=pltpu.CompilerParams(
            dimension_semantics=("arbitrary", "arbitrary"),
            vmem_limit_bytes=VMEM_LIMIT),
        name="token_mixer",
    )(x, x, x, mod, *consts, sb_start, sf0)


def _prep_mixer_weights(w_in, gla_w_gate, gla_b_gate, gla_norm_w, pool_w, pool_scale, sgu_w, sgu_b,
                        sgu_ln_w, sgu_ln_b, cm_conv_w, cm_conv_b, cm_ln_w, cm_ln_b, w_out,
                        post_w, post_b):
    depth, d, _ = w_in.shape
    n_pool = pool_w.shape[1]
    qkvr = w_in[:, :, 0:768]
    low = w_in[:, :, 768:800]
    zb = w_in[:, :, 800:1056]
    zc = w_in[:, :, 1056:1568]
    zd = w_in[:, :, 1568:2080]
    pad = jnp.zeros((depth, d, LANES - 2 * GLA_RANK), _F32)
    w_main = jnp.concatenate([qkvr, low, pad, zc, zb, zd], axis=2).astype(_BF16)
    kv = w_in[:, :, 128:512]
    pad1 = jnp.zeros((depth, d, LANES - GLA_RANK), _F32)
    w_state = jnp.stack(
        [jnp.concatenate([kv, low[:, :, di * GLA_RANK:(di + 1) * GLA_RANK], pad1], axis=2) for di in range(2)],
        axis=1).astype(_BF16)
    wg1 = jnp.pad(gla_w_gate, ((0, 0), (0, 0), (0, LANES - GLA_RANK), (0, 0))).astype(_BF16)
    zero = jnp.zeros((depth, GLA_RANK, GLA_QK), _F32)
    wg2 = jnp.concatenate([
        jnp.concatenate([gla_w_gate[:, 0], zero], axis=2),
        jnp.concatenate([zero, gla_w_gate[:, 1]], axis=2),
        jnp.zeros((depth, LANES - 2 * GLA_RANK, 2 * GLA_QK), _F32)], axis=1).astype(_BF16)
    eye = jnp.eye(n_pool, dtype=_F32)
    pool_blk = (pool_w[:, :, :, None, :] * eye[None, :, None, :, None]).reshape(depth, GROUP_W, GROUP_W)
    row = lambda a: a[:, None, :]
    return {
        "w_in": w_main, "w_state": w_state, "wg1": wg1, "bg1": gla_b_gate[:, :, None, :],
        "wg2": wg2, "bg2": gla_b_gate.reshape(depth, 1, 2 * GLA_QK),
        "gnw": row(gla_norm_w), "pool_blk": pool_blk.astype(_BF16), "pool_scale": row(pool_scale),
        "sgu_w": sgu_w.transpose(0, 2, 1, 3).reshape(depth, MIX_BLOCK, SGU_HEADS * MIX_BLOCK).astype(_BF16),
        "sgu_b": jnp.repeat(sgu_b.transpose(0, 2, 1), SGU_HD, axis=2),
        "sgu_ln_w": row(sgu_ln_w), "sgu_ln_b": row(sgu_ln_b),
        "cm_w": jnp.pad(cm_conv_w, ((0, 0), (0, 1), (0, 0))),
        "cm_b": row(cm_conv_b), "cm_ln_w": row(cm_ln_w), "cm_ln_b": row(cm_ln_b),
        "w_out": w_out.astype(_BF16),
        "post_w": post_w[:, :, None, :], "post_b": post_b[:, :, None, :],
    }


def _mod_kernel(c_ref, w_ref, b_ref, o_ref):
    cv = c_ref[...]

    def split(a):
        hi = a.astype(_BF16)
        return hi, (a - hi.astype(_F32)).astype(_BF16)

    a_hi, a_lo = split(cv * _sigmoid(cv))
    w_hi, w_lo = split(w_ref[0])
    dot = functools.partial(jnp.dot, preferred_element_type=_F32)
    o_ref[0] = (dot(a_hi, w_lo) + dot(a_lo, w_hi)) + dot(a_hi, w_hi) + b_ref[0]


def _mod_call(cond, w_mod, b_mod):
    depth, d, six_d = w_mod.shape
    rows = cond.shape[0]
    return pl.pallas_call(
        _mod_kernel,
        grid=(depth, six_d // d),
        in_specs=[
            pl.BlockSpec((rows, d), lambda li, j: (0, 0)),
            pl.BlockSpec((1, d, d), lambda li, j: (li, 0, j)),
            pl.BlockSpec((1, 1, d), lambda li, j: (li, 0, j)),
        ],
        out_specs=pl.BlockSpec((1, rows, d), lambda li, j: (li, 0, j)),
        out_shape=jax.ShapeDtypeStruct((depth, rows, six_d), _F32),
        compiler_params=pltpu.CompilerParams(
            dimension_semantics=("arbitrary", "arbitrary"),
            vmem_limit_bytes=VMEM_LIMIT),
        name="adaln_modulation",
    )(cond, w_mod, b_mod.reshape(depth, 1, six_d))


def _pick_tile(length, target):
    return min(length, target)


def kernel(x, c, ctx, c_ctx, w_mod, b_mod, w_in, gla_w_gate, gla_b_gate, gla_norm_w, pool_w, pool_scale, sgu_w, sgu_b, sgu_ln_w, sgu_ln_b, cm_conv_w, cm_conv_b, cm_ln_w, cm_ln_b, w_out, ffn_w_up, ffn_conv_w, ffn_w_down, post_ln_w, post_ln_b):
    batch, seq, d = x.shape
    ctx_len = ctx.shape[1]
    depth = w_mod.shape[0]
    alpha = (2 * depth) ** 0.25

    cond_rows = -(-(batch + 1) // SUBLANES) * SUBLANES
    cond = jnp.concatenate([c, c_ctx[None, :], jnp.zeros((cond_rows - batch - 1, d), _F32)], axis=0)
    mod = _mod_call(cond, w_mod, b_mod).reshape(depth, cond_rows, 6, d)

    lat_mix_tile = _pick_tile(seq, MIX_TILE)
    ctx_mix_tile = _pick_tile(ctx_len, MIX_TILE)
    lat_ffn_tile = _pick_tile(seq, FFN_TILE)
    zero_state = jnp.zeros((batch, GLA_V, GLA_QK), _F32)

    mw = _prep_mixer_weights(w_in, gla_w_gate, gla_b_gate, gla_norm_w, pool_w, pool_scale, sgu_w, sgu_b,
                             sgu_ln_w, sgu_ln_b, cm_conv_w, cm_conv_b, cm_ln_w, cm_ln_b, w_out,
                             post_ln_w, post_ln_b)
    fw = _prep_ffn_weights(ffn_w_up, ffn_conv_w, ffn_w_down) + (mw["post_w"], mw["post_b"])
    lat = dict(ctx_row=None)
    cx = dict(ctx_row=batch)

    for li in range(depth):
        with_ctx_out = li < depth - 1
        ctx_sb, ctx_sb_fin = _state_call(ctx, mod, mw, zero_state, li=li, tile=ctx_mix_tile,
                                         reverse=True, **cx)
        if with_ctx_out:
            ctx_mixed, ctx_sf_fin = _mixer_call(ctx, mod, mw, ctx_sb, zero_state, li=li,
                                                tile=ctx_mix_tile, alpha=alpha, **cx)
        else:
            _, ctx_sf_fin = _state_call(ctx, mod, mw, zero_state, li=li, tile=ctx_mix_tile,
                                        reverse=False, **cx)

        lat_sb, _ = _state_call(x, mod, mw, ctx_sb_fin, li=li, tile=lat_mix_tile, reverse=True, **lat)
        x, _ = _mixer_call(x, mod, mw, lat_sb, ctx_sf_fin, li=li, tile=lat_mix_tile, alpha=alpha, **lat)
        x = _ffn_call(x, mod, fw, li=li, tile=lat_ffn_tile, gw=GRID_W, alpha=alpha, **lat)
        if with_ctx_out:
            ctx = _ffn_call(ctx_mixed, mod, fw, li=li, tile=ctx_len, gw=ctx_len, alpha=alpha, **cx)
    return x
```

```python
import functools

import jax
import jax.numpy as jnp
from jax import lax
from jax.experimental import pallas as pl
from jax.experimental.pallas import tpu as pltpu

EPS = 1e-6
LANES = 128
SUBLANES = 8
GRID_W = 64
FFN_CHUNK = 256
FFN_GRANULE = 512
BF16_ROWS = 16
MIX_TILE = 512
FFN_TILE = 1024
VMEM_LIMIT = 60 * 1024 * 1024

_F32 = jnp.float32
_BF16 = jnp.bfloat16


def _norm_rows(x):
    mu = jnp.mean(x, axis=-1, keepdims=True)
    xc = x - mu
    var = jnp.mean(xc * xc, axis=-1, keepdims=True)
    return xc * lax.rsqrt(var + EPS)


def _sigmoid(x):
    return 1.0 / (1.0 + jnp.exp(-x))


def _pick_spec(arr, *lead):
    rest = arr.shape[len(lead):]
    index = tuple(lead) + (0,) * len(rest)
    return pl.BlockSpec((None,) * len(lead) + rest, lambda *_: index, pipeline_mode=pl.Buffered(1))


def _mod_spec(li, ctx_row):
    if ctx_row is None:
        return lambda mod: pl.BlockSpec((None, 1) + mod.shape[2:], lambda bi, i: (li, bi, 0, 0))
    return lambda mod: pl.BlockSpec((None, 1) + mod.shape[2:], lambda bi, i: (li, ctx_row, 0, 0))


def _ffn_kernel(*refs, tile, gw, n_tiles, n_gran, has_halo, alpha):
    if has_halo:
        (xm_ref, xp_ref, xn_ref, mod_ref, wup_ref, wcv_ref, wdn_ref, lnw_ref, lnb_ref,
         o_ref, h_scr, u0_scr, u1_scr, s_scr, acc_scr) = refs
    else:
        (xm_ref, mod_ref, wup_ref, wcv_ref, wdn_ref, lnw_ref, lnb_ref,
         o_ref, h_scr, u0_scr, u1_scr, s_scr, acc_scr) = refs
    u_bufs = (u0_scr, u1_scr)
    i = pl.program_id(1)
    n_chunks = wdn_ref.shape[0]
    pad = gw if has_halo else 0
    n_rows = tile // gw

    shift = mod_ref[0, 3:4, :]
    scale1 = 1.0 + mod_ref[0, 4:5, :]
    gate = mod_ref[0, 5:6, :]

    def modulated(xv):
        return _norm_rows(xv) * scale1 + shift

    ext = h_scr.shape[0]
    up_rows = ext // n_gran
    rows_per_gran = n_rows // n_gran
    gran_rows = rows_per_gran * gw

    def modulate_granule(g):
        rows = slice(g * gran_rows, (g + 1) * gran_rows)
        h_scr[pad + g * gran_rows:pad + (g + 1) * gran_rows, :] = modulated(xm_ref[0, rows, :]).astype(_BF16)
        if has_halo and g == 0:
            hp = jnp.where(i > 0, modulated(xp_ref[0]), 0.0)
            h_scr[0:gw, :] = hp.astype(_BF16)
        if has_halo and g == n_gran - 1:
            hn = jnp.where(i < n_tiles - 1, modulated(xn_ref[0]), 0.0)
            h_scr[pad + tile:pad + tile + gw, :] = hn.astype(_BF16)

    hidden = n_chunks * FFN_CHUNK

    def chunk_cols(c, half):
        return pl.ds(pl.multiple_of(half * hidden + c * FFN_CHUNK, FFN_CHUNK), FFN_CHUNK)

    def up_proj(c, buf, g):
        rows = pl.ds(pl.multiple_of(g * up_rows, BF16_ROWS), up_rows)
        for half in range(2):
            u_bufs[buf][rows, half * FFN_CHUNK:(half + 1) * FFN_CHUNK] = jnp.dot(
                h_scr[rows, :], wup_ref[:, chunk_cols(c, half)], preferred_element_type=_F32)

    edge = jnp.zeros((1, LANES), _F32)

    def shift_tokens(v, down):
        if down:
            return jnp.concatenate([edge, v[:-1]], axis=0)
        return jnp.concatenate([v[1:], edge], axis=0)

    def conv_cols(ub, wc, base, half, lo):
        ulo = half * FFN_CHUNK + lo

        def tap(k, off):
            return wc[k:k + 1, lo:lo + LANES] * ub[pl.ds(base + off, gw), ulo:ulo + LANES]

        def col_taps(dc):
            k = dc + 1
            v = tap(3 + k, pad)
            if has_halo:
                v = v + tap(k, 0) + tap(6 + k, 2 * gw)
            return v
        return col_taps(0) + shift_tokens(col_taps(-1), True) + shift_tokens(col_taps(1), False)

    def conv_gate(c, buf, g):
        ub = u_bufs[buf]
        wc_a = wcv_ref[:, chunk_cols(c, 0)]
        wc_g = wcv_ref[:, chunk_cols(c, 1)]
        for r in range(rows_per_gran):
            base = pl.multiple_of(g * gran_rows + r * gw, gw)
            for j in range(FFN_CHUNK // LANES):
                a = conv_cols(ub, wc_a, base, 0, j * LANES)
                gt = conv_cols(ub, wc_g, base, 1, j * LANES)
                half_g = 0.5 * gt
                p = a * half_g
                s_scr[pl.ds(base, gw), j * LANES:(j + 1) * LANES] = (p + p * jnp.tanh(half_g)).astype(_BF16)

    def down_proj(c, g):
        rows = pl.ds(pl.multiple_of(g * gran_rows, gran_rows), gran_rows)
        acc_scr[rows, :] += jnp.dot(s_scr[rows, :], wdn_ref[c], preferred_element_type=_F32)

    def chunk_steps(c, buf, has_next):
        def step(g, carry):
            if has_next:
                up_proj(c + 1, 1 - buf, g)
            wrap = jnp.where(g == 0, 1, 0)
            down_proj(jnp.maximum(c - wrap, 0), g - 1 + wrap * n_gran)
            conv_gate(c, buf, g)
            return carry
        lax.fori_loop(0, n_gran, step, 0)

    acc_scr[...] = jnp.zeros_like(acc_scr)
    s_scr[...] = jnp.zeros_like(s_scr)
    done = 0
    for g in range(n_gran):
        need = min(n_gran, -(-((g + 1) * up_rows - pad) // gran_rows))
        for gm in range(done, need):
            modulate_granule(gm)
        done = max(done, need)
        up_proj(0, 0, g)

    def body(k, carry):
        c = 2 * k
        chunk_steps(c, 0, True)
        chunk_steps(c + 1, 1, True)
        return carry

    n_pairs = (n_chunks - 1) // 2
    lax.fori_loop(0, n_pairs, body, 0)
    if n_chunks % 2 == 0:
        chunk_steps(n_chunks - 2, 0, True)
    chunk_steps(n_chunks - 1, (n_chunks - 1) % 2, False)
    down_proj(n_chunks - 1, n_gran - 1)

    for g in range(n_gran):
        rows = slice(g * gran_rows, (g + 1) * gran_rows)
        y = alpha * xm_ref[0, rows, :] + gate * acc_scr[rows, :]
        o_ref[0, rows, :] = _norm_rows(y) * lnw_ref[...] + lnb_ref[...]


def _ffn_call(x, mod, fw, *, li, ctx_row, tile, gw, alpha):
    wup, wcv, wdn, post_w, post_b = fw
    b, l, d = x.shape
    n_tiles = l // tile
    has_halo = l > gw
    rows_per_tile = tile // gw
    n_grid_rows = l // gw
    ext = tile + (2 * gw if has_halo else 0)

    x_spec = pl.BlockSpec((1, tile, d), lambda bi, i: (bi, i, 0))
    in_specs = [x_spec]
    args = [x]
    if has_halo:
        in_specs.append(pl.BlockSpec(
            (1, gw, d), lambda bi, i: (bi, jnp.maximum(i * rows_per_tile - 1, 0), 0)))
        in_specs.append(pl.BlockSpec(
            (1, gw, d), lambda bi, i: (bi, jnp.minimum((i + 1) * rows_per_tile, n_grid_rows - 1), 0)))
        args += [x, x]
    in_specs += [
        _mod_spec(li, ctx_row)(mod),
        _pick_spec(wup, li), _pick_spec(wcv, li), _pick_spec(wdn, li),
        _pick_spec(post_w, li, 1), _pick_spec(post_b, li, 1),
    ]
    args += [mod, wup, wcv, wdn, post_w, post_b]
    n_gran = max(1, tile // FFN_GRANULE)
    assert ext % (n_gran * BF16_ROWS) == 0 and rows_per_tile % n_gran == 0
    kern = functools.partial(_ffn_kernel, tile=tile, gw=gw, n_tiles=n_tiles, n_gran=n_gran,
                             has_halo=has_halo, alpha=alpha)
    return pl.pallas_call(
        kern,
        grid=(b, n_tiles),
        in_specs=in_specs,
        out_specs=x_spec,
        out_shape=jax.ShapeDtypeStruct(x.shape, _F32),
        scratch_shapes=[
            pltpu.VMEM((ext, d), _BF16),
            pltpu.VMEM((ext, 2 * FFN_CHUNK), _F32),
            pltpu.VMEM((ext, 2 * FFN_CHUNK), _F32),
            pltpu.VMEM((tile, FFN_CHUNK), _BF16),
            pltpu.VMEM((tile, d), _F32),
        ],
        compiler_params=pltpu.CompilerParams(
            dimension_semantics=("arbitrary", "arbitrary"),
            vmem_limit_bytes=VMEM_LIMIT),
        name="channel_mixer",
    )(*args)


def _prep_ffn_weights(w_up, conv_w, w_down):
    depth, d, two_h = w_up.shape
    n_chunks = two_h // 2 // FFN_CHUNK
    wup = w_up.astype(_BF16)
    wcv = conv_w.reshape(depth, 9, two_h)
    wdn = w_down.reshape(depth, n_chunks, FFN_CHUNK, w_down.shape[-1]).astype(_BF16)
    return wup, wcv, wdn


GLA_HEADS = 4
GLA_DK = 32
GLA_DV = 64
GLA_QK = GLA_HEADS * GLA_DK
GLA_V = GLA_HEADS * GLA_DV
GLA_RANK = 16
GLA_CHUNK = 64
GLA_GATE_NORMALIZER = 16.0
GLA_LOG_DECAY_MIN = -1.0


def _log_decay(logit):
    log_sig = jnp.minimum(logit, 0.0) - jnp.log(1.0 + jnp.exp(-jnp.abs(logit)))
    return jnp.maximum(log_sig / GLA_GATE_NORMALIZER, GLA_LOG_DECAY_MIN)


def _chunk_cumsum(x, reverse):
    n = x.shape[0]
    pos = lax.broadcasted_iota(jnp.int32, x.shape, 0) % GLA_CHUNK
    s = 1
    while s < GLA_CHUNK:
        if reverse:
            x = x + jnp.where(pos < GLA_CHUNK - s, pltpu.roll(x, n - s, axis=0), 0.0)
        else:
            x = x + jnp.where(pos >= s, pltpu.roll(x, s, axis=0), 0.0)
        s *= 2
    return x


def _state_mask():
    r = lax.broadcasted_iota(jnp.int32, (GLA_V, GLA_QK), 0) // GLA_DV
    c = lax.broadcasted_iota(jnp.int32, (GLA_V, GLA_QK), 1) // GLA_DK
    return r == c


def _stack4(a):
    return jnp.concatenate([a, a, a, a], axis=0)


def _compact_state(s):
    return (s[0:GLA_DV] + s[GLA_DV:2 * GLA_DV]) + (s[2 * GLA_DV:3 * GLA_DV] + s[3 * GLA_DV:4 * GLA_DV])


def _state_update(state, v_bf, k_dec_bf, decay_row, mask):
    kv_t = lax.dot_general(v_bf, k_dec_bf, (((0,), (0,)), ((), ())), preferred_element_type=_F32)
    return state * decay_row + jnp.where(mask, kv_t, 0.0)


def _state_kernel(x_ref, mod_ref, w_ref, wg_ref, bg_ref, s0_ref, start_ref, fin_ref, st_scr,
                  *, tile, reverse):
    i = pl.program_id(1)
    n_tiles = pl.num_programs(1)
    chunks = tile // GLA_CHUNK

    @pl.when(i == 0)
    def _():
        st_scr[...] = s0_ref[0]

    shift = mod_ref[0, 0:1, :]
    scale1 = 1.0 + mod_ref[0, 1:2, :]
    blocks = []
    for r0 in range(0, tile, MIX_BLOCK):
        h = (_norm_rows(x_ref[0, r0:r0 + MIX_BLOCK, :]) * scale1 + shift).astype(_BF16)
        blocks.append(jnp.dot(h, w_ref[...], preferred_element_type=_F32))
    z = jnp.concatenate(blocks, axis=0)
    k = z[:, 0:GLA_QK]
    v_bf = z[:, GLA_QK:GLA_QK + GLA_V].astype(_BF16)
    low = z[:, GLA_QK + GLA_V:].astype(_BF16)
    g = _log_decay(jnp.dot(low, wg_ref[...], preferred_element_type=_F32) + bg_ref[...])
    run = _chunk_cumsum(g, reverse)
    mask = _state_mask()
    state = st_scr[...]
    order = range(chunks - 1, -1, -1) if reverse else range(chunks)
    for ci in order:
        rows = slice(ci * GLA_CHUNK, (ci + 1) * GLA_CHUNK)
        edge = ci * GLA_CHUNK if reverse else (ci + 1) * GLA_CHUNK - 1
        total = run[edge:edge + 1, :]
        k_dec = (k[rows] * jnp.exp(total - run[rows])).astype(_BF16)
        start_ref[0, ci] = _compact_state(state)
        state = _state_update(state, v_bf[rows], k_dec, jnp.exp(total), mask)
    st_scr[...] = state

    @pl.when(i == n_tiles - 1)
    def _():
        fin_ref[0] = state


def _state_call(x, mod, mw, s0, *, li, ctx_row, tile, reverse):
    di = 1 if reverse else 0
    w, wg, bg = mw["w_state"], mw["wg1"], mw["bg1"]
    b, l, d = x.shape
    n_tiles = l // tile
    chunks = tile // GLA_CHUNK
    tmap = (lambda bi, i: (bi, n_tiles - 1 - i, 0)) if reverse else (lambda bi, i: (bi, i, 0))
    smap = (lambda bi, i: (bi, n_tiles - 1 - i, 0, 0)) if reverse else (lambda bi, i: (bi, i, 0, 0))
    st_spec = pl.BlockSpec((1, GLA_V, GLA_QK), lambda bi, i: (bi, 0, 0))
    kern = functools.partial(_state_kernel, tile=tile, reverse=reverse)
    return pl.pallas_call(
        kern,
        grid=(b, n_tiles),
        in_specs=[
            pl.BlockSpec((1, tile, d), tmap),
            _mod_spec(li, ctx_row)(mod),
            _pick_spec(w, li, di), _pick_spec(wg, li, di), _pick_spec(bg, li, di),
            st_spec,
        ],
        out_specs=[
            pl.BlockSpec((1, chunks, GLA_DV, GLA_QK), smap),
            st_spec,
        ],
        out_shape=[
            jax.ShapeDtypeStruct((b, l // GLA_CHUNK, GLA_DV, GLA_QK), _F32),
            jax.ShapeDtypeStruct((b, GLA_V, GLA_QK), _F32),
        ],
        scratch_shapes=[pltpu.VMEM((GLA_V, GLA_QK), _F32)],
        compiler_params=pltpu.CompilerParams(
            dimension_semantics=("arbitrary", "arbitrary"),
            vmem_limit_bytes=VMEM_LIMIT),
        name="gla_state_rev" if reverse else "gla_state_fwd",
    )(x, mod, w, wg, bg, s0)


GROUP_W = 256
COL_Q, COL_K, COL_V, COL_R, COL_LOW, COL_C, COL_B, COL_D = 0, 128, 256, 512, 768, 896, 1408, 1664
IN_COLS = 2176
HALO = 16
POOL_REACH = 8
MIX_BLOCK = 128
CONV_WIDTH = 31
SGU_HEADS = 4
SGU_HD = 64


def _gelu_tanh(x):
    return 0.5 * x * (1.0 + jnp.tanh(0.7978845608028654 * (x + 0.044715 * (x * x * x))))


def _mixer_kernel(xm_ref, xp_ref, xn_ref, mod_ref, win_ref, wg_ref, bg_ref, gnw_ref,
                  pblk_ref, pscale_ref, sguw_ref, sgub_ref, slnw_ref, slnb_ref,
                  cw_ref, cb_ref, clnw_ref, clnb_ref, wout_ref, plw_ref, plb_ref,
                  sb_ref, sf0_ref,
                  o_ref, sfin_ref,
                  z_scr, zb_scr, y_scr, run_scr, o_scr, cat_scr, st_scr,
                  *, tile, seq_len, alpha):
    i = pl.program_id(1)
    n_tiles = pl.num_programs(1)
    n_blocks = tile // MIX_BLOCK

    @pl.when(i == 0)
    def _():
        st_scr[...] = sf0_ref[0]

    shift = mod_ref[0, 0:1, :]
    scale1 = 1.0 + mod_ref[0, 1:2, :]
    gate = mod_ref[0, 2:3, :]

    def modulated(xv):
        return _norm_rows(xv) * scale1 + shift

    hp = jnp.where(i > 0, modulated(xp_ref[0]), 0.0).astype(_BF16)
    hn = jnp.where(i < n_tiles - 1, modulated(xn_ref[0]), 0.0).astype(_BF16)
    zh = jnp.dot(jnp.concatenate([hp, hn], axis=0), win_ref[:, COL_B:],
                 preferred_element_type=_F32)

    def glu(zd):
        return zd[:, :GROUP_W] * _sigmoid(zd[:, GROUP_W:])

    zb_scr[0:HALO, :] = zh[0:HALO, 0:GROUP_W]
    zb_scr[HALO + tile:, :] = zh[HALO:, 0:GROUP_W]
    y_scr[0:HALO, :] = glu(zh[0:HALO, GROUP_W:])
    y_scr[HALO + tile:, :] = glu(zh[HALO:, GROUP_W:])

    def block_rows(blk):
        return slice(blk * MIX_BLOCK, (blk + 1) * MIX_BLOCK)

    def in_proj(blk):
        rows = block_rows(blk)
        ext_rows = slice(HALO + blk * MIX_BLOCK, HALO + (blk + 1) * MIX_BLOCK)
        h = modulated(xm_ref[0, rows, :]).astype(_BF16)
        z = jnp.dot(h, win_ref[...], preferred_element_type=_F32)
        z_scr[rows, :] = z
        zb_scr[ext_rows, :] = z[:, COL_B:COL_B + GROUP_W]
        y_scr[ext_rows, :] = glu(z[:, COL_D:COL_D + 2 * GROUP_W])
        low = z[:, COL_LOW:COL_LOW + LANES].astype(_BF16)
        g = _log_decay(jnp.dot(low, wg_ref[...], preferred_element_type=_F32) + bg_ref[...])
        run_scr[rows, 0:GLA_QK] = _chunk_cumsum(g[:, 0:GLA_QK], False)
        run_scr[rows, GLA_QK:] = _chunk_cumsum(g[:, GLA_QK:], True)

    lane = lax.broadcasted_iota(jnp.int32, (MIX_BLOCK, GROUP_W), 1)
    row = lax.broadcasted_iota(jnp.int32, (MIX_BLOCK, GROUP_W), 0)
    lane_group = lane // SGU_HD
    pool_half = jnp.left_shift(1, lane_group)

    def local_mixers(blk):
        r0 = blk * MIX_BLOCK
        rows = block_rows(blk)

        n = MIX_BLOCK + 2 * POOL_REACH
        xb = zb_scr[r0 + HALO - POOL_REACH:r0 + HALO - POOL_REACH + n, :]
        p2 = xb + pltpu.roll(xb, 1, axis=0)
        p4 = pltpu.roll(p2, 1, axis=0) + pltpu.roll(p2, n - 1, axis=0)
        p8 = pltpu.roll(p4, 2, axis=0) + pltpu.roll(p4, n - 2, axis=0)
        p16 = pltpu.roll(p8, 4, axis=0) + pltpu.roll(p8, n - 4, axis=0)
        inner = slice(POOL_REACH, POOL_REACH + MIX_BLOCK)
        win = jnp.where(lane_group == 0, p2[inner],
                        jnp.where(lane_group == 1, p4[inner],
                                  jnp.where(lane_group == 2, p8[inner], p16[inner])))
        tok = i * tile + r0 + row
        count = jnp.minimum(tok + pool_half, seq_len) - jnp.maximum(tok - pool_half, 0)
        pooled = win / count.astype(_F32) - xb[inner]
        yb = jnp.dot(pooled.astype(_BF16), pblk_ref[...], preferred_element_type=_F32) * pscale_ref[...]
        cat_scr[rows, GROUP_W:2 * GROUP_W] = yb.astype(_BF16)

        gz = _gelu_tanh(z_scr[rows, COL_C:COL_C + 2 * GROUP_W])
        vn = _norm_rows(gz[:, GROUP_W:]) * slnw_ref[...] + slnb_ref[...]
        vstack = jnp.concatenate(
            [jnp.where(lane_group == hd, vn, 0.0) for hd in range(SGU_HEADS)], axis=0).astype(_BF16)
        sg = jnp.dot(sguw_ref[...], vstack, preferred_element_type=_F32) + sgub_ref[...]
        cat_scr[rows, 2 * GROUP_W:3 * GROUP_W] = (gz[:, :GROUP_W] * sg).astype(_BF16)

        span = MIX_BLOCK + SUBLANES
        halves = []
        for lo in range(0, GROUP_W, LANES):
            conv = None
            for b in range(SUBLANES):
                part = None
                for a in range(4):
                    o = SUBLANES * a + b
                    if 1 <= o <= CONV_WIDTH:
                        term = (cw_ref[o - 1:o, lo:lo + LANES]
                                * y_scr[r0 + SUBLANES * a:r0 + SUBLANES * a + span, lo:lo + LANES])
                        part = term if part is None else part + term
                part = part[b:b + MIX_BLOCK]
                conv = part if conv is None else conv + part
            halves.append(conv)
        conv = jnp.concatenate(halves, axis=1)
        yd = _norm_rows(conv + cb_ref[...]) * clnw_ref[...] + clnb_ref[...]
        cat_scr[rows, 3 * GROUP_W:4 * GROUP_W] = (yd * _sigmoid(yd)).astype(_BF16)

    mask_s = _state_mask()
    vr = lax.broadcasted_iota(jnp.int32, (GLA_V, GLA_V), 0) // GLA_DV
    vc = lax.broadcasted_iota(jnp.int32, (GLA_V, GLA_V), 1) // GLA_DV
    mask_v = vr == vc
    t_id = lax.broadcasted_iota(jnp.int32, (GLA_CHUNK, GLA_V), 0)
    s_id = lax.broadcasted_iota(jnp.int32, (GLA_CHUNK, GLA_V), 1) % GLA_CHUNK
    nt_dims = (((1,), (1,)), ((), ()))

    zero = jnp.zeros((), _BF16)

    def chunk_rows(ci):
        return slice(ci * GLA_CHUNK, (ci + 1) * GLA_CHUNK)

    def scores(ci):
        r = chunk_rows(ci)
        q = z_scr[r, COL_Q:COL_Q + GLA_QK] * (GLA_DK ** -0.5)
        k = z_scr[r, COL_K:COL_K + GLA_QK]
        cf = run_scr[r, 0:GLA_QK]
        rb = run_scr[r, GLA_QK:]
        qf = (q * jnp.exp(cf)).astype(_BF16)
        kf = (k * jnp.exp(-cf)).astype(_BF16)
        qb = (q * jnp.exp(rb)).astype(_BF16)
        kb = (k * jnp.exp(-rb)).astype(_BF16)
        a_f = lax.dot_general(qf, jnp.where(mask_s, _stack4(kf), zero), nt_dims,
                              preferred_element_type=_F32)
        a_b = lax.dot_general(qb, jnp.where(mask_s, _stack4(kb), zero), nt_dims,
                              preferred_element_type=_F32)
        a = (jnp.where(s_id <= t_id, a_f, 0.0) + jnp.where(s_id >= t_id, a_b, 0.0)).astype(_BF16)
        return a, qf, qb

    def outputs(ci, a, qf, qb):
        r = chunk_rows(ci)
        k = z_scr[r, COL_K:COL_K + GLA_QK]
        v = z_scr[r, COL_V:COL_V + GLA_V].astype(_BF16)
        cf = run_scr[r, 0:GLA_QK]
        total = cf[GLA_CHUNK - 1:GLA_CHUNK, :]
        k_dec = (k * jnp.exp(total - cf)).astype(_BF16)
        o = jnp.dot(a, jnp.where(mask_v, _stack4(v), zero), preferred_element_type=_F32)
        sf = st_scr[...]
        sb = jnp.where(mask_s, _stack4(sb_ref[0, ci]), 0.0)
        o = o + lax.dot_general(qf, sf.astype(_BF16), nt_dims, preferred_element_type=_F32)
        o = o + lax.dot_general(qb, sb.astype(_BF16), nt_dims, preferred_element_type=_F32)
        o_scr[r, :] = o
        st_scr[...] = _state_update(sf, v, k_dec, jnp.exp(total), mask_s)

    head_mean = jnp.where(mask_v, 1.0 / GLA_DV, 0.0).astype(_BF16)

    def readout(blk):
        rows = block_rows(blk)
        o = o_scr[rows, :]
        sq = o * o
        sq_hi = sq.astype(_BF16)
        sq_lo = (sq - sq_hi.astype(_F32)).astype(_BF16)
        ms = (jnp.dot(sq_hi, head_mean, preferred_element_type=_F32)
              + jnp.dot(sq_lo, head_mean, preferred_element_type=_F32))
        rg = z_scr[rows, COL_R:COL_R + GROUP_W]
        ya = o * lax.rsqrt(ms + EPS) * gnw_ref[...] * (rg * _sigmoid(rg))
        cat_scr[rows, 0:GROUP_W] = ya.astype(_BF16)

    def out_proj(blk):
        rows = block_rows(blk)
        y = jnp.dot(cat_scr[rows, :], wout_ref[...], preferred_element_type=_F32)
        o_ref[0, rows, :] = (_norm_rows(alpha * xm_ref[0, rows, :] + gate * y) * plw_ref[...]
                             + plb_ref[...])

    chunks_per_block = MIX_BLOCK // GLA_CHUNK
    n_gla_chunks = tile // GLA_CHUNK
    in_proj(0)
    if n_blocks > 1:
        in_proj(1)
    cur = scores(0)
    for blk in range(n_blocks):
        if blk + 2 < n_blocks:
            in_proj(blk + 2)
        if blk > 0:
            out_proj(blk - 1)
        local_mixers(blk)
        for ci in range(blk * chunks_per_block, (blk + 1) * chunks_per_block):
            nxt = scores(ci + 1) if ci + 1 < n_gla_chunks else None
            outputs(ci, *cur)
            cur = nxt
        readout(blk)
    out_proj(n_blocks - 1)

    @pl.when(i == n_tiles - 1)
    def _():
        sfin_ref[0] = st_scr[...]


def _mixer_call(x, mod, mw, sb_start, sf0, *, li, ctx_row, tile, alpha):
    b, l, d = x.shape
    n_tiles = l // tile
    halo_per_tile = tile // HALO
    n_halo_blocks = l // HALO
    consts = [mw[k] for k in ("w_in", "wg2", "bg2", "gnw", "pool_blk", "pool_scale", "sgu_w", "sgu_b",
                              "sgu_ln_w", "sgu_ln_b", "cm_w", "cm_b", "cm_ln_w", "cm_ln_b", "w_out")]
    const_specs = [_pick_spec(c, li) for c in consts]
    consts += [mw["post_w"], mw["post_b"]]
    const_specs += [_pick_spec(mw["post_w"], li, 0), _pick_spec(mw["post_b"], li, 0)]
    x_spec = pl.BlockSpec((1, tile, d), lambda bi, i: (bi, i, 0))
    st_spec = pl.BlockSpec((1, GLA_V, GLA_QK), lambda bi, i: (bi, 0, 0))
    in_specs = [
        x_spec,
        pl.BlockSpec((1, HALO, d), lambda bi, i: (bi, jnp.maximum(i * halo_per_tile - 1, 0), 0)),
        pl.BlockSpec((1, HALO, d), lambda bi, i: (bi, jnp.minimum((i + 1) * halo_per_tile, n_halo_blocks - 1), 0)),
        _mod_spec(li, ctx_row)(mod),
    ] + const_specs + [
        pl.BlockSpec((1, tile // GLA_CHUNK, GLA_DV, GLA_QK), lambda bi, i: (bi, i, 0, 0)),
        st_spec,
    ]
    kern = functools.partial(_mixer_kernel, tile=tile, seq_len=l, alpha=alpha)
    return pl.pallas_call(
        kern,
        grid=(b, n_tiles),
        in_specs=in_specs,
        out_specs=[x_spec, st_spec],
        out_shape=[jax.ShapeDtypeStruct(x.shape, _F32),
                   jax.ShapeDtypeStruct((b, GLA_V, GLA_QK), _F32)],
        scratch_shapes=[
            pltpu.VMEM((tile, IN_COLS), _F32),
            pltpu.VMEM((tile + 2 * HALO, GROUP_W), _F32),
            pltpu.VMEM((tile + 2 * HALO, GROUP_W), _F32),
            pltpu.VMEM((tile, 2 * GLA_QK), _F32),
            pltpu.VMEM((tile, GLA_V), _F32),
            pltpu.VMEM((tile, 4 * GROUP_W), _BF16),
            pltpu.VMEM((GLA_V, GLA_QK), _F32),
        ],
        compiler_params=pltpu.CompilerParams(
            dimension_semantics=("arbitrary", "arbitrary"),
            vmem_limit_bytes=VMEM_LIMIT),
        name="token_mixer",
    )(x, x, x, mod, *consts, sb_start, sf0)


def _prep_mixer_weights(w_in, gla_w_gate, gla_b_gate, gla_norm_w, pool_w, pool_scale, sgu_w, sgu_b,
                        sgu_ln_w, sgu_ln_b, cm_conv_w, cm_conv_b, cm_ln_w, cm_ln_b, w_out,
                        post_w, post_b):
    depth, d, _ = w_in.shape
    n_pool = pool_w.shape[1]
    qkvr = w_in[:, :, 0:768]
    low = w_in[:, :, 768:800]
    zb = w_in[:, :, 800:1056]
    zc = w_in[:, :, 1056:1568]
    zd = w_in[:, :, 1568:2080]
    pad = jnp.zeros((depth, d, LANES - 2 * GLA_RANK), _F32)
    w_main = jnp.concatenate([qkvr, low, pad, zc, zb, zd], axis=2).astype(_BF16)
    kv = w_in[:, :, 128:512]
    pad1 = jnp.zeros((depth, d, LANES - GLA_RANK), _F32)
    w_state = jnp.stack(
        [jnp.concatenate([kv, low[:, :, di * GLA_RANK:(di + 1) * GLA_RANK], pad1], axis=2) for di in range(2)],
        axis=1).astype(_BF16)
    wg1 = jnp.pad(gla_w_gate, ((0, 0), (0, 0), (0, LANES - GLA_RANK), (0, 0))).astype(_BF16)
    zero = jnp.zeros((depth, GLA_RANK, GLA_QK), _F32)
    wg2 = jnp.concatenate([
        jnp.concatenate([gla_w_gate[:, 0], zero], axis=2),
        jnp.concatenate([zero, gla_w_gate[:, 1]], axis=2),
        jnp.zeros((depth, LANES - 2 * GLA_RANK, 2 * GLA_QK), _F32)], axis=1).astype(_BF16)
    eye = jnp.eye(n_pool, dtype=_F32)
    pool_blk = (pool_w[:, :, :, None, :] * eye[None, :, None, :, None]).reshape(depth, GROUP_W, GROUP_W)
    row = lambda a: a[:, None, :]
    return {
        "w_in": w_main, "w_state": w_state, "wg1": wg1, "bg1": gla_b_gate[:, :, None, :],
        "wg2": wg2, "bg2": gla_b_gate.reshape(depth, 1, 2 * GLA_QK),
        "gnw": row(gla_norm_w), "pool_blk": pool_blk.astype(_BF16), "pool_scale": row(pool_scale),
        "sgu_w": sgu_w.transpose(0, 2, 1, 3).reshape(depth, MIX_BLOCK, SGU_HEADS * MIX_BLOCK).astype(_BF16),
        "sgu_b": jnp.repeat(sgu_b.transpose(0, 2, 1), SGU_HD, axis=2),
        "sgu_ln_w": row(sgu_ln_w), "sgu_ln_b": row(sgu_ln_b),
        "cm_w": jnp.pad(cm_conv_w, ((0, 0), (0, 1), (0, 0))),
        "cm_b": row(cm_conv_b), "cm_ln_w": row(cm_ln_w), "cm_ln_b": row(cm_ln_b),
        "w_out": w_out.astype(_BF16),
        "post_w": post_w[:, :, None, :], "post_b": post_b[:, :, None, :],
    }


def _mod_kernel(c_ref, w_ref, b_ref, o_ref):
    cv = c_ref[...]

    def split(a):
        hi = a.astype(_BF16)
        return hi, (a - hi.astype(_F32)).astype(_BF16)

    a_hi, a_lo = split(cv * _sigmoid(cv))
    w_hi, w_lo = split(w_ref[0])
    dot = functools.partial(jnp.dot, preferred_element_type=_F32)
    o_ref[0] = (dot(a_hi, w_lo) + dot(a_lo, w_hi)) + dot(a_hi, w_hi) + b_ref[0]


def _mod_call(cond, w_mod, b_mod):
    depth, d, six_d = w_mod.shape
    rows = cond.shape[0]
    return pl.pallas_call(
        _mod_kernel,
        grid=(depth, six_d // d),
        in_specs=[
            pl.BlockSpec((rows, d), lambda li, j: (0, 0)),
            pl.BlockSpec((1, d, d), lambda li, j: (li, 0, j)),
            pl.BlockSpec((1, 1, d), lambda li, j: (li, 0, j)),
        ],
        out_specs=pl.BlockSpec((1, rows, d), lambda li, j: (li, 0, j)),
        out_shape=jax.ShapeDtypeStruct((depth, rows, six_d), _F32),
        compiler_params=pltpu.CompilerParams(
            dimension_semantics=("arbitrary", "arbitrary"),
            vmem_limit_bytes=VMEM_LIMIT),
        name="adaln_modulation",
    )(cond, w_mod, b_mod.reshape(depth, 1, six_d))


def _pick_tile(length, target):
    return min(length, target)


def kernel(x, c, ctx, c_ctx, w_mod, b_mod, w_in, gla_w_gate, gla_b_gate, gla_norm_w, pool_w, pool_scale, sgu_w, sgu_b, sgu_ln_w, sgu_ln_b, cm_conv_w, cm_conv_b, cm_ln_w, cm_ln_b, w_out, ffn_w_up, ffn_conv_w, ffn_w_down, post_ln_w, post_ln_b):
    batch, seq, d = x.shape
    ctx_len = ctx.shape[1]
    depth = w_mod.shape[0]
    alpha = (2 * depth) ** 0.25

    cond_rows = -(-(batch + 1) // SUBLANES) * SUBLANES
    cond = jnp.concatenate([c, c_ctx[None, :], jnp.zeros((cond_rows - batch - 1, d), _F32)], axis=0)
    mod = _mod_call(cond, w_mod, b_mod).reshape(depth, cond_rows, 6, d)

    lat_mix_tile = _pick_tile(seq, MIX_TILE)
    ctx_mix_tile = _pick_tile(ctx_len, MIX_TILE)
    lat_ffn_tile = _pick_tile(seq, FFN_TILE)
    zero_state = jnp.zeros((batch, GLA_V, GLA_QK), _F32)

    mw = _prep_mixer_weights(w_in, gla_w_gate, gla_b_gate, gla_norm_w, pool_w, pool_scale, sgu_w, sgu_b,
                             sgu_ln_w, sgu_ln_b, cm_conv_w, cm_conv_b, cm_ln_w, cm_ln_b, w_out,
                             post_ln_w, post_ln_b)
    fw = _prep_ffn_weights(ffn_w_up, ffn_conv_w, ffn_w_down) + (mw["post_w"], mw["post_b"])
    lat = dict(ctx_row=None)
    cx = dict(ctx_row=batch)

    for li in range(depth):
        with_ctx_out = li < depth - 1
        ctx_sb, ctx_sb_fin = _state_call(ctx, mod, mw, zero_state, li=li, tile=ctx_mix_tile,
                                         reverse=True, **cx)
        if with_ctx_out:
            ctx_mixed, ctx_sf_fin = _mixer_call(ctx, mod, mw, ctx_sb, zero_state, li=li,
                                                tile=ctx_mix_tile, alpha=alpha, **cx)
        else:
            _, ctx_sf_fin = _state_call(ctx, mod, mw, zero_state, li=li, tile=ctx_mix_tile,
                                        reverse=False, **cx)

        lat_sb, _ = _state_call(x, mod, mw, ctx_sb_fin, li=li, tile=lat_mix_tile, reverse=True, **lat)
        x, _ = _mixer_call(x, mod, mw, lat_sb, ctx_sf_fin, li=li, tile=lat_mix_tile, alpha=alpha, **lat)
        x = _ffn_call(x, mod, fw, li=li, tile=lat_ffn_tile, gw=GRID_W, alpha=alpha, **lat)
        if with_ctx_out:
            ctx = _ffn_call(ctx_mixed, mod, fw, li=li, tile=ctx_len, gw=ctx_len, alpha=alpha, **cx)
    return x
```

```python
import functools

import jax
import jax.numpy as jnp
from jax import lax
from jax.experimental import pallas as pl
from jax.experimental.pallas import tpu as pltpu

EPS = 1e-6
LANES = 128
SUBLANES = 8
GRID_W = 64
FFN_CHUNK = 256
FFN_GRANULE = 512
BF16_ROWS = 16
MIX_TILE = 512
FFN_TILE = 1024
VMEM_LIMIT = 60 * 1024 * 1024

_F32 = jnp.float32
_BF16 = jnp.bfloat16


def _norm_rows(x):
    mu = jnp.mean(x, axis=-1, keepdims=True)
    xc = x - mu
    var = jnp.mean(xc * xc, axis=-1, keepdims=True)
    return xc * lax.rsqrt(var + EPS)


def _sigmoid(x):
    return 1.0 / (1.0 + jnp.exp(-x))


def _pick_spec(arr, *lead):
    rest = arr.shape[len(lead):]
    index = tuple(lead) + (0,) * len(rest)
    return pl.BlockSpec((None,) * len(lead) + rest, lambda *_: index, pipeline_mode=pl.Buffered(1))


def _mod_spec(li, ctx_row):
    if ctx_row is None:
        return lambda mod: pl.BlockSpec((None, 1) + mod.shape[2:], lambda bi, i: (li, bi, 0, 0))
    return lambda mod: pl.BlockSpec((None, 1) + mod.shape[2:], lambda bi, i: (li, ctx_row, 0, 0))


def _ffn_kernel(*refs, tile, gw, n_tiles, n_gran, has_halo, alpha):
    if has_halo:
        (xm_ref, xp_ref, xn_ref, mod_ref, wup_ref, wcv_ref, wdn_ref, lnw_ref, lnb_ref,
         o_ref, h_scr, u0_scr, u1_scr, s_scr, acc_scr) = refs
    else:
        (xm_ref, mod_ref, wup_ref, wcv_ref, wdn_ref, lnw_ref, lnb_ref,
         o_ref, h_scr, u0_scr, u1_scr, s_scr, acc_scr) = refs
    u_bufs = (u0_scr, u1_scr)
    i = pl.program_id(1)
    n_chunks = wdn_ref.shape[0]
    pad = gw if has_halo else 0
    n_rows = tile // gw

    shift = mod_ref[0, 3:4, :]
    scale1 = 1.0 + mod_ref[0, 4:5, :]
    gate = mod_ref[0, 5:6, :]

    def modulated(xv):
        return _norm_rows(xv) * scale1 + shift

    ext = h_scr.shape[0]
    up_rows = ext // n_gran
    rows_per_gran = n_rows // n_gran
    gran_rows = rows_per_gran * gw

    def modulate_granule(g):
        rows = slice(g * gran_rows, (g + 1) * gran_rows)
        h_scr[pad + g * gran_rows:pad + (g + 1) * gran_rows, :] = modulated(xm_ref[0, rows, :]).astype(_BF16)
        if has_halo and g == 0:
            hp = jnp.where(i > 0, modulated(xp_ref[0]), 0.0)
            h_scr[0:gw, :] = hp.astype(_BF16)
        if has_halo and g == n_gran - 1:
            hn = jnp.where(i < n_tiles - 1, modulated(xn_ref[0]), 0.0)
            h_scr[pad + tile:pad + tile + gw, :] = hn.astype(_BF16)

    hidden = n_chunks * FFN_CHUNK

    def chunk_cols(c, half):
        return pl.ds(pl.multiple_of(half * hidden + c * FFN_CHUNK, FFN_CHUNK), FFN_CHUNK)

    def up_proj(c, buf, g):
        rows = pl.ds(pl.multiple_of(g * up_rows, BF16_ROWS), up_rows)
        for half in range(2):
            u = jnp.dot(h_scr[rows, :], wup_ref[:, chunk_cols(c, half)], preferred_element_type=_F32)
            for j in range(FFN_CHUNK // LANES):
                u_bufs[buf][half * (FFN_CHUNK // LANES) + j, rows, :] = u[:, j * LANES:(j + 1) * LANES]

    edge = jnp.zeros((1, LANES), _F32)

    def shift_tokens(v, down):
        if down:
            return jnp.concatenate([edge, v[:-1]], axis=0)
        return jnp.concatenate([v[1:], edge], axis=0)

    def conv_cols(ub, wc, base, half, lo):
        ucol = (half * FFN_CHUNK + lo) // LANES

        def tap(k, off):
            return wc[k:k + 1, lo:lo + LANES] * ub[ucol, pl.ds(base + off, gw), :]

        def col_taps(dc):
            k = dc + 1
            v = tap(3 + k, pad)
            if has_halo:
                v = v + tap(k, 0) + tap(6 + k, 2 * gw)
            return v
        return col_taps(0) + shift_tokens(col_taps(-1), True) + shift_tokens(col_taps(1), False)

    def conv_gate(c, buf, g):
        ub = u_bufs[buf]
        wc_a = wcv_ref[:, chunk_cols(c, 0)]
        wc_g = wcv_ref[:, chunk_cols(c, 1)]
        for r in range(rows_per_gran):
            base = pl.multiple_of(g * gran_rows + r * gw, gw)
            for j in range(FFN_CHUNK // LANES):
                a = conv_cols(ub, wc_a, base, 0, j * LANES)
                gt = conv_cols(ub, wc_g, base, 1, j * LANES)
                half_g = 0.5 * gt
                p = a * half_g
                s_scr[pl.ds(base, gw), j * LANES:(j + 1) * LANES] = (p + p * jnp.tanh(half_g)).astype(_BF16)

    def down_proj(c, g):
        rows = pl.ds(pl.multiple_of(g * gran_rows, gran_rows), gran_rows)
        f = jnp.dot(s_scr[rows, :], wdn_ref[c], preferred_element_type=_F32)
        for j in range(acc_scr.shape[0]):
            acc_scr[j, rows, :] += f[:, j * LANES:(j + 1) * LANES]

    def chunk_steps(c, buf, has_next):
        def step(g, carry):
            if has_next:
                up_proj(c + 1, 1 - buf, g)
            wrap = jnp.where(g == 0, 1, 0)
            down_proj(jnp.maximum(c - wrap, 0), g - 1 + wrap * n_gran)
            conv_gate(c, buf, g)
            return carry
        lax.fori_loop(0, n_gran, step, 0)

    acc_scr[...] = jnp.zeros_like(acc_scr)
    s_scr[...] = jnp.zeros_like(s_scr)
    done = 0
    for g in range(n_gran):
        need = min(n_gran, -(-((g + 1) * up_rows - pad) // gran_rows))
        for gm in range(done, need):
            modulate_granule(gm)
        done = max(done, need)
        up_proj(0, 0, g)

    def body(k, carry):
        c = 2 * k
        chunk_steps(c, 0, True)
        chunk_steps(c + 1, 1, True)
        return carry

    n_pairs = (n_chunks - 1) // 2
    lax.fori_loop(0, n_pairs, body, 0)
    if n_chunks % 2 == 0:
        chunk_steps(n_chunks - 2, 0, True)
    chunk_steps(n_chunks - 1, (n_chunks - 1) % 2, False)
    down_proj(n_chunks - 1, n_gran - 1)

    for g in range(n_gran):
        rows = slice(g * gran_rows, (g + 1) * gran_rows)
        f = jnp.concatenate([acc_scr[j, rows, :] for j in range(acc_scr.shape[0])], axis=1)
        y = alpha * xm_ref[0, rows, :] + gate * f
        o_ref[0, rows, :] = _norm_rows(y) * lnw_ref[...] + lnb_ref[...]


def _ffn_call(x, mod, fw, *, li, ctx_row, tile, gw, alpha):
    wup, wcv, wdn, post_w, post_b = fw
    b, l, d = x.shape
    n_tiles = l // tile
    has_halo = l > gw
    rows_per_tile = tile // gw
    n_grid_rows = l // gw
    ext = tile + (2 * gw if has_halo else 0)

    x_spec = pl.BlockSpec((1, tile, d), lambda bi, i: (bi, i, 0))
    in_specs = [x_spec]
    args = [x]
    if has_halo:
        in_specs.append(pl.BlockSpec(
            (1, gw, d), lambda bi, i: (bi, jnp.maximum(i * rows_per_tile - 1, 0), 0)))
        in_specs.append(pl.BlockSpec(
            (1, gw, d), lambda bi, i: (bi, jnp.minimum((i + 1) * rows_per_tile, n_grid_rows - 1), 0)))
        args += [x, x]
    in_specs += [
        _mod_spec(li, ctx_row)(mod),
        _pick_spec(wup, li), _pick_spec(wcv, li), _pick_spec(wdn, li),
        _pick_spec(post_w, li, 1), _pick_spec(post_b, li, 1),
    ]
    args += [mod, wup, wcv, wdn, post_w, post_b]
    n_gran = max(1, tile // FFN_GRANULE)
    assert ext % (n_gran * BF16_ROWS) == 0 and rows_per_tile % n_gran == 0
    kern = functools.partial(_ffn_kernel, tile=tile, gw=gw, n_tiles=n_tiles, n_gran=n_gran,
                             has_halo=has_halo, alpha=alpha)
    return pl.pallas_call(
        kern,
        grid=(b, n_tiles),
        in_specs=in_specs,
        out_specs=x_spec,
        out_shape=jax.ShapeDtypeStruct(x.shape, _F32),
        scratch_shapes=[
            pltpu.VMEM((ext, d), _BF16),
            pltpu.VMEM((2 * FFN_CHUNK // LANES, ext, LANES), _F32),
            pltpu.VMEM((2 * FFN_CHUNK // LANES, ext, LANES), _F32),
            pltpu.VMEM((tile, FFN_CHUNK), _BF16),
            pltpu.VMEM((d // LANES, tile, LANES), _F32),
        ],
        compiler_params=pltpu.CompilerParams(
            dimension_semantics=("arbitrary", "arbitrary"),
            vmem_limit_bytes=VMEM_LIMIT),
        name="channel_mixer",
    )(*args)


def _prep_ffn_weights(w_up, conv_w, w_down):
    depth, d, two_h = w_up.shape
    n_chunks = two_h // 2 // FFN_CHUNK
    wup = w_up.astype(_BF16)
    wcv = conv_w.reshape(depth, 9, two_h)
    wdn = w_down.reshape(depth, n_chunks, FFN_CHUNK, w_down.shape[-1]).astype(_BF16)
    return wup, wcv, wdn


GLA_HEADS = 4
GLA_DK = 32
GLA_DV = 64
GLA_QK = GLA_HEADS * GLA_DK
GLA_V = GLA_HEADS * GLA_DV
GLA_RANK = 16
GLA_CHUNK = 64
GLA_GATE_NORMALIZER = 16.0
GLA_LOG_DECAY_MIN = -1.0


def _log_decay(logit):
    log_sig = jnp.minimum(logit, 0.0) - jnp.log(1.0 + jnp.exp(-jnp.abs(logit)))
    return jnp.maximum(log_sig / GLA_GATE_NORMALIZER, GLA_LOG_DECAY_MIN)


def _chunk_cumsum(x, reverse):
    n = x.shape[0]
    pos = lax.broadcasted_iota(jnp.int32, x.shape, 0) % GLA_CHUNK
    s = 1
    while s < GLA_CHUNK:
        if reverse:
            x = x + jnp.where(pos < GLA_CHUNK - s, pltpu.roll(x, n - s, axis=0), 0.0)
        else:
            x = x + jnp.where(pos >= s, pltpu.roll(x, s, axis=0), 0.0)
        s *= 2
    return x


def _state_mask():
    r = lax.broadcasted_iota(jnp.int32, (GLA_V, GLA_QK), 0) // GLA_DV
    c = lax.broadcasted_iota(jnp.int32, (GLA_V, GLA_QK), 1) // GLA_DK
    return r == c


def _stack4(a):
    return jnp.concatenate([a, a, a, a], axis=0)


def _compact_state(s):
    return (s[0:GLA_DV] + s[GLA_DV:2 * GLA_DV]) + (s[2 * GLA_DV:3 * GLA_DV] + s[3 * GLA_DV:4 * GLA_DV])


def _state_update(state, v_bf, k_dec_bf, decay_row, mask):
    kv_t = lax.dot_general(v_bf, k_dec_bf, (((0,), (0,)), ((), ())), preferred_element_type=_F32)
    return state * decay_row + jnp.where(mask, kv_t, 0.0)


def _state_kernel(x_ref, mod_ref, w_ref, wg_ref, bg_ref, s0_ref, start_ref, fin_ref, st_scr,
                  *, tile, reverse):
    i = pl.program_id(1)
    n_tiles = pl.num_programs(1)
    chunks = tile // GLA_CHUNK

    @pl.when(i == 0)
    def _():
        st_scr[...] = s0_ref[0]

    shift = mod_ref[0, 0:1, :]
    scale1 = 1.0 + mod_ref[0, 1:2, :]
    blocks = []
    for r0 in range(0, tile, MIX_BLOCK):
        h = (_norm_rows(x_ref[0, r0:r0 + MIX_BLOCK, :]) * scale1 + shift).astype(_BF16)
        blocks.append(jnp.dot(h, w_ref[...], preferred_element_type=_F32))
    z = jnp.concatenate(blocks, axis=0)
    k = z[:, 0:GLA_QK]
    v_bf = z[:, GLA_QK:GLA_QK + GLA_V].astype(_BF16)
    low = z[:, GLA_QK + GLA_V:].astype(_BF16)
    g = _log_decay(jnp.dot(low, wg_ref[...], preferred_element_type=_F32) + bg_ref[...])
    run = _chunk_cumsum(g, reverse)
    mask = _state_mask()
    state = st_scr[...]
    order = range(chunks - 1, -1, -1) if reverse else range(chunks)
    for ci in order:
        rows = slice(ci * GLA_CHUNK, (ci + 1) * GLA_CHUNK)
        edge = ci * GLA_CHUNK if reverse else (ci + 1) * GLA_CHUNK - 1
        total = run[edge:edge + 1, :]
        k_dec = (k[rows] * jnp.exp(total - run[rows])).astype(_BF16)
        start_ref[0, ci] = _compact_state(state)
        state = _state_update(state, v_bf[rows], k_dec, jnp.exp(total), mask)
    st_scr[...] = state

    @pl.when(i == n_tiles - 1)
    def _():
        fin_ref[0] = state


def _state_call(x, mod, mw, s0, *, li, ctx_row, tile, reverse):
    di = 1 if reverse else 0
    w, wg, bg = mw["w_state"], mw["wg1"], mw["bg1"]
    b, l, d = x.shape
    n_tiles = l // tile
    chunks = tile // GLA_CHUNK
    tmap = (lambda bi, i: (bi, n_tiles - 1 - i, 0)) if reverse else (lambda bi, i: (bi, i, 0))
    smap = (lambda bi, i: (bi, n_tiles - 1 - i, 0, 0)) if reverse else (lambda bi, i: (bi, i, 0, 0))
    st_spec = pl.BlockSpec((1, GLA_V, GLA_QK), lambda bi, i: (bi, 0, 0))
    kern = functools.partial(_state_kernel, tile=tile, reverse=reverse)
    return pl.pallas_call(
        kern,
        grid=(b, n_tiles),
        in_specs=[
            pl.BlockSpec((1, tile, d), tmap),
            _mod_spec(li, ctx_row)(mod),
            _pick_spec(w, li, di), _pick_spec(wg, li, di), _pick_spec(bg, li, di),
            st_spec,
        ],
        out_specs=[
            pl.BlockSpec((1, chunks, GLA_DV, GLA_QK), smap),
            st_spec,
        ],
        out_shape=[
            jax.ShapeDtypeStruct((b, l // GLA_CHUNK, GLA_DV, GLA_QK), _F32),
            jax.ShapeDtypeStruct((b, GLA_V, GLA_QK), _F32),
        ],
        scratch_shapes=[pltpu.VMEM((GLA_V, GLA_QK), _F32)],
        compiler_params=pltpu.CompilerParams(
            dimension_semantics=("arbitrary", "arbitrary"),
            vmem_limit_bytes=VMEM_LIMIT),
        name="gla_state_rev" if reverse else "gla_state_fwd",
    )(x, mod, w, wg, bg, s0)


GROUP_W = 256
COL_Q, COL_K, COL_V, COL_R, COL_LOW, COL_C, COL_B, COL_D = 0, 128, 256, 512, 768, 896, 1408, 1664
IN_COLS = 2176
HALO = 16
POOL_REACH = 8
MIX_BLOCK = 128
CONV_WIDTH = 31
SGU_HEADS = 4
SGU_HD = 64


def _gelu_tanh(x):
    return 0.5 * x * (1.0 + jnp.tanh(0.7978845608028654 * (x + 0.044715 * (x * x * x))))


def _mixer_kernel(xm_ref, xp_ref, xn_ref, mod_ref, win_ref, wg_ref, bg_ref, gnw_ref,
                  pblk_ref, pscale_ref, sguw_ref, sgub_ref, slnw_ref, slnb_ref,
                  cw_ref, cb_ref, clnw_ref, clnb_ref, wout_ref, plw_ref, plb_ref,
                  sb_ref, sf0_ref,
                  o_ref, sfin_ref,
                  z_scr, zb_scr, y_scr, run_scr, o_scr, cat_scr, st_scr,
                  *, tile, seq_len, alpha):
    i = pl.program_id(1)
    n_tiles = pl.num_programs(1)
    n_blocks = tile // MIX_BLOCK

    @pl.when(i == 0)
    def _():
        st_scr[...] = sf0_ref[0]

    shift = mod_ref[0, 0:1, :]
    scale1 = 1.0 + mod_ref[0, 1:2, :]
    gate = mod_ref[0, 2:3, :]

    def modulated(xv):
        return _norm_rows(xv) * scale1 + shift

    hp = jnp.where(i > 0, modulated(xp_ref[0]), 0.0).astype(_BF16)
    hn = jnp.where(i < n_tiles - 1, modulated(xn_ref[0]), 0.0).astype(_BF16)
    zh = jnp.dot(jnp.concatenate([hp, hn], axis=0), win_ref[:, COL_B:],
                 preferred_element_type=_F32)

    def glu(zd):
        return zd[:, :GROUP_W] * _sigmoid(zd[:, GROUP_W:])

    zb_scr[0:HALO, :] = zh[0:HALO, 0:GROUP_W]
    zb_scr[HALO + tile:, :] = zh[HALO:, 0:GROUP_W]
    y_scr[0:HALO, :] = glu(zh[0:HALO, GROUP_W:])
    y_scr[HALO + tile:, :] = glu(zh[HALO:, GROUP_W:])

    def span_rows(start, size, align=SUBLANES):
        if isinstance(start, int):
            return slice(start, start + size)
        return pl.ds(pl.multiple_of(start, align), size)

    def block_rows(blk):
        return span_rows(blk * MIX_BLOCK, MIX_BLOCK, MIX_BLOCK)

    def in_proj(blk):
        rows = block_rows(blk)
        ext_rows = span_rows(HALO + blk * MIX_BLOCK, MIX_BLOCK, HALO)
        h = modulated(xm_ref[0, rows, :]).astype(_BF16)
        z = jnp.dot(h, win_ref[...], preferred_element_type=_F32)
        z_scr[rows, :] = z
        zb_scr[ext_rows, :] = z[:, COL_B:COL_B + GROUP_W]
        y_scr[ext_rows, :] = glu(z[:, COL_D:COL_D + 2 * GROUP_W])
        low = z[:, COL_LOW:COL_LOW + LANES].astype(_BF16)
        g = _log_decay(jnp.dot(low, wg_ref[...], preferred_element_type=_F32) + bg_ref[...])
        run_scr[rows, 0:GLA_QK] = _chunk_cumsum(g[:, 0:GLA_QK], False)
        run_scr[rows, GLA_QK:] = _chunk_cumsum(g[:, GLA_QK:], True)

    lane = lax.broadcasted_iota(jnp.int32, (MIX_BLOCK, GROUP_W), 1)
    row = lax.broadcasted_iota(jnp.int32, (MIX_BLOCK, GROUP_W), 0)
    lane_group = lane // SGU_HD
    pool_half = jnp.left_shift(1, lane_group)

    def local_mixers(blk):
        r0 = blk * MIX_BLOCK
        rows = block_rows(blk)

        n = MIX_BLOCK + 2 * POOL_REACH
        xb = zb_scr[span_rows(r0 + HALO - POOL_REACH, n), :]
        p2 = xb + pltpu.roll(xb, 1, axis=0)
        p4 = pltpu.roll(p2, 1, axis=0) + pltpu.roll(p2, n - 1, axis=0)
        p8 = pltpu.roll(p4, 2, axis=0) + pltpu.roll(p4, n - 2, axis=0)
        p16 = pltpu.roll(p8, 4, axis=0) + pltpu.roll(p8, n - 4, axis=0)
        inner = slice(POOL_REACH, POOL_REACH + MIX_BLOCK)
        win = jnp.where(lane_group == 0, p2[inner],
                        jnp.where(lane_group == 1, p4[inner],
                                  jnp.where(lane_group == 2, p8[inner], p16[inner])))
        tok = i * tile + r0 + row
        count = jnp.minimum(tok + pool_half, seq_len) - jnp.maximum(tok - pool_half, 0)
        pooled = win / count.astype(_F32) - xb[inner]
        yb = jnp.dot(pooled.astype(_BF16), pblk_ref[...], preferred_element_type=_F32) * pscale_ref[...]
        cat_scr[rows, GROUP_W:2 * GROUP_W] = yb.astype(_BF16)

        gz = _gelu_tanh(z_scr[rows, COL_C:COL_C + 2 * GROUP_W])
        vn = _norm_rows(gz[:, GROUP_W:]) * slnw_ref[...] + slnb_ref[...]
        vstack = jnp.concatenate(
            [jnp.where(lane_group == hd, vn, 0.0) for hd in range(SGU_HEADS)], axis=0).astype(_BF16)
        sg = jnp.dot(sguw_ref[...], vstack, preferred_element_type=_F32) + sgub_ref[...]
        cat_scr[rows, 2 * GROUP_W:3 * GROUP_W] = (gz[:, :GROUP_W] * sg).astype(_BF16)

        span = MIX_BLOCK + SUBLANES
        halves = []
        for lo in range(0, GROUP_W, LANES):
            conv = None
            for b in range(SUBLANES):
                part = None
                for a in range(4):
                    o = SUBLANES * a + b
                    if 1 <= o <= CONV_WIDTH:
                        term = (cw_ref[o - 1:o, lo:lo + LANES]
                                * y_scr[span_rows(r0 + SUBLANES * a, span), lo:lo + LANES])
                        part = term if part is None else part + term
                part = part[b:b + MIX_BLOCK]
                conv = part if conv is None else conv + part
            halves.append(conv)
        conv = jnp.concatenate(halves, axis=1)
        yd = _norm_rows(conv + cb_ref[...]) * clnw_ref[...] + clnb_ref[...]
        cat_scr[rows, 3 * GROUP_W:4 * GROUP_W] = (yd * _sigmoid(yd)).astype(_BF16)

    mask_s = _state_mask()
    vr = lax.broadcasted_iota(jnp.int32, (GLA_V, GLA_V), 0) // GLA_DV
    vc = lax.broadcasted_iota(jnp.int32, (GLA_V, GLA_V), 1) // GLA_DV
    mask_v = vr == vc
    t_id = lax.broadcasted_iota(jnp.int32, (GLA_CHUNK, GLA_V), 0)
    s_id = lax.broadcasted_iota(jnp.int32, (GLA_CHUNK, GLA_V), 1) % GLA_CHUNK
    nt_dims = (((1,), (1,)), ((), ()))

    zero = jnp.zeros((), _BF16)

    def chunk_rows(ci):
        return span_rows(ci * GLA_CHUNK, GLA_CHUNK, GLA_CHUNK)

    def scores(ci):
        r = chunk_rows(ci)
        q = z_scr[r, COL_Q:COL_Q + GLA_QK] * (GLA_DK ** -0.5)
        k = z_scr[r, COL_K:COL_K + GLA_QK]
        cf = run_scr[r, 0:GLA_QK]
        rb = run_scr[r, GLA_QK:]
        qf = (q * jnp.exp(cf)).astype(_BF16)
        kf = (k * jnp.exp(-cf)).astype(_BF16)
        qb = (q * jnp.exp(rb)).astype(_BF16)
        kb = (k * jnp.exp(-rb)).astype(_BF16)
        a_f = lax.dot_general(qf, jnp.where(mask_s, _stack4(kf), zero), nt_dims,
                              preferred_element_type=_F32)
        a_b = lax.dot_general(qb, jnp.where(mask_s, _stack4(kb), zero), nt_dims,
                              preferred_element_type=_F32)
        a = (jnp.where(s_id <= t_id, a_f, 0.0) + jnp.where(s_id >= t_id, a_b, 0.0)).astype(_BF16)
        return a, qf, qb

    def outputs(ci, a, qf, qb):
        r = chunk_rows(ci)
        k = z_scr[r, COL_K:COL_K + GLA_QK]
        v = z_scr[r, COL_V:COL_V + GLA_V].astype(_BF16)
        cf = run_scr[r, 0:GLA_QK]
        total = cf[GLA_CHUNK - 1:GLA_CHUNK, :]
        k_dec = (k * jnp.exp(total - cf)).astype(_BF16)
        o = jnp.dot(a, jnp.where(mask_v, _stack4(v), zero), preferred_element_type=_F32)
        sf = st_scr[...]
        sb = jnp.where(mask_s, _stack4(sb_ref[0, ci]), 0.0)
        o = o + lax.dot_general(qf, sf.astype(_BF16), nt_dims, preferred_element_type=_F32)
        o = o + lax.dot_general(qb, sb.astype(_BF16), nt_dims, preferred_element_type=_F32)
        o_scr[r, :] = o
        st_scr[...] = _state_update(sf, v, k_dec, jnp.exp(total), mask_s)

    head_mean = jnp.where(mask_v, 1.0 / GLA_DV, 0.0).astype(_BF16)

    def readout(blk):
        rows = block_rows(blk)
        o = o_scr[rows, :]
        sq = o * o
        sq_hi = sq.astype(_BF16)
        sq_lo = (sq - sq_hi.astype(_F32)).astype(_BF16)
        ms = (jnp.dot(sq_hi, head_mean, preferred_element_type=_F32)
              + jnp.dot(sq_lo, head_mean, preferred_element_type=_F32))
        rg = z_scr[rows, COL_R:COL_R + GROUP_W]
        ya = o * lax.rsqrt(ms + EPS) * gnw_ref[...] * (rg * _sigmoid(rg))
        cat_scr[rows, 0:GROUP_W] = ya.astype(_BF16)

    def out_proj(blk):
        rows = block_rows(blk)
        y = jnp.dot(cat_scr[rows, :], wout_ref[...], preferred_element_type=_F32)
        o_ref[0, rows, :] = (_norm_rows(alpha * xm_ref[0, rows, :] + gate * y) * plw_ref[...]
                             + plb_ref[...])

    chunks_per_block = MIX_BLOCK // GLA_CHUNK
    n_gla_chunks = tile // GLA_CHUNK
    def block_step(blk, cur, stage_ahead, finish_prev, last):
        if stage_ahead:
            in_proj(blk + 2)
        if finish_prev:
            out_proj(blk - 1)
        local_mixers(blk)
        for cj in range(chunks_per_block):
            ci = blk * chunks_per_block + cj
            nxt = None if (last and cj == chunks_per_block - 1) else scores(ci + 1)
            outputs(ci, *cur)
            cur = nxt
        readout(blk)
        return cur

    in_proj(0)
    if n_blocks > 1:
        in_proj(1)
    cur = scores(0)
    for blk in range(n_blocks):
        cur = block_step(blk, cur, blk + 2 < n_blocks, blk > 0, blk == n_blocks - 1)
    out_proj(n_blocks - 1)

    @pl.when(i == n_tiles - 1)
    def _():
        sfin_ref[0] = st_scr[...]


def _mixer_call(x, mod, mw, sb_start, sf0, *, li, ctx_row, tile, alpha):
    b, l, d = x.shape
    n_tiles = l // tile
    halo_per_tile = tile // HALO
    n_halo_blocks = l // HALO
    consts = [mw[k] for k in ("w_in", "wg2", "bg2", "gnw", "pool_blk", "pool_scale", "sgu_w", "sgu_b",
                              "sgu_ln_w", "sgu_ln_b", "cm_w", "cm_b", "cm_ln_w", "cm_ln_b", "w_out")]
    const_specs = [_pick_spec(c, li) for c in consts]
    consts += [mw["post_w"], mw["post_b"]]
    const_specs += [_pick_spec(mw["post_w"], li, 0), _pick_spec(mw["post_b"], li, 0)]
    x_spec = pl.BlockSpec((1, tile, d), lambda bi, i: (bi, i, 0))
    st_spec = pl.BlockSpec((1, GLA_V, GLA_QK), lambda bi, i: (bi, 0, 0))
    in_specs = [
        x_spec,
        pl.BlockSpec((1, HALO, d), lambda bi, i: (bi, jnp.maximum(i * halo_per_tile - 1, 0), 0)),
        pl.BlockSpec((1, HALO, d), lambda bi, i: (bi, jnp.minimum((i + 1) * halo_per_tile, n_halo_blocks - 1), 0)),
        _mod_spec(li, ctx_row)(mod),
    ] + const_specs + [
        pl.BlockSpec((1, tile // GLA_CHUNK, GLA_DV, GLA_QK), lambda bi, i: (bi, i, 0, 0)),
        st_spec,
    ]
    kern = functools.partial(_mixer_kernel, tile=tile, seq_len=l, alpha=alpha)
    return pl.pallas_call(
        kern,
        grid=(b, n_tiles),
        in_specs=in_specs,
        out_specs=[x_spec, st_spec],
        out_shape=[jax.ShapeDtypeStruct(x.shape, _F32),
                   jax.ShapeDtypeStruct((b, GLA_V, GLA_QK), _F32)],
        scratch_shapes=[
            pltpu.VMEM((tile, IN_COLS), _F32),
            pltpu.VMEM((tile + 2 * HALO, GROUP_W), _F32),
            pltpu.VMEM((tile + 2 * HALO, GROUP_W), _F32),
            pltpu.VMEM((tile, 2 * GLA_QK), _F32),
            pltpu.VMEM((tile, GLA_V), _F32),
            pltpu.VMEM((tile, 4 * GROUP_W), _BF16),
            pltpu.VMEM((GLA_V, GLA_QK), _F32),
        ],
        compiler_params=pltpu.CompilerParams(
            dimension_semantics=("arbitrary", "arbitrary"),
            vmem_limit_bytes=VMEM_LIMIT),
        name="token_mixer",
    )(x, x, x, mod, *consts, sb_start, sf0)


def _prep_mixer_weights(w_in, gla_w_gate, gla_b_gate, gla_norm_w, pool_w, pool_scale, sgu_w, sgu_b,
                        sgu_ln_w, sgu_ln_b, cm_conv_w, cm_conv_b, cm_ln_w, cm_ln_b, w_out,
                        post_w, post_b):
    depth, d, _ = w_in.shape
    n_pool = pool_w.shape[1]
    qkvr = w_in[:, :, 0:768]
    low = w_in[:, :, 768:800]
    zb = w_in[:, :, 800:1056]
    zc = w_in[:, :, 1056:1568]
    zd = w_in[:, :, 1568:2080]
    pad = jnp.zeros((depth, d, LANES - 2 * GLA_RANK), _F32)
    w_main = jnp.concatenate([qkvr, low, pad, zc, zb, zd], axis=2).astype(_BF16)
    kv = w_in[:, :, 128:512]
    pad1 = jnp.zeros((depth, d, LANES - GLA_RANK), _F32)
    w_state = jnp.stack(
        [jnp.concatenate([kv, low[:, :, di * GLA_RANK:(di + 1) * GLA_RANK], pad1], axis=2) for di in range(2)],
        axis=1).astype(_BF16)
    wg1 = jnp.pad(gla_w_gate, ((0, 0), (0, 0), (0, LANES - GLA_RANK), (0, 0))).astype(_BF16)
    zero = jnp.zeros((depth, GLA_RANK, GLA_QK), _F32)
    wg2 = jnp.concatenate([
        jnp.concatenate([gla_w_gate[:, 0], zero], axis=2),
        jnp.concatenate([zero, gla_w_gate[:, 1]], axis=2),
        jnp.zeros((depth, LANES - 2 * GLA_RANK, 2 * GLA_QK), _F32)], axis=1).astype(_BF16)
    eye = jnp.eye(n_pool, dtype=_F32)
    pool_blk = (pool_w[:, :, :, None, :] * eye[None, :, None, :, None]).reshape(depth, GROUP_W, GROUP_W)
    row = lambda a: a[:, None, :]
    return {
        "w_in": w_main, "w_state": w_state, "wg1": wg1, "bg1": gla_b_gate[:, :, None, :],
        "wg2": wg2, "bg2": gla_b_gate.reshape(depth, 1, 2 * GLA_QK),
        "gnw": row(gla_norm_w), "pool_blk": pool_blk.astype(_BF16), "pool_scale": row(pool_scale),
        "sgu_w": sgu_w.transpose(0, 2, 1, 3).reshape(depth, MIX_BLOCK, SGU_HEADS * MIX_BLOCK).astype(_BF16),
        "sgu_b": jnp.repeat(sgu_b.transpose(0, 2, 1), SGU_HD, axis=2),
        "sgu_ln_w": row(sgu_ln_w), "sgu_ln_b": row(sgu_ln_b),
        "cm_w": jnp.pad(cm_conv_w, ((0, 0), (0, 1), (0, 0))),
        "cm_b": row(cm_conv_b), "cm_ln_w": row(cm_ln_w), "cm_ln_b": row(cm_ln_b),
        "w_out": w_out.astype(_BF16),
        "post_w": post_w[:, :, None, :], "post_b": post_b[:, :, None, :],
    }


def _mod_kernel(c_ref, w_ref, b_ref, o_ref):
    cv = c_ref[...]

    def split(a):
        hi = a.astype(_BF16)
        return hi, (a - hi.astype(_F32)).astype(_BF16)

    a_hi, a_lo = split(cv * _sigmoid(cv))
    w_hi, w_lo = split(w_ref[0])
    dot = functools.partial(jnp.dot, preferred_element_type=_F32)
    o_ref[0] = (dot(a_hi, w_lo) + dot(a_lo, w_hi)) + dot(a_hi, w_hi) + b_ref[0]


def _mod_call(cond, w_mod, b_mod):
    depth, d, six_d = w_mod.shape
    rows = cond.shape[0]
    return pl.pallas_call(
        _mod_kernel,
        grid=(depth, six_d // d),
        in_specs=[
            pl.BlockSpec((rows, d), lambda li, j: (0, 0)),
            pl.BlockSpec((1, d, d), lambda li, j: (li, 0, j)),
            pl.BlockSpec((1, 1, d), lambda li, j: (li, 0, j)),
        ],
        out_specs=pl.BlockSpec((1, rows, d), lambda li, j: (li, 0, j)),
        out_shape=jax.ShapeDtypeStruct((depth, rows, six_d), _F32),
        compiler_params=pltpu.CompilerParams(
            dimension_semantics=("arbitrary", "arbitrary"),
            vmem_limit_bytes=VMEM_LIMIT),
        name="adaln_modulation",
    )(cond, w_mod, b_mod.reshape(depth, 1, six_d))


def _pick_tile(length, target):
    return min(length, target)


def kernel(x, c, ctx, c_ctx, w_mod, b_mod, w_in, gla_w_gate, gla_b_gate, gla_norm_w, pool_w, pool_scale, sgu_w, sgu_b, sgu_ln_w, sgu_ln_b, cm_conv_w, cm_conv_b, cm_ln_w, cm_ln_b, w_out, ffn_w_up, ffn_conv_w, ffn_w_down, post_ln_w, post_ln_b):
    batch, seq, d = x.shape
    ctx_len = ctx.shape[1]
    depth = w_mod.shape[0]
    alpha = (2 * depth) ** 0.25

    cond_rows = -(-(batch + 1) // SUBLANES) * SUBLANES
    cond = jnp.concatenate([c, c_ctx[None, :], jnp.zeros((cond_rows - batch - 1, d), _F32)], axis=0)
    mod = _mod_call(cond, w_mod, b_mod).reshape(depth, cond_rows, 6, d)

    lat_mix_tile = _pick_tile(seq, MIX_TILE)
    ctx_mix_tile = _pick_tile(ctx_len, MIX_TILE)
    lat_ffn_tile = _pick_tile(seq, FFN_TILE)
    zero_state = jnp.zeros((batch, GLA_V, GLA_QK), _F32)

    mw = _prep_mixer_weights(w_in, gla_w_gate, gla_b_gate, gla_norm_w, pool_w, pool_scale, sgu_w, sgu_b,
                             sgu_ln_w, sgu_ln_b, cm_conv_w, cm_conv_b, cm_ln_w, cm_ln_b, w_out,
                             post_ln_w, post_ln_b)
    fw = _prep_ffn_weights(ffn_w_up, ffn_conv_w, ffn_w_down) + (mw["post_w"], mw["post_b"])
    lat = dict(ctx_row=None)
    cx = dict(ctx_row=batch)

    for li in range(depth):
        with_ctx_out = li < depth - 1
        ctx_sb, ctx_sb_fin = _state_call(ctx, mod, mw, zero_state, li=li, tile=ctx_mix_tile,
                                         reverse=True, **cx)
        if with_ctx_out:
            ctx_mixed, ctx_sf_fin = _mixer_call(ctx, mod, mw, ctx_sb, zero_state, li=li,
                                                tile=ctx_mix_tile, alpha=alpha, **cx)
        else:
            _, ctx_sf_fin = _state_call(ctx, mod, mw, zero_state, li=li, tile=ctx_mix_tile,
                                        reverse=False, **cx)

        lat_sb, _ = _state_call(x, mod, mw, ctx_sb_fin, li=li, tile=lat_mix_tile, reverse=True, **lat)
        x, _ = _mixer_call(x, mod, mw, lat_sb, ctx_sf_fin, li=li, tile=lat_mix_tile, alpha=alpha, **lat)
        x = _ffn_call(x, mod, fw, li=li, tile=lat_ffn_tile, gw=GRID_W, alpha=alpha, **lat)
        if with_ctx_out:
            ctx = _ffn_call(ctx_mixed, mod, fw, li=li, tile=ctx_len, gw=ctx_len, alpha=alpha, **cx)
    return x
```

```python
import functools

import jax
import jax.numpy as jnp
from jax import lax
from jax.experimental import pallas as pl
from jax.experimental.pallas import tpu as pltpu

EPS = 1e-6
LANES = 128
SUBLANES = 8
GRID_W = 64
FFN_CHUNK = 256
FFN_GRANULE = 512
BF16_ROWS = 16
MIX_TILE = 512
FFN_TILE = 1024
VMEM_LIMIT = 60 * 1024 * 1024

_F32 = jnp.float32
_BF16 = jnp.bfloat16


def _norm_rows(x):
    mu = jnp.mean(x, axis=-1, keepdims=True)
    xc = x - mu
    var = jnp.mean(xc * xc, axis=-1, keepdims=True)
    return xc * lax.rsqrt(var + EPS)


def _sigmoid(x):
    return 1.0 / (1.0 + jnp.exp(-x))


def _pick_spec(arr, *lead):
    rest = arr.shape[len(lead):]
    index = tuple(lead) + (0,) * len(rest)
    return pl.BlockSpec((None,) * len(lead) + rest, lambda *_: index, pipeline_mode=pl.Buffered(1))


def _mod_spec(li, ctx_row):
    if ctx_row is None:
        return lambda mod: pl.BlockSpec((None, 1) + mod.shape[2:], lambda bi, i: (li, bi, 0, 0))
    return lambda mod: pl.BlockSpec((None, 1) + mod.shape[2:], lambda bi, i: (li, ctx_row, 0, 0))


def _ffn_kernel(*refs, tile, gw, n_tiles, n_gran, has_halo, alpha):
    if has_halo:
        (xm_ref, xp_ref, xn_ref, mod_ref, wup_ref, wcv_ref, wdn_ref, lnw_ref, lnb_ref,
         o_ref, h_scr, u0_scr, u1_scr, s_scr, acc_scr) = refs
    else:
        (xm_ref, mod_ref, wup_ref, wcv_ref, wdn_ref, lnw_ref, lnb_ref,
         o_ref, h_scr, u0_scr, u1_scr, s_scr, acc_scr) = refs
    u_bufs = (u0_scr, u1_scr)
    i = pl.program_id(1)
    n_chunks = wdn_ref.shape[0]
    pad = gw if has_halo else 0
    n_rows = tile // gw

    shift = mod_ref[0, 3:4, :]
    scale1 = 1.0 + mod_ref[0, 4:5, :]
    gate = mod_ref[0, 5:6, :]

    def modulated(xv):
        return _norm_rows(xv) * scale1 + shift

    ext = h_scr.shape[0]
    up_rows = ext // n_gran
    rows_per_gran = n_rows // n_gran
    gran_rows = rows_per_gran * gw

    def modulate_granule(g):
        rows = slice(g * gran_rows, (g + 1) * gran_rows)
        h_scr[pad + g * gran_rows:pad + (g + 1) * gran_rows, :] = modulated(xm_ref[0, rows, :]).astype(_BF16)
        if has_halo and g == 0:
            hp = jnp.where(i > 0, modulated(xp_ref[0]), 0.0)
            h_scr[0:gw, :] = hp.astype(_BF16)
        if has_halo and g == n_gran - 1:
            hn = jnp.where(i < n_tiles - 1, modulated(xn_ref[0]), 0.0)
            h_scr[pad + tile:pad + tile + gw, :] = hn.astype(_BF16)

    hidden = n_chunks * FFN_CHUNK

    def chunk_cols(c, half):
        return pl.ds(pl.multiple_of(half * hidden + c * FFN_CHUNK, FFN_CHUNK), FFN_CHUNK)

    def up_proj(c, buf, g):
        rows = pl.ds(pl.multiple_of(g * up_rows, BF16_ROWS), up_rows)
        for half in range(2):
            u = jnp.dot(h_scr[rows, :], wup_ref[:, chunk_cols(c, half)], preferred_element_type=_F32)
            for j in range(FFN_CHUNK // LANES):
                u_bufs[buf][half * (FFN_CHUNK // LANES) + j, rows, :] = u[:, j * LANES:(j + 1) * LANES]

    edge = jnp.zeros((1, LANES), _F32)

    def shift_tokens(v, down):
        if down:
            return jnp.concatenate([edge, v[:-1]], axis=0)
        return jnp.concatenate([v[1:], edge], axis=0)

    def conv_cols(ub, wc, base, half, lo):
        ucol = (half * FFN_CHUNK + lo) // LANES

        def tap(k, off):
            return wc[k:k + 1, lo:lo + LANES] * ub[ucol, pl.ds(base + off, gw), :]

        def col_taps(dc):
            k = dc + 1
            v = tap(3 + k, pad)
            if has_halo:
                v = v + tap(k, 0) + tap(6 + k, 2 * gw)
            return v
        return col_taps(0) + shift_tokens(col_taps(-1), True) + shift_tokens(col_taps(1), False)

    def conv_gate(c, buf, g):
        ub = u_bufs[buf]
        wc_a = wcv_ref[:, chunk_cols(c, 0)]
        wc_g = wcv_ref[:, chunk_cols(c, 1)]
        for r in range(rows_per_gran):
            base = pl.multiple_of(g * gran_rows + r * gw, gw)
            for j in range(FFN_CHUNK // LANES):
                a = conv_cols(ub, wc_a, base, 0, j * LANES)
                gt = conv_cols(ub, wc_g, base, 1, j * LANES)
                half_g = 0.5 * gt
                p = a * half_g
                s_scr[pl.ds(base, gw), j * LANES:(j + 1) * LANES] = (p + p * jnp.tanh(half_g)).astype(_BF16)

    def down_proj(c, g):
        rows = pl.ds(pl.multiple_of(g * gran_rows, gran_rows), gran_rows)
        f = jnp.dot(s_scr[rows, :], wdn_ref[c], preferred_element_type=_F32)
        for j in range(acc_scr.shape[0]):
            acc_scr[j, rows, :] += f[:, j * LANES:(j + 1) * LANES]

    def chunk_steps(c, buf, has_next):
        def step(g, carry):
            if has_next:
                up_proj(c + 1, 1 - buf, g)
            wrap = jnp.where(g == 0, 1, 0)
            down_proj(jnp.maximum(c - wrap, 0), g - 1 + wrap * n_gran)
            conv_gate(c, buf, g)
            return carry
        lax.fori_loop(0, n_gran, step, 0)

    acc_scr[...] = jnp.zeros_like(acc_scr)
    s_scr[...] = jnp.zeros_like(s_scr)
    done = 0
    for g in range(n_gran):
        need = min(n_gran, -(-((g + 1) * up_rows - pad) // gran_rows))
        for gm in range(done, need):
            modulate_granule(gm)
        done = max(done, need)
        up_proj(0, 0, g)

    def body(k, carry):
        c = 2 * k
        chunk_steps(c, 0, True)
        chunk_steps(c + 1, 1, True)
        return carry

    n_pairs = (n_chunks - 1) // 2
    lax.fori_loop(0, n_pairs, body, 0)
    if n_chunks % 2 == 0:
        chunk_steps(n_chunks - 2, 0, True)
    chunk_steps(n_chunks - 1, (n_chunks - 1) % 2, False)
    down_proj(n_chunks - 1, n_gran - 1)

    for g in range(n_gran):
        rows = slice(g * gran_rows, (g + 1) * gran_rows)
        f = jnp.concatenate([acc_scr[j, rows, :] for j in range(acc_scr.shape[0])], axis=1)
        y = alpha * xm_ref[0, rows, :] + gate * f
        o_ref[0, rows, :] = _norm_rows(y) * lnw_ref[...] + lnb_ref[...]


def _ffn_call(x, mod, fw, *, li, ctx_row, tile, gw, alpha):
    wup, wcv, wdn, post_w, post_b = fw
    b, l, d = x.shape
    n_tiles = l // tile
    has_halo = l > gw
    rows_per_tile = tile // gw
    n_grid_rows = l // gw
    ext = tile + (2 * gw if has_halo else 0)

    x_spec = pl.BlockSpec((1, tile, d), lambda bi, i: (bi, i, 0))
    in_specs = [x_spec]
    args = [x]
    if has_halo:
        in_specs.append(pl.BlockSpec(
            (1, gw, d), lambda bi, i: (bi, jnp.maximum(i * rows_per_tile - 1, 0), 0)))
        in_specs.append(pl.BlockSpec(
            (1, gw, d), lambda bi, i: (bi, jnp.minimum((i + 1) * rows_per_tile, n_grid_rows - 1), 0)))
        args += [x, x]
    in_specs += [
        _mod_spec(li, ctx_row)(mod),
        _pick_spec(wup, li), _pick_spec(wcv, li), _pick_spec(wdn, li),
        _pick_spec(post_w, li, 1), _pick_spec(post_b, li, 1),
    ]
    args += [mod, wup, wcv, wdn, post_w, post_b]
    n_gran = max(1, tile // FFN_GRANULE)
    assert ext % (n_gran * BF16_ROWS) == 0 and rows_per_tile % n_gran == 0
    kern = functools.partial(_ffn_kernel, tile=tile, gw=gw, n_tiles=n_tiles, n_gran=n_gran,
                             has_halo=has_halo, alpha=alpha)
    return pl.pallas_call(
        kern,
        grid=(b, n_tiles),
        in_specs=in_specs,
        out_specs=x_spec,
        out_shape=jax.ShapeDtypeStruct(x.shape, _F32),
        scratch_shapes=[
            pltpu.VMEM((ext, d), _BF16),
            pltpu.VMEM((2 * FFN_CHUNK // LANES, ext, LANES), _F32),
            pltpu.VMEM((2 * FFN_CHUNK // LANES, ext, LANES), _F32),
            pltpu.VMEM((tile, FFN_CHUNK), _BF16),
            pltpu.VMEM((d // LANES, tile, LANES), _F32),
        ],
        compiler_params=pltpu.CompilerParams(
            dimension_semantics=("arbitrary", "arbitrary"),
            vmem_limit_bytes=VMEM_LIMIT),
        name="channel_mixer",
    )(*args)


def _prep_ffn_weights(w_up, conv_w, w_down):
    depth, d, two_h = w_up.shape
    n_chunks = two_h // 2 // FFN_CHUNK
    wup = w_up.astype(_BF16)
    wcv = conv_w.reshape(depth, 9, two_h)
    wdn = w_down.reshape(depth, n_chunks, FFN_CHUNK, w_down.shape[-1]).astype(_BF16)
    return wup, wcv, wdn


GLA_HEADS = 4
GLA_DK = 32
GLA_DV = 64
GLA_QK = GLA_HEADS * GLA_DK
GLA_V = GLA_HEADS * GLA_DV
GLA_RANK = 16
GLA_CHUNK = 64
GLA_GATE_NORMALIZER = 16.0
GLA_LOG_DECAY_MIN = -1.0


def _log_decay(logit):
    log_sig = jnp.minimum(logit, 0.0) - jnp.log(1.0 + jnp.exp(-jnp.abs(logit)))
    return jnp.maximum(log_sig / GLA_GATE_NORMALIZER, GLA_LOG_DECAY_MIN)


def _chunk_cumsum(x, reverse):
    n = x.shape[0]
    pos = lax.broadcasted_iota(jnp.int32, x.shape, 0) % GLA_CHUNK
    s = 1
    while s < GLA_CHUNK:
        if reverse:
            x = x + jnp.where(pos < GLA_CHUNK - s, pltpu.roll(x, n - s, axis=0), 0.0)
        else:
            x = x + jnp.where(pos >= s, pltpu.roll(x, s, axis=0), 0.0)
        s *= 2
    return x


def _state_mask():
    r = lax.broadcasted_iota(jnp.int32, (GLA_V, GLA_QK), 0) // GLA_DV
    c = lax.broadcasted_iota(jnp.int32, (GLA_V, GLA_QK), 1) // GLA_DK
    return r == c


def _stack4(a):
    return jnp.concatenate([a, a, a, a], axis=0)


def _compact_state(s):
    return (s[0:GLA_DV] + s[GLA_DV:2 * GLA_DV]) + (s[2 * GLA_DV:3 * GLA_DV] + s[3 * GLA_DV:4 * GLA_DV])


def _state_update(state, v_bf, k_dec_bf, decay_row, mask):
    kv_t = lax.dot_general(v_bf, k_dec_bf, (((0,), (0,)), ((), ())), preferred_element_type=_F32)
    return state * decay_row + jnp.where(mask, kv_t, 0.0)


def _state_kernel(x_ref, mod_ref, w_ref, wg_ref, bg_ref, s0_ref, start_ref, fin_ref, st_scr,
                  *, tile, reverse):
    i = pl.program_id(1)
    n_tiles = pl.num_programs(1)
    chunks = tile // GLA_CHUNK

    @pl.when(i == 0)
    def _():
        st_scr[...] = s0_ref[0]

    shift = mod_ref[0, 0:1, :]
    scale1 = 1.0 + mod_ref[0, 1:2, :]
    blocks = []
    for r0 in range(0, tile, MIX_BLOCK):
        h = (_norm_rows(x_ref[0, r0:r0 + MIX_BLOCK, :]) * scale1 + shift).astype(_BF16)
        blocks.append(jnp.dot(h, w_ref[...], preferred_element_type=_F32))
    z = jnp.concatenate(blocks, axis=0)
    k = z[:, 0:GLA_QK]
    v_bf = z[:, GLA_QK:GLA_QK + GLA_V].astype(_BF16)
    low = z[:, GLA_QK + GLA_V:].astype(_BF16)
    g = _log_decay(jnp.dot(low, wg_ref[...], preferred_element_type=_F32) + bg_ref[...])
    run = _chunk_cumsum(g, reverse)
    mask = _state_mask()
    state = st_scr[...]
    order = range(chunks - 1, -1, -1) if reverse else range(chunks)
    for ci in order:
        rows = slice(ci * GLA_CHUNK, (ci + 1) * GLA_CHUNK)
        edge = ci * GLA_CHUNK if reverse else (ci + 1) * GLA_CHUNK - 1
        total = run[edge:edge + 1, :]
        k_dec = (k[rows] * jnp.exp(total - run[rows])).astype(_BF16)
        start_ref[0, ci] = _compact_state(state)
        state = _state_update(state, v_bf[rows], k_dec, jnp.exp(total), mask)
    st_scr[...] = state

    @pl.when(i == n_tiles - 1)
    def _():
        fin_ref[0] = state


def _state_call(x, mod, mw, s0, *, li, ctx_row, tile, reverse):
    di = 1 if reverse else 0
    w, wg, bg = mw["w_state"], mw["wg1"], mw["bg1"]
    b, l, d = x.shape
    n_tiles = l // tile
    chunks = tile // GLA_CHUNK
    tmap = (lambda bi, i: (bi, n_tiles - 1 - i, 0)) if reverse else (lambda bi, i: (bi, i, 0))
    smap = (lambda bi, i: (bi, n_tiles - 1 - i, 0, 0)) if reverse else (lambda bi, i: (bi, i, 0, 0))
    st_spec = pl.BlockSpec((1, GLA_V, GLA_QK), lambda bi, i: (bi, 0, 0))
    kern = functools.partial(_state_kernel, tile=tile, reverse=reverse)
    return pl.pallas_call(
        kern,
        grid=(b, n_tiles),
        in_specs=[
            pl.BlockSpec((1, tile, d), tmap),
            _mod_spec(li, ctx_row)(mod),
            _pick_spec(w, li, di), _pick_spec(wg, li, di), _pick_spec(bg, li, di),
            st_spec,
        ],
        out_specs=[
            pl.BlockSpec((1, chunks, GLA_DV, GLA_QK), smap),
            st_spec,
        ],
        out_shape=[
            jax.ShapeDtypeStruct((b, l // GLA_CHUNK, GLA_DV, GLA_QK), _F32),
            jax.ShapeDtypeStruct((b, GLA_V, GLA_QK), _F32),
        ],
        scratch_shapes=[pltpu.VMEM((GLA_V, GLA_QK), _F32)],
        compiler_params=pltpu.CompilerParams(
            dimension_semantics=("arbitrary", "arbitrary"),
            vmem_limit_bytes=VMEM_LIMIT),
        name="gla_state_rev" if reverse else "gla_state_fwd",
    )(x, mod, w, wg, bg, s0)


GROUP_W = 256
COL_Q, COL_K, COL_V, COL_R, COL_LOW, COL_C, COL_B, COL_D = 0, 128, 256, 512, 768, 896, 1408, 1664
IN_COLS = 2176
HALO = 16
POOL_REACH = 8
MIX_BLOCK = 128
CONV_WIDTH = 31
SGU_HEADS = 4
SGU_HD = 64


def _gelu_tanh(x):
    return 0.5 * x * (1.0 + jnp.tanh(0.7978845608028654 * (x + 0.044715 * (x * x * x))))


def _mixer_kernel(xm_ref, xp_ref, xn_ref, mod_ref, win_ref, wg_ref, bg_ref, gnw_ref,
                  pblk_ref, pscale_ref, sguw_ref, sgub_ref, slnw_ref, slnb_ref,
                  cw_ref, cb_ref, clnw_ref, clnb_ref, wout_ref, plw_ref, plb_ref,
                  sb_ref, sf0_ref,
                  o_ref, sfin_ref,
                  z_scr, zb_scr, y_scr, run_scr, o_scr, cat_scr, st_scr,
                  *, tile, seq_len, alpha):
    i = pl.program_id(1)
    n_tiles = pl.num_programs(1)
    n_blocks = tile // MIX_BLOCK

    @pl.when(i == 0)
    def _():
        st_scr[...] = sf0_ref[0]

    shift = mod_ref[0, 0:1, :]
    scale1 = 1.0 + mod_ref[0, 1:2, :]
    gate = mod_ref[0, 2:3, :]

    def modulated(xv):
        return _norm_rows(xv) * scale1 + shift

    hp = jnp.where(i > 0, modulated(xp_ref[0]), 0.0).astype(_BF16)
    hn = jnp.where(i < n_tiles - 1, modulated(xn_ref[0]), 0.0).astype(_BF16)
    zh = jnp.dot(jnp.concatenate([hp, hn], axis=0), win_ref[:, COL_B:],
                 preferred_element_type=_F32)

    def glu(zd):
        return zd[:, :GROUP_W] * _sigmoid(zd[:, GROUP_W:])

    zb_scr[0:HALO, :] = zh[0:HALO, 0:GROUP_W]
    zb_scr[HALO + tile:, :] = zh[HALO:, 0:GROUP_W]
    def stage_conv_input(rows, zd):
        y = glu(zd)
        for j in range(GROUP_W // LANES):
            y_scr[j, rows, :] = y[:, j * LANES:(j + 1) * LANES]

    stage_conv_input(slice(0, HALO), zh[0:HALO, GROUP_W:])
    stage_conv_input(slice(HALO + tile, 2 * HALO + tile), zh[HALO:, GROUP_W:])

    def span_rows(start, size, align=SUBLANES):
        if isinstance(start, int):
            return slice(start, start + size)
        return pl.ds(pl.multiple_of(start, align), size)

    def block_rows(blk):
        return span_rows(blk * MIX_BLOCK, MIX_BLOCK, MIX_BLOCK)

    def in_proj(blk):
        rows = block_rows(blk)
        ext_rows = span_rows(HALO + blk * MIX_BLOCK, MIX_BLOCK, HALO)
        h = modulated(xm_ref[0, rows, :]).astype(_BF16)
        z = jnp.dot(h, win_ref[...], preferred_element_type=_F32)
        z_scr[rows, :] = z
        zb_scr[ext_rows, :] = z[:, COL_B:COL_B + GROUP_W]
        stage_conv_input(ext_rows, z[:, COL_D:COL_D + 2 * GROUP_W])
        low = z[:, COL_LOW:COL_LOW + LANES].astype(_BF16)
        g = _log_decay(jnp.dot(low, wg_ref[...], preferred_element_type=_F32) + bg_ref[...])
        run_scr[rows, 0:GLA_QK] = _chunk_cumsum(g[:, 0:GLA_QK], False)
        run_scr[rows, GLA_QK:] = _chunk_cumsum(g[:, GLA_QK:], True)

    lane = lax.broadcasted_iota(jnp.int32, (MIX_BLOCK, GROUP_W), 1)
    row = lax.broadcasted_iota(jnp.int32, (MIX_BLOCK, GROUP_W), 0)
    lane_group = lane // SGU_HD
    pool_half = jnp.left_shift(1, lane_group)

    def local_mixers(blk):
        r0 = blk * MIX_BLOCK
        rows = block_rows(blk)

        n = MIX_BLOCK + 2 * POOL_REACH
        xb = zb_scr[span_rows(r0 + HALO - POOL_REACH, n), :]
        p2 = xb + pltpu.roll(xb, 1, axis=0)
        p4 = pltpu.roll(p2, 1, axis=0) + pltpu.roll(p2, n - 1, axis=0)
        p8 = pltpu.roll(p4, 2, axis=0) + pltpu.roll(p4, n - 2, axis=0)
        p16 = pltpu.roll(p8, 4, axis=0) + pltpu.roll(p8, n - 4, axis=0)
        inner = slice(POOL_REACH, POOL_REACH + MIX_BLOCK)
        win = jnp.where(lane_group == 0, p2[inner],
                        jnp.where(lane_group == 1, p4[inner],
                                  jnp.where(lane_group == 2, p8[inner], p16[inner])))
        tok = i * tile + r0 + row
        count = jnp.minimum(tok + pool_half, seq_len) - jnp.maximum(tok - pool_half, 0)
        pooled = win / count.astype(_F32) - xb[inner]
        yb = jnp.dot(pooled.astype(_BF16), pblk_ref[...], preferred_element_type=_F32) * pscale_ref[...]
        cat_scr[rows, GROUP_W:2 * GROUP_W] = yb.astype(_BF16)

        gz = _gelu_tanh(z_scr[rows, COL_C:COL_C + 2 * GROUP_W])
        vn = _norm_rows(gz[:, GROUP_W:]) * slnw_ref[...] + slnb_ref[...]
        vstack = jnp.concatenate(
            [jnp.where(lane_group == hd, vn, 0.0) for hd in range(SGU_HEADS)], axis=0).astype(_BF16)
        sg = jnp.dot(sguw_ref[...], vstack, preferred_element_type=_F32) + sgub_ref[...]
        cat_scr[rows, 2 * GROUP_W:3 * GROUP_W] = (gz[:, :GROUP_W] * sg).astype(_BF16)

        span = MIX_BLOCK + SUBLANES
        halves = []
        for lo in range(0, GROUP_W, LANES):
            conv = None
            for b in range(SUBLANES):
                part = None
                for a in range(4):
                    o = SUBLANES * a + b
                    if 1 <= o <= CONV_WIDTH:
                        term = (cw_ref[o - 1:o, lo:lo + LANES]
                                * y_scr[lo // LANES, span_rows(r0 + SUBLANES * a, span), :])
                        part = term if part is None else part + term
                part = part[b:b + MIX_BLOCK]
                conv = part if conv is None else conv + part
            halves.append(conv)
        conv = jnp.concatenate(halves, axis=1)
        yd = _norm_rows(conv + cb_ref[...]) * clnw_ref[...] + clnb_ref[...]
        cat_scr[rows, 3 * GROUP_W:4 * GROUP_W] = (yd * _sigmoid(yd)).astype(_BF16)

    mask_s = _state_mask()
    vr = lax.broadcasted_iota(jnp.int32, (GLA_V, GLA_V), 0) // GLA_DV
    vc = lax.broadcasted_iota(jnp.int32, (GLA_V, GLA_V), 1) // GLA_DV
    mask_v = vr == vc
    t_id = lax.broadcasted_iota(jnp.int32, (GLA_CHUNK, GLA_V), 0)
    s_id = lax.broadcasted_iota(jnp.int32, (GLA_CHUNK, GLA_V), 1) % GLA_CHUNK
    nt_dims = (((1,), (1,)), ((), ()))

    zero = jnp.zeros((), _BF16)

    def chunk_rows(ci):
        return span_rows(ci * GLA_CHUNK, GLA_CHUNK, GLA_CHUNK)

    def scores(ci):
        r = chunk_rows(ci)
        q = z_scr[r, COL_Q:COL_Q + GLA_QK] * (GLA_DK ** -0.5)
        k = z_scr[r, COL_K:COL_K + GLA_QK]
        cf = run_scr[r, 0:GLA_QK]
        rb = run_scr[r, GLA_QK:]
        qf = (q * jnp.exp(cf)).astype(_BF16)
        kf = (k * jnp.exp(-cf)).astype(_BF16)
        qb = (q * jnp.exp(rb)).astype(_BF16)
        kb = (k * jnp.exp(-rb)).astype(_BF16)
        a_f = lax.dot_general(qf, jnp.where(mask_s, _stack4(kf), zero), nt_dims,
                              preferred_element_type=_F32)
        a_b = lax.dot_general(qb, jnp.where(mask_s, _stack4(kb), zero), nt_dims,
                              preferred_element_type=_F32)
        a = (jnp.where(s_id <= t_id, a_f, 0.0) + jnp.where(s_id >= t_id, a_b, 0.0)).astype(_BF16)
        return a, qf, qb

    def outputs(ci, a, qf, qb):
        r = chunk_rows(ci)
        k = z_scr[r, COL_K:COL_K + GLA_QK]
        v = z_scr[r, COL_V:COL_V + GLA_V].astype(_BF16)
        cf = run_scr[r, 0:GLA_QK]
        total = cf[GLA_CHUNK - 1:GLA_CHUNK, :]
        k_dec = (k * jnp.exp(total - cf)).astype(_BF16)
        o = jnp.dot(a, jnp.where(mask_v, _stack4(v), zero), preferred_element_type=_F32)
        sf = st_scr[...]
        sb = jnp.where(mask_s, _stack4(sb_ref[0, ci]), 0.0)
        o = o + lax.dot_general(qf, sf.astype(_BF16), nt_dims, preferred_element_type=_F32)
        o = o + lax.dot_general(qb, sb.astype(_BF16), nt_dims, preferred_element_type=_F32)
        o_scr[r, :] = o
        st_scr[...] = _state_update(sf, v, k_dec, jnp.exp(total), mask_s)

    head_mean = jnp.where(mask_v, 1.0 / GLA_DV, 0.0).astype(_BF16)

    def readout(blk):
        rows = block_rows(blk)
        o = o_scr[rows, :]
        sq = o * o
        sq_hi = sq.astype(_BF16)
        sq_lo = (sq - sq_hi.astype(_F32)).astype(_BF16)
        ms = (jnp.dot(sq_hi, head_mean, preferred_element_type=_F32)
              + jnp.dot(sq_lo, head_mean, preferred_element_type=_F32))
        rg = z_scr[rows, COL_R:COL_R + GROUP_W]
        ya = o * lax.rsqrt(ms + EPS) * gnw_ref[...] * (rg * _sigmoid(rg))
        cat_scr[rows, 0:GROUP_W] = ya.astype(_BF16)

    def out_proj(blk):
        rows = block_rows(blk)
        y = jnp.dot(cat_scr[rows, :], wout_ref[...], preferred_element_type=_F32)
        o_ref[0, rows, :] = (_norm_rows(alpha * xm_ref[0, rows, :] + gate * y) * plw_ref[...]
                             + plb_ref[...])

    chunks_per_block = MIX_BLOCK // GLA_CHUNK
    n_gla_chunks = tile // GLA_CHUNK
    def block_step(blk, cur, stage_ahead, finish_prev, last):
        if stage_ahead:
            in_proj(blk + 2)
        if finish_prev:
            out_proj(blk - 1)
        local_mixers(blk)
        for cj in range(chunks_per_block):
            ci = blk * chunks_per_block + cj
            nxt = None if (last and cj == chunks_per_block - 1) else scores(ci + 1)
            outputs(ci, *cur)
            cur = nxt
        readout(blk)
        return cur

    in_proj(0)
    if n_blocks > 1:
        in_proj(1)
    cur = scores(0)
    for blk in range(n_blocks):
        cur = block_step(blk, cur, blk + 2 < n_blocks, blk > 0, blk == n_blocks - 1)
    out_proj(n_blocks - 1)

    @pl.when(i == n_tiles - 1)
    def _():
        sfin_ref[0] = st_scr[...]


def _mixer_call(x, mod, mw, sb_start, sf0, *, li, ctx_row, tile, alpha):
    b, l, d = x.shape
    n_tiles = l // tile
    halo_per_tile = tile // HALO
    n_halo_blocks = l // HALO
    consts = [mw[k] for k in ("w_in", "wg2", "bg2", "gnw", "pool_blk", "pool_scale", "sgu_w", "sgu_b",
                              "sgu_ln_w", "sgu_ln_b", "cm_w", "cm_b", "cm_ln_w", "cm_ln_b", "w_out")]
    const_specs = [_pick_spec(c, li) for c in consts]
    consts += [mw["post_w"], mw["post_b"]]
    const_specs += [_pick_spec(mw["post_w"], li, 0), _pick_spec(mw["post_b"], li, 0)]
    x_spec = pl.BlockSpec((1, tile, d), lambda bi, i: (bi, i, 0))
    st_spec = pl.BlockSpec((1, GLA_V, GLA_QK), lambda bi, i: (bi, 0, 0))
    in_specs = [
        x_spec,
        pl.BlockSpec((1, HALO, d), lambda bi, i: (bi, jnp.maximum(i * halo_per_tile - 1, 0), 0)),
        pl.BlockSpec((1, HALO, d), lambda bi, i: (bi, jnp.minimum((i + 1) * halo_per_tile, n_halo_blocks - 1), 0)),
        _mod_spec(li, ctx_row)(mod),
    ] + const_specs + [
        pl.BlockSpec((1, tile // GLA_CHUNK, GLA_DV, GLA_QK), lambda bi, i: (bi, i, 0, 0)),
        st_spec,
    ]
    kern = functools.partial(_mixer_kernel, tile=tile, seq_len=l, alpha=alpha)
    return pl.pallas_call(
        kern,
        grid=(b, n_tiles),
        in_specs=in_specs,
        out_specs=[x_spec, st_spec],
        out_shape=[jax.ShapeDtypeStruct(x.shape, _F32),
                   jax.ShapeDtypeStruct((b, GLA_V, GLA_QK), _F32)],
        scratch_shapes=[
            pltpu.VMEM((tile, IN_COLS), _F32),
            pltpu.VMEM((tile + 2 * HALO, GROUP_W), _F32),
            pltpu.VMEM((GROUP_W // LANES, tile + 2 * HALO, LANES), _F32),
            pltpu.VMEM((tile, 2 * GLA_QK), _F32),
            pltpu.VMEM((tile, GLA_V), _F32),
            pltpu.VMEM((tile, 4 * GROUP_W), _BF16),
            pltpu.VMEM((GLA_V, GLA_QK), _F32),
        ],
        compiler_params=pltpu.CompilerParams(
            dimension_semantics=("arbitrary", "arbitrary"),
            vmem_limit_bytes=VMEM_LIMIT),
        name="token_mixer",
    )(x, x, x, mod, *consts, sb_start, sf0)


def _prep_mixer_weights(w_in, gla_w_gate, gla_b_gate, gla_norm_w, pool_w, pool_scale, sgu_w, sgu_b,
                        sgu_ln_w, sgu_ln_b, cm_conv_w, cm_conv_b, cm_ln_w, cm_ln_b, w_out,
                        post_w, post_b):
    depth, d, _ = w_in.shape
    n_pool = pool_w.shape[1]
    qkvr = w_in[:, :, 0:768]
    low = w_in[:, :, 768:800]
    zb = w_in[:, :, 800:1056]
    zc = w_in[:, :, 1056:1568]
    zd = w_in[:, :, 1568:2080]
    pad = jnp.zeros((depth, d, LANES - 2 * GLA_RANK), _F32)
    w_main = jnp.concatenate([qkvr, low, pad, zc, zb, zd], axis=2).astype(_BF16)
    kv = w_in[:, :, 128:512]
    pad1 = jnp.zeros((depth, d, LANES - GLA_RANK), _F32)
    w_state = jnp.stack(
        [jnp.concatenate([kv, low[:, :, di * GLA_RANK:(di + 1) * GLA_RANK], pad1], axis=2) for di in range(2)],
        axis=1).astype(_BF16)
    wg1 = jnp.pad(gla_w_gate, ((0, 0), (0, 0), (0, LANES - GLA_RANK), (0, 0))).astype(_BF16)
    zero = jnp.zeros((depth, GLA_RANK, GLA_QK), _F32)
    wg2 = jnp.concatenate([
        jnp.concatenate([gla_w_gate[:, 0], zero], axis=2),
        jnp.concatenate([zero, gla_w_gate[:, 1]], axis=2),
        jnp.zeros((depth, LANES - 2 * GLA_RANK, 2 * GLA_QK), _F32)], axis=1).astype(_BF16)
    eye = jnp.eye(n_pool, dtype=_F32)
    pool_blk = (pool_w[:, :, :, None, :] * eye[None, :, None, :, None]).reshape(depth, GROUP_W, GROUP_W)
    row = lambda a: a[:, None, :]
    return {
        "w_in": w_main, "w_state": w_state, "wg1": wg1, "bg1": gla_b_gate[:, :, None, :],
        "wg2": wg2, "bg2": gla_b_gate.reshape(depth, 1, 2 * GLA_QK),
        "gnw": row(gla_norm_w), "pool_blk": pool_blk.astype(_BF16), "pool_scale": row(pool_scale),
        "sgu_w": sgu_w.transpose(0, 2, 1, 3).reshape(depth, MIX_BLOCK, SGU_HEADS * MIX_BLOCK).astype(_BF16),
        "sgu_b": jnp.repeat(sgu_b.transpose(0, 2, 1), SGU_HD, axis=2),
        "sgu_ln_w": row(sgu_ln_w), "sgu_ln_b": row(sgu_ln_b),
        "cm_w": jnp.pad(cm_conv_w, ((0, 0), (0, 1), (0, 0))),
        "cm_b": row(cm_conv_b), "cm_ln_w": row(cm_ln_w), "cm_ln_b": row(cm_ln_b),
        "w_out": w_out.astype(_BF16),
        "post_w": post_w[:, :, None, :], "post_b": post_b[:, :, None, :],
    }


def _mod_kernel(c_ref, w_ref, b_ref, o_ref):
    cv = c_ref[...]

    def split(a):
        hi = a.astype(_BF16)
        return hi, (a - hi.astype(_F32)).astype(_BF16)

    a_hi, a_lo = split(cv * _sigmoid(cv))
    w_hi, w_lo = split(w_ref[0])
    dot = functools.partial(jnp.dot, preferred_element_type=_F32)
    o_ref[0] = (dot(a_hi, w_lo) + dot(a_lo, w_hi)) + dot(a_hi, w_hi) + b_ref[0]


def _mod_call(cond, w_mod, b_mod):
    depth, d, six_d = w_mod.shape
    rows = cond.shape[0]
    return pl.pallas_call(
        _mod_kernel,
        grid=(depth, six_d // d),
        in_specs=[
            pl.BlockSpec((rows, d), lambda li, j: (0, 0)),
            pl.BlockSpec((1, d, d), lambda li, j: (li, 0, j)),
            pl.BlockSpec((1, 1, d), lambda li, j: (li, 0, j)),
        ],
        out_specs=pl.BlockSpec((1, rows, d), lambda li, j: (li, 0, j)),
        out_shape=jax.ShapeDtypeStruct((depth, rows, six_d), _F32),
        compiler_params=pltpu.CompilerParams(
            dimension_semantics=("arbitrary", "arbitrary"),
            vmem_limit_bytes=VMEM_LIMIT),
        name="adaln_modulation",
    )(cond, w_mod, b_mod.reshape(depth, 1, six_d))


def _pick_tile(length, target):
    return min(length, target)


def kernel(x, c, ctx, c_ctx, w_mod, b_mod, w_in, gla_w_gate, gla_b_gate, gla_norm_w, pool_w, pool_scale, sgu_w, sgu_b, sgu_ln_w, sgu_ln_b, cm_conv_w, cm_conv_b, cm_ln_w, cm_ln_b, w_out, ffn_w_up, ffn_conv_w, ffn_w_down, post_ln_w, post_ln_b):
    batch, seq, d = x.shape
    ctx_len = ctx.shape[1]
    depth = w_mod.shape[0]
    alpha = (2 * depth) ** 0.25

    cond_rows = -(-(batch + 1) // SUBLANES) * SUBLANES
    cond = jnp.concatenate([c, c_ctx[None, :], jnp.zeros((cond_rows - batch - 1, d), _F32)], axis=0)
    mod = _mod_call(cond, w_mod, b_mod).reshape(depth, cond_rows, 6, d)

    lat_mix_tile = _pick_tile(seq, MIX_TILE)
    ctx_mix_tile = _pick_tile(ctx_len, MIX_TILE)
    lat_ffn_tile = _pick_tile(seq, FFN_TILE)
    zero_state = jnp.zeros((batch, GLA_V, GLA_QK), _F32)

    mw = _prep_mixer_weights(w_in, gla_w_gate, gla_b_gate, gla_norm_w, pool_w, pool_scale, sgu_w, sgu_b,
                             sgu_ln_w, sgu_ln_b, cm_conv_w, cm_conv_b, cm_ln_w, cm_ln_b, w_out,
                             post_ln_w, post_ln_b)
    fw = _prep_ffn_weights(ffn_w_up, ffn_conv_w, ffn_w_down) + (mw["post_w"], mw["post_b"])
    lat = dict(ctx_row=None)
    cx = dict(ctx_row=batch)

    for li in range(depth):
        with_ctx_out = li < depth - 1
        ctx_sb, ctx_sb_fin = _state_call(ctx, mod, mw, zero_state, li=li, tile=ctx_mix_tile,
                                         reverse=True, **cx)
        if with_ctx_out:
            ctx_mixed, ctx_sf_fin = _mixer_call(ctx, mod, mw, ctx_sb, zero_state, li=li,
                                                tile=ctx_mix_tile, alpha=alpha, **cx)
        else:
            _, ctx_sf_fin = _state_call(ctx, mod, mw, zero_state, li=li, tile=ctx_mix_tile,
                                        reverse=False, **cx)

        lat_sb, _ = _state_call(x, mod, mw, ctx_sb_fin, li=li, tile=lat_mix_tile, reverse=True, **lat)
        x, _ = _mixer_call(x, mod, mw, lat_sb, ctx_sf_fin, li=li, tile=lat_mix_tile, alpha=alpha, **lat)
        x = _ffn_call(x, mod, fw, li=li, tile=lat_ffn_tile, gw=GRID_W, alpha=alpha, **lat)
        if with_ctx_out:
            ctx = _ffn_call(ctx_mixed, mod, fw, li=li, tile=ctx_len, gw=ctx_len, alpha=alpha, **cx)
    return x
```

```python
import functools

import jax
import jax.numpy as jnp
from jax import lax
from jax.experimental import pallas as pl
from jax.experimental.pallas import tpu as pltpu

EPS = 1e-6
LANES = 128
SUBLANES = 8
GRID_W = 64
FFN_CHUNK = 256
FFN_GRANULE = 512
BF16_ROWS = 16
MIX_TILE = 512
FFN_TILE = 1024
VMEM_LIMIT = 60 * 1024 * 1024
MIXER_VMEM_LIMIT = 40 * 1024 * 1024
STATE_VMEM_LIMIT = 32 * 1024 * 1024

_F32 = jnp.float32
_BF16 = jnp.bfloat16


def _norm_rows(x):
    mu = jnp.mean(x, axis=-1, keepdims=True)
    xc = x - mu
    var = jnp.mean(xc * xc, axis=-1, keepdims=True)
    return xc * lax.rsqrt(var + EPS)


def _sigmoid(x):
    return 1.0 / (1.0 + jnp.exp(-x))


def _pick_spec(arr, *lead):
    rest = arr.shape[len(lead):]
    index = tuple(lead) + (0,) * len(rest)
    return pl.BlockSpec((None,) * len(lead) + rest, lambda *_: index, pipeline_mode=pl.Buffered(1))


def _mod_spec(li, ctx_row):
    if ctx_row is None:
        return lambda mod: pl.BlockSpec((None, 1) + mod.shape[2:], lambda bi, i: (li, bi, 0, 0))
    return lambda mod: pl.BlockSpec((None, 1) + mod.shape[2:], lambda bi, i: (li, ctx_row, 0, 0))


def _ffn_kernel(*refs, tile, gw, n_tiles, n_gran, has_halo, alpha):
    if has_halo:
        (xm_ref, xp_ref, xn_ref, mod_ref, wup_ref, wcv_ref, wdn_ref, lnw_ref, lnb_ref,
         o_ref, h_scr, u0_scr, u1_scr, s_scr, acc_scr) = refs
    else:
        (xm_ref, mod_ref, wup_ref, wcv_ref, wdn_ref, lnw_ref, lnb_ref,
         o_ref, h_scr, u0_scr, u1_scr, s_scr, acc_scr) = refs
    u_bufs = (u0_scr, u1_scr)
    i = pl.program_id(1)
    n_chunks = wdn_ref.shape[0]
    pad = gw if has_halo else 0
    n_rows = tile // gw

    shift = mod_ref[0, 3:4, :]
    scale1 = 1.0 + mod_ref[0, 4:5, :]
    gate = mod_ref[0, 5:6, :]

    def modulated(xv):
        return _norm_rows(xv) * scale1 + shift

    ext = h_scr.shape[0]
    up_rows = ext // n_gran
    rows_per_gran = n_rows // n_gran
    gran_rows = rows_per_gran * gw

    def modulate_granule(g):
        rows = slice(g * gran_rows, (g + 1) * gran_rows)
        h_scr[pad + g * gran_rows:pad + (g + 1) * gran_rows, :] = modulated(xm_ref[0, rows, :]).astype(_BF16)
        if has_halo and g == 0:
            hp = jnp.where(i > 0, modulated(xp_ref[0]), 0.0)
            h_scr[0:gw, :] = hp.astype(_BF16)
        if has_halo and g == n_gran - 1:
            hn = jnp.where(i < n_tiles - 1, modulated(xn_ref[0]), 0.0)
            h_scr[pad + tile:pad + tile + gw, :] = hn.astype(_BF16)

    hidden = n_chunks * FFN_CHUNK

    def chunk_cols(c, half):
        return pl.ds(pl.multiple_of(half * hidden + c * FFN_CHUNK, FFN_CHUNK), FFN_CHUNK)

    def up_proj(c, buf, g):
        rows = pl.ds(pl.multiple_of(g * up_rows, BF16_ROWS), up_rows)
        for half in range(2):
            u = jnp.dot(h_scr[rows, :], wup_ref[:, chunk_cols(c, half)], preferred_element_type=_F32)
            for j in range(FFN_CHUNK // LANES):
                u_bufs[buf][half * (FFN_CHUNK // LANES) + j, rows, :] = u[:, j * LANES:(j + 1) * LANES]

    edge = jnp.zeros((1, LANES), _F32)

    def shift_tokens(v, down):
        if down:
            return jnp.concatenate([edge, v[:-1]], axis=0)
        return jnp.concatenate([v[1:], edge], axis=0)

    def conv_cols(ub, wc, base, half, lo):
        ucol = (half * FFN_CHUNK + lo) // LANES

        def tap(k, off):
            return wc[k:k + 1, lo:lo + LANES] * ub[ucol, pl.ds(base + off, gw), :]

        def col_taps(dc):
            k = dc + 1
            v = tap(3 + k, pad)
            if has_halo:
                v = v + tap(k, 0) + tap(6 + k, 2 * gw)
            return v
        return col_taps(0) + shift_tokens(col_taps(-1), True) + shift_tokens(col_taps(1), False)

    def conv_gate(c, buf, g):
        ub = u_bufs[buf]
        wc_a = wcv_ref[:, chunk_cols(c, 0)]
        wc_g = wcv_ref[:, chunk_cols(c, 1)]
        for r in range(rows_per_gran):
            base = pl.multiple_of(g * gran_rows + r * gw, gw)
            for j in range(FFN_CHUNK // LANES):
                a = conv_cols(ub, wc_a, base, 0, j * LANES)
                gt = conv_cols(ub, wc_g, base, 1, j * LANES)
                half_g = 0.5 * gt
                p = a * half_g
                s_scr[pl.ds(base, gw), j * LANES:(j + 1) * LANES] = (p + p * jnp.tanh(half_g)).astype(_BF16)

    def down_proj(c, g):
        rows = pl.ds(pl.multiple_of(g * gran_rows, gran_rows), gran_rows)
        f = jnp.dot(s_scr[rows, :], wdn_ref[c], preferred_element_type=_F32)
        for j in range(acc_scr.shape[0]):
            acc_scr[j, rows, :] += f[:, j * LANES:(j + 1) * LANES]

    def chunk_steps(c, buf, has_next):
        def step(g, carry):
            if has_next:
                up_proj(c + 1, 1 - buf, g)
            wrap = jnp.where(g == 0, 1, 0)
            down_proj(jnp.maximum(c - wrap, 0), g - 1 + wrap * n_gran)
            conv_gate(c, buf, g)
            return carry
        lax.fori_loop(0, n_gran, step, 0)

    acc_scr[...] = jnp.zeros_like(acc_scr)
    s_scr[...] = jnp.zeros_like(s_scr)
    done = 0
    for g in range(n_gran):
        need = min(n_gran, -(-((g + 1) * up_rows - pad) // gran_rows))
        for gm in range(done, need):
            modulate_granule(gm)
        done = max(done, need)
        up_proj(0, 0, g)

    def body(k, carry):
        c = 2 * k
        chunk_steps(c, 0, True)
        chunk_steps(c + 1, 1, True)
        return carry

    n_pairs = (n_chunks - 1) // 2
    lax.fori_loop(0, n_pairs, body, 0)
    if n_chunks % 2 == 0:
        chunk_steps(n_chunks - 2, 0, True)
    chunk_steps(n_chunks - 1, (n_chunks - 1) % 2, False)
    down_proj(n_chunks - 1, n_gran - 1)

    for g in range(n_gran):
        rows = slice(g * gran_rows, (g + 1) * gran_rows)
        f = jnp.concatenate([acc_scr[j, rows, :] for j in range(acc_scr.shape[0])], axis=1)
        y = alpha * xm_ref[0, rows, :] + gate * f
        o_ref[0, rows, :] = _norm_rows(y) * lnw_ref[...] + lnb_ref[...]


def _ffn_call(x, mod, fw, *, li, ctx_row, tile, gw, alpha):
    wup, wcv, wdn, post_w, post_b = fw
    b, l, d = x.shape
    n_tiles = l // tile
    has_halo = l > gw
    rows_per_tile = tile // gw
    n_grid_rows = l // gw
    ext = tile + (2 * gw if has_halo else 0)

    x_spec = pl.BlockSpec((1, tile, d), lambda bi, i: (bi, i, 0))
    in_specs = [x_spec]
    args = [x]
    if has_halo:
        in_specs.append(pl.BlockSpec(
            (1, gw, d), lambda bi, i: (bi, jnp.maximum(i * rows_per_tile - 1, 0), 0)))
        in_specs.append(pl.BlockSpec(
            (1, gw, d), lambda bi, i: (bi, jnp.minimum((i + 1) * rows_per_tile, n_grid_rows - 1), 0)))
        args += [x, x]
    in_specs += [
        _mod_spec(li, ctx_row)(mod),
        _pick_spec(wup, li), _pick_spec(wcv, li), _pick_spec(wdn, li),
        _pick_spec(post_w, li, 1), _pick_spec(post_b, li, 1),
    ]
    args += [mod, wup, wcv, wdn, post_w, post_b]
    n_gran = max(1, tile // FFN_GRANULE)
    assert ext % (n_gran * BF16_ROWS) == 0 and rows_per_tile % n_gran == 0
    kern = functools.partial(_ffn_kernel, tile=tile, gw=gw, n_tiles=n_tiles, n_gran=n_gran,
                             has_halo=has_halo, alpha=alpha)
    return pl.pallas_call(
        kern,
        grid=(b, n_tiles),
        in_specs=in_specs,
        out_specs=x_spec,
        out_shape=jax.ShapeDtypeStruct(x.shape, _F32),
        scratch_shapes=[
            pltpu.VMEM((ext, d), _BF16),
            pltpu.VMEM((2 * FFN_CHUNK // LANES, ext, LANES), _F32),
            pltpu.VMEM((2 * FFN_CHUNK // LANES, ext, LANES), _F32),
            pltpu.VMEM((tile, FFN_CHUNK), _BF16),
            pltpu.VMEM((d // LANES, tile, LANES), _F32),
        ],
        compiler_params=pltpu.CompilerParams(
            dimension_semantics=("arbitrary", "arbitrary"),
            vmem_limit_bytes=VMEM_LIMIT),
        name="channel_mixer",
    )(*args)


def _prep_ffn_weights(w_up, conv_w, w_down):
    depth, d, two_h = w_up.shape
    n_chunks = two_h // 2 // FFN_CHUNK
    wup = w_up.astype(_BF16)
    wcv = conv_w.reshape(depth, 9, two_h)
    wdn = w_down.reshape(depth, n_chunks, FFN_CHUNK, w_down.shape[-1]).astype(_BF16)
    return wup, wcv, wdn


GLA_HEADS = 4
GLA_DK = 32
GLA_DV = 64
GLA_QK = GLA_HEADS * GLA_DK
GLA_V = GLA_HEADS * GLA_DV
GLA_RANK = 16
GLA_CHUNK = 64
GLA_GATE_NORMALIZER = 16.0
GLA_LOG_DECAY_MIN = -1.0


def _log_decay(logit):
    log_sig = jnp.minimum(logit, 0.0) - jnp.log(1.0 + jnp.exp(-jnp.abs(logit)))
    return jnp.maximum(log_sig / GLA_GATE_NORMALIZER, GLA_LOG_DECAY_MIN)


def _chunk_cumsum(x, reverse):
    n = x.shape[0]
    pos = lax.broadcasted_iota(jnp.int32, x.shape, 0) % GLA_CHUNK
    s = 1
    while s < GLA_CHUNK:
        if reverse:
            x = x + jnp.where(pos < GLA_CHUNK - s, pltpu.roll(x, n - s, axis=0), 0.0)
        else:
            x = x + jnp.where(pos >= s, pltpu.roll(x, s, axis=0), 0.0)
        s *= 2
    return x


def _state_mask():
    r = lax.broadcasted_iota(jnp.int32, (GLA_V, GLA_QK), 0) // GLA_DV
    c = lax.broadcasted_iota(jnp.int32, (GLA_V, GLA_QK), 1) // GLA_DK
    return r == c


def _stack4(a):
    return jnp.concatenate([a, a, a, a], axis=0)


def _compact_state(s):
    return (s[0:GLA_DV] + s[GLA_DV:2 * GLA_DV]) + (s[2 * GLA_DV:3 * GLA_DV] + s[3 * GLA_DV:4 * GLA_DV])


def _state_update(state, v_bf, k_dec_bf, decay_row, mask):
    kv_t = lax.dot_general(v_bf, k_dec_bf, (((0,), (0,)), ((), ())), preferred_element_type=_F32)
    return state * decay_row + jnp.where(mask, kv_t, 0.0)


def _state_kernel(x_ref, mod_ref, w_ref, wg_ref, bg_ref, s0_ref, start_ref, fin_ref, st_scr,
                  *, tile, reverse):
    i = pl.program_id(1)
    n_tiles = pl.num_programs(1)
    chunks = tile // GLA_CHUNK

    @pl.when(i == 0)
    def _():
        st_scr[...] = s0_ref[0]

    shift = mod_ref[0, 0:1, :]
    scale1 = 1.0 + mod_ref[0, 1:2, :]
    blocks = []
    for r0 in range(0, tile, MIX_BLOCK):
        h = (_norm_rows(x_ref[0, r0:r0 + MIX_BLOCK, :]) * scale1 + shift).astype(_BF16)
        blocks.append(jnp.dot(h, w_ref[...], preferred_element_type=_F32))
    z = jnp.concatenate(blocks, axis=0)
    k = z[:, 0:GLA_QK]
    v_bf = z[:, GLA_QK:GLA_QK + GLA_V].astype(_BF16)
    low = z[:, GLA_QK + GLA_V:].astype(_BF16)
    g = _log_decay(jnp.dot(low, wg_ref[...], preferred_element_type=_F32) + bg_ref[...])
    run = _chunk_cumsum(g, reverse)
    mask = _state_mask()
    state = st_scr[...]
    order = range(chunks - 1, -1, -1) if reverse else range(chunks)
    for ci in order:
        rows = slice(ci * GLA_CHUNK, (ci + 1) * GLA_CHUNK)
        edge = ci * GLA_CHUNK if reverse else (ci + 1) * GLA_CHUNK - 1
        total = run[edge:edge + 1, :]
        k_dec = (k[rows] * jnp.exp(total - run[rows])).astype(_BF16)
        start_ref[0, ci] = _compact_state(state)
        state = _state_update(state, v_bf[rows], k_dec, jnp.exp(total), mask)
    st_scr[...] = state

    @pl.when(i == n_tiles - 1)
    def _():
        fin_ref[0] = state


def _state_call(x, mod, mw, s0, *, li, ctx_row, tile, reverse):
    di = 1 if reverse else 0
    w, wg, bg = mw["w_state"], mw["wg1"], mw["bg1"]
    b, l, d = x.shape
    n_tiles = l // tile
    chunks = tile // GLA_CHUNK
    tmap = (lambda bi, i: (bi, n_tiles - 1 - i, 0)) if reverse else (lambda bi, i: (bi, i, 0))
    smap = (lambda bi, i: (bi, n_tiles - 1 - i, 0, 0)) if reverse else (lambda bi, i: (bi, i, 0, 0))
    st_spec = pl.BlockSpec((1, GLA_V, GLA_QK), lambda bi, i: (bi, 0, 0))
    kern = functools.partial(_state_kernel, tile=tile, reverse=reverse)
    return pl.pallas_call(
        kern,
        grid=(b, n_tiles),
        in_specs=[
            pl.BlockSpec((1, tile, d), tmap),
            _mod_spec(li, ctx_row)(mod),
            _pick_spec(w, li, di), _pick_spec(wg, li, di), _pick_spec(bg, li, di),
            st_spec,
        ],
        out_specs=[
            pl.BlockSpec((1, chunks, GLA_DV, GLA_QK), smap),
            st_spec,
        ],
        out_shape=[
            jax.ShapeDtypeStruct((b, l // GLA_CHUNK, GLA_DV, GLA_QK), _F32),
            jax.ShapeDtypeStruct((b, GLA_V, GLA_QK), _F32),
        ],
        scratch_shapes=[pltpu.VMEM((GLA_V, GLA_QK), _F32)],
        compiler_params=pltpu.CompilerParams(
            dimension_semantics=("arbitrary", "arbitrary"),
            vmem_limit_bytes=STATE_VMEM_LIMIT),
        name="gla_state_rev" if reverse else "gla_state_fwd",
    )(x, mod, w, wg, bg, s0)


GROUP_W = 256
COL_Q, COL_K, COL_V, COL_R, COL_LOW, COL_C, COL_B, COL_D = 0, 128, 256, 512, 768, 896, 1408, 1664
IN_COLS = 2176
HALO = 16
POOL_REACH = 8
MIX_BLOCK = 128
CONV_WIDTH = 31
SGU_HEADS = 4
SGU_HD = 64


def _gelu_tanh(x):
    return 0.5 * x * (1.0 + jnp.tanh(0.7978845608028654 * (x + 0.044715 * (x * x * x))))


def _mixer_kernel(xm_ref, xp_ref, xn_ref, mod_ref, win_ref, wg_ref, bg_ref, gnw_ref,
                  pblk_ref, pscale_ref, sguw_ref, sgub_ref, slnw_ref, slnb_ref,
                  cw_ref, cb_ref, clnw_ref, clnb_ref, wout_ref, plw_ref, plb_ref,
                  sb_ref, sf0_ref,
                  o_ref, sfin_ref,
                  z_scr, zb_scr, y_scr, run_scr, o_scr, cat_scr, st_scr,
                  *, tile, seq_len, alpha):
    i = pl.program_id(1)
    n_tiles = pl.num_programs(1)
    n_blocks = tile // MIX_BLOCK

    @pl.when(i == 0)
    def _():
        st_scr[...] = sf0_ref[0]

    shift = mod_ref[0, 0:1, :]
    scale1 = 1.0 + mod_ref[0, 1:2, :]
    gate = mod_ref[0, 2:3, :]

    def modulated(xv):
        return _norm_rows(xv) * scale1 + shift

    hp = jnp.where(i > 0, modulated(xp_ref[0]), 0.0).astype(_BF16)
    hn = jnp.where(i < n_tiles - 1, modulated(xn_ref[0]), 0.0).astype(_BF16)
    zh = jnp.dot(jnp.concatenate([hp, hn], axis=0), win_ref[:, COL_B:],
                 preferred_element_type=_F32)

    def glu(zd):
        return zd[:, :GROUP_W] * _sigmoid(zd[:, GROUP_W:])

    zb_scr[0:HALO, :] = zh[0:HALO, 0:GROUP_W]
    zb_scr[HALO + tile:, :] = zh[HALO:, 0:GROUP_W]
    def stage_conv_input(rows, zd):
        y = glu(zd)
        for j in range(GROUP_W // LANES):
            y_scr[j, rows, :] = y[:, j * LANES:(j + 1) * LANES]

    stage_conv_input(slice(0, HALO), zh[0:HALO, GROUP_W:])
    stage_conv_input(slice(HALO + tile, 2 * HALO + tile), zh[HALO:, GROUP_W:])

    def span_rows(start, size, align=SUBLANES):
        if isinstance(start, int):
            return slice(start, start + size)
        return pl.ds(pl.multiple_of(start, align), size)

    def block_rows(blk):
        return span_rows(blk * MIX_BLOCK, MIX_BLOCK, MIX_BLOCK)

    def in_proj(blk):
        rows = block_rows(blk)
        ext_rows = span_rows(HALO + blk * MIX_BLOCK, MIX_BLOCK, HALO)
        h = modulated(xm_ref[0, rows, :]).astype(_BF16)
        z = jnp.dot(h, win_ref[...], preferred_element_type=_F32)
        z_scr[rows, :] = z
        zb_scr[ext_rows, :] = z[:, COL_B:COL_B + GROUP_W]
        stage_conv_input(ext_rows, z[:, COL_D:COL_D + 2 * GROUP_W])
        low = z[:, COL_LOW:COL_LOW + LANES].astype(_BF16)
        g = _log_decay(jnp.dot(low, wg_ref[...], preferred_element_type=_F32) + bg_ref[...])
        run_scr[rows, 0:GLA_QK] = _chunk_cumsum(g[:, 0:GLA_QK], False)
        run_scr[rows, GLA_QK:] = _chunk_cumsum(g[:, GLA_QK:], True)

    lane = lax.broadcasted_iota(jnp.int32, (MIX_BLOCK, GROUP_W), 1)
    row = lax.broadcasted_iota(jnp.int32, (MIX_BLOCK, GROUP_W), 0)
    lane_group = lane // SGU_HD
    pool_half = jnp.left_shift(1, lane_group)

    def local_mixers(blk):
        r0 = blk * MIX_BLOCK
        rows = block_rows(blk)

        n = MIX_BLOCK + 2 * POOL_REACH
        xb = zb_scr[span_rows(r0 + HALO - POOL_REACH, n), :]
        p2 = xb + pltpu.roll(xb, 1, axis=0)
        p4 = pltpu.roll(p2, 1, axis=0) + pltpu.roll(p2, n - 1, axis=0)
        p8 = pltpu.roll(p4, 2, axis=0) + pltpu.roll(p4, n - 2, axis=0)
        p16 = pltpu.roll(p8, 4, axis=0) + pltpu.roll(p8, n - 4, axis=0)
        inner = slice(POOL_REACH, POOL_REACH + MIX_BLOCK)
        win = jnp.where(lane_group == 0, p2[inner],
                        jnp.where(lane_group == 1, p4[inner],
                                  jnp.where(lane_group == 2, p8[inner], p16[inner])))
        tok = i * tile + r0 + row
        count = jnp.minimum(tok + pool_half, seq_len) - jnp.maximum(tok - pool_half, 0)
        pooled = win / count.astype(_F32) - xb[inner]
        yb = jnp.dot(pooled.astype(_BF16), pblk_ref[...], preferred_element_type=_F32) * pscale_ref[...]
        cat_scr[rows, GROUP_W:2 * GROUP_W] = yb.astype(_BF16)

        gz = _gelu_tanh(z_scr[rows, COL_C:COL_C + 2 * GROUP_W])
        vn = _norm_rows(gz[:, GROUP_W:]) * slnw_ref[...] + slnb_ref[...]
        vstack = jnp.concatenate(
            [jnp.where(lane_group == hd, vn, 0.0) for hd in range(SGU_HEADS)], axis=0).astype(_BF16)
        sg = jnp.dot(sguw_ref[...], vstack, preferred_element_type=_F32) + sgub_ref[...]
        cat_scr[rows, 2 * GROUP_W:3 * GROUP_W] = (gz[:, :GROUP_W] * sg).astype(_BF16)

        span = MIX_BLOCK + SUBLANES
        halves = []
        for lo in range(0, GROUP_W, LANES):
            conv = None
            for b in range(SUBLANES):
                part = None
                for a in range(4):
                    o = SUBLANES * a + b
                    if 1 <= o <= CONV_WIDTH:
                        term = (cw_ref[o - 1:o, lo:lo + LANES]
                                * y_scr[lo // LANES, span_rows(r0 + SUBLANES * a, span), :])
                        part = term if part is None else part + term
                part = part[b:b + MIX_BLOCK]
                conv = part if conv is None else conv + part
            halves.append(conv)
        conv = jnp.concatenate(halves, axis=1)
        yd = _norm_rows(conv + cb_ref[...]) * clnw_ref[...] + clnb_ref[...]
        cat_scr[rows, 3 * GROUP_W:4 * GROUP_W] = (yd * _sigmoid(yd)).astype(_BF16)

    mask_s = _state_mask()
    vr = lax.broadcasted_iota(jnp.int32, (GLA_V, GLA_V), 0) // GLA_DV
    vc = lax.broadcasted_iota(jnp.int32, (GLA_V, GLA_V), 1) // GLA_DV
    mask_v = vr == vc
    t_id = lax.broadcasted_iota(jnp.int32, (GLA_CHUNK, GLA_V), 0)
    s_id = lax.broadcasted_iota(jnp.int32, (GLA_CHUNK, GLA_V), 1) % GLA_CHUNK
    nt_dims = (((1,), (1,)), ((), ()))

    zero = jnp.zeros((), _BF16)

    def chunk_rows(ci):
        return span_rows(ci * GLA_CHUNK, GLA_CHUNK, GLA_CHUNK)

    def scores(ci):
        r = chunk_rows(ci)
        q = z_scr[r, COL_Q:COL_Q + GLA_QK] * (GLA_DK ** -0.5)
        k = z_scr[r, COL_K:COL_K + GLA_QK]
        cf = run_scr[r, 0:GLA_QK]
        rb = run_scr[r, GLA_QK:]
        qf = (q * jnp.exp(cf)).astype(_BF16)
        kf = (k * jnp.exp(-cf)).astype(_BF16)
        qb = (q * jnp.exp(rb)).astype(_BF16)
        kb = (k * jnp.exp(-rb)).astype(_BF16)
        a_f = lax.dot_general(qf, jnp.where(mask_s, _stack4(kf), zero), nt_dims,
                              preferred_element_type=_F32)
        a_b = lax.dot_general(qb, jnp.where(mask_s, _stack4(kb), zero), nt_dims,
                              preferred_element_type=_F32)
        a = (jnp.where(s_id <= t_id, a_f, 0.0) + jnp.where(s_id >= t_id, a_b, 0.0)).astype(_BF16)
        return a, qf, qb

    def outputs(ci, a, qf, qb):
        r = chunk_rows(ci)
        k = z_scr[r, COL_K:COL_K + GLA_QK]
        v = z_scr[r, COL_V:COL_V + GLA_V].astype(_BF16)
        cf = run_scr[r, 0:GLA_QK]
        total = cf[GLA_CHUNK - 1:GLA_CHUNK, :]
        k_dec = (k * jnp.exp(total - cf)).astype(_BF16)
        o = jnp.dot(a, jnp.where(mask_v, _stack4(v), zero), preferred_element_type=_F32)
        sf = st_scr[...]
        sb = jnp.where(mask_s, _stack4(sb_ref[0, ci]), 0.0)
        o = o + lax.dot_general(qf, sf.astype(_BF16), nt_dims, preferred_element_type=_F32)
        o = o + lax.dot_general(qb, sb.astype(_BF16), nt_dims, preferred_element_type=_F32)
        o_scr[r, :] = o
        st_scr[...] = _state_update(sf, v, k_dec, jnp.exp(total), mask_s)

    head_mean = jnp.where(mask_v, 1.0 / GLA_DV, 0.0).astype(_BF16)

    def readout(blk):
        rows = block_rows(blk)
        o = o_scr[rows, :]
        sq = o * o
        sq_hi = sq.astype(_BF16)
        sq_lo = (sq - sq_hi.astype(_F32)).astype(_BF16)
        ms = (jnp.dot(sq_hi, head_mean, preferred_element_type=_F32)
              + jnp.dot(sq_lo, head_mean, preferred_element_type=_F32))
        rg = z_scr[rows, COL_R:COL_R + GROUP_W]
        ya = o * lax.rsqrt(ms + EPS) * gnw_ref[...] * (rg * _sigmoid(rg))
        cat_scr[rows, 0:GROUP_W] = ya.astype(_BF16)

    def out_proj(blk):
        rows = block_rows(blk)
        y = jnp.dot(cat_scr[rows, :], wout_ref[...], preferred_element_type=_F32)
        o_ref[0, rows, :] = (_norm_rows(alpha * xm_ref[0, rows, :] + gate * y) * plw_ref[...]
                             + plb_ref[...])

    chunks_per_block = MIX_BLOCK // GLA_CHUNK
    n_gla_chunks = tile // GLA_CHUNK
    def block_step(blk, cur, stage_ahead, finish_prev, last):
        if stage_ahead:
            in_proj(blk + 2)
        if finish_prev:
            out_proj(blk - 1)
        local_mixers(blk)
        for cj in range(chunks_per_block):
            ci = blk * chunks_per_block + cj
            nxt = None if (last and cj == chunks_per_block - 1) else scores(ci + 1)
            outputs(ci, *cur)
            cur = nxt
        readout(blk)
        return cur

    in_proj(0)
    if n_blocks > 1:
        in_proj(1)
    cur = scores(0)
    for blk in range(n_blocks):
        cur = block_step(blk, cur, blk + 2 < n_blocks, blk > 0, blk == n_blocks - 1)
    out_proj(n_blocks - 1)

    @pl.when(i == n_tiles - 1)
    def _():
        sfin_ref[0] = st_scr[...]


def _mixer_call(x, mod, mw, sb_start, sf0, *, li, ctx_row, tile, alpha):
    b, l, d = x.shape
    n_tiles = l // tile
    halo_per_tile = tile // HALO
    n_halo_blocks = l // HALO
    consts = [mw[k] for k in ("w_in", "wg2", "bg2", "gnw", "pool_blk", "pool_scale", "sgu_w", "sgu_b",
                              "sgu_ln_w", "sgu_ln_b", "cm_w", "cm_b", "cm_ln_w", "cm_ln_b", "w_out")]
    const_specs = [_pick_spec(c, li) for c in consts]
    consts += [mw["post_w"], mw["post_b"]]
    const_specs += [_pick_spec(mw["post_w"], li, 0), _pick_spec(mw["post_b"], li, 0)]
    x_spec = pl.BlockSpec((1, tile, d), lambda bi, i: (bi, i, 0))
    st_spec = pl.BlockSpec((1, GLA_V, GLA_QK), lambda bi, i: (bi, 0, 0))
    in_specs = [
        x_spec,
        pl.BlockSpec((1, HALO, d), lambda bi, i: (bi, jnp.maximum(i * halo_per_tile - 1, 0), 0)),
        pl.BlockSpec((1, HALO, d), lambda bi, i: (bi, jnp.minimum((i + 1) * halo_per_tile, n_halo_blocks - 1), 0)),
        _mod_spec(li, ctx_row)(mod),
    ] + const_specs + [
        pl.BlockSpec((1, tile // GLA_CHUNK, GLA_DV, GLA_QK), lambda bi, i: (bi, i, 0, 0)),
        st_spec,
    ]
    kern = functools.partial(_mixer_kernel, tile=tile, seq_len=l, alpha=alpha)
    return pl.pallas_call(
        kern,
        grid=(b, n_tiles),
        in_specs=in_specs,
        out_specs=[x_spec, st_spec],
        out_shape=[jax.ShapeDtypeStruct(x.shape, _F32),
                   jax.ShapeDtypeStruct((b, GLA_V, GLA_QK), _F32)],
        scratch_shapes=[
            pltpu.VMEM((tile, IN_COLS), _F32),
            pltpu.VMEM((tile + 2 * HALO, GROUP_W), _F32),
            pltpu.VMEM((GROUP_W // LANES, tile + 2 * HALO, LANES), _F32),
            pltpu.VMEM((tile, 2 * GLA_QK), _F32),
            pltpu.VMEM((tile, GLA_V), _F32),
            pltpu.VMEM((tile, 4 * GROUP_W), _BF16),
            pltpu.VMEM((GLA_V, GLA_QK), _F32),
        ],
        compiler_params=pltpu.CompilerParams(
            dimension_semantics=("arbitrary", "arbitrary"),
            vmem_limit_bytes=MIXER_VMEM_LIMIT),
        name="token_mixer",
    )(x, x, x, mod, *consts, sb_start, sf0)


def _prep_mixer_weights(w_in, gla_w_gate, gla_b_gate, gla_norm_w, pool_w, pool_scale, sgu_w, sgu_b,
                        sgu_ln_w, sgu_ln_b, cm_conv_w, cm_conv_b, cm_ln_w, cm_ln_b, w_out,
                        post_w, post_b):
    depth, d, _ = w_in.shape
    n_pool = pool_w.shape[1]
    qkvr = w_in[:, :, 0:768]
    low = w_in[:, :, 768:800]
    zb = w_in[:, :, 800:1056]
    zc = w_in[:, :, 1056:1568]
    zd = w_in[:, :, 1568:2080]
    pad = jnp.zeros((depth, d, LANES - 2 * GLA_RANK), _F32)
    w_main = jnp.concatenate([qkvr, low, pad, zc, zb, zd], axis=2).astype(_BF16)
    kv = w_in[:, :, 128:512]
    pad1 = jnp.zeros((depth, d, LANES - GLA_RANK), _F32)
    w_state = jnp.stack(
        [jnp.concatenate([kv, low[:, :, di * GLA_RANK:(di + 1) * GLA_RANK], pad1], axis=2) for di in range(2)],
        axis=1).astype(_BF16)
    wg1 = jnp.pad(gla_w_gate, ((0, 0), (0, 0), (0, LANES - GLA_RANK), (0, 0))).astype(_BF16)
    zero = jnp.zeros((depth, GLA_RANK, GLA_QK), _F32)
    wg2 = jnp.concatenate([
        jnp.concatenate([gla_w_gate[:, 0], zero], axis=2),
        jnp.concatenate([zero, gla_w_gate[:, 1]], axis=2),
        jnp.zeros((depth, LANES - 2 * GLA_RANK, 2 * GLA_QK), _F32)], axis=1).astype(_BF16)
    eye = jnp.eye(n_pool, dtype=_F32)
    pool_blk = (pool_w[:, :, :, None, :] * eye[None, :, None, :, None]).reshape(depth, GROUP_W, GROUP_W)
    row = lambda a: a[:, None, :]
    return {
        "w_in": w_main, "w_state": w_state, "wg1": wg1, "bg1": gla_b_gate[:, :, None, :],
        "wg2": wg2, "bg2": gla_b_gate.reshape(depth, 1, 2 * GLA_QK),
        "gnw": row(gla_norm_w), "pool_blk": pool_blk.astype(_BF16), "pool_scale": row(pool_scale),
        "sgu_w": sgu_w.transpose(0, 2, 1, 3).reshape(depth, MIX_BLOCK, SGU_HEADS * MIX_BLOCK).astype(_BF16),
        "sgu_b": jnp.repeat(sgu_b.transpose(0, 2, 1), SGU_HD, axis=2),
        "sgu_ln_w": row(sgu_ln_w), "sgu_ln_b": row(sgu_ln_b),
        "cm_w": jnp.pad(cm_conv_w, ((0, 0), (0, 1), (0, 0))),
        "cm_b": row(cm_conv_b), "cm_ln_w": row(cm_ln_w), "cm_ln_b": row(cm_ln_b),
        "w_out": w_out.astype(_BF16),
        "post_w": post_w[:, :, None, :], "post_b": post_b[:, :, None, :],
    }


def _mod_kernel(c_ref, w_ref, b_ref, o_ref):
    cv = c_ref[...]

    def split(a):
        hi = a.astype(_BF16)
        return hi, (a - hi.astype(_F32)).astype(_BF16)

    a_hi, a_lo = split(cv * _sigmoid(cv))
    w_hi, w_lo = split(w_ref[0])
    dot = functools.partial(jnp.dot, preferred_element_type=_F32)
    o_ref[0] = (dot(a_hi, w_lo) + dot(a_lo, w_hi)) + dot(a_hi, w_hi) + b_ref[0]


def _mod_call(cond, w_mod, b_mod):
    depth, d, six_d = w_mod.shape
    rows = cond.shape[0]
    return pl.pallas_call(
        _mod_kernel,
        grid=(depth, six_d // d),
        in_specs=[
            pl.BlockSpec((rows, d), lambda li, j: (0, 0)),
            pl.BlockSpec((1, d, d), lambda li, j: (li, 0, j)),
            pl.BlockSpec((1, 1, d), lambda li, j: (li, 0, j)),
        ],
        out_specs=pl.BlockSpec((1, rows, d), lambda li, j: (li, 0, j)),
        out_shape=jax.ShapeDtypeStruct((depth, rows, six_d), _F32),
        compiler_params=pltpu.CompilerParams(
            dimension_semantics=("arbitrary", "arbitrary"),
            vmem_limit_bytes=STATE_VMEM_LIMIT),
        name="adaln_modulation",
    )(cond, w_mod, b_mod.reshape(depth, 1, six_d))


def _pick_tile(length, target):
    return min(length, target)


def kernel(x, c, ctx, c_ctx, w_mod, b_mod, w_in, gla_w_gate, gla_b_gate, gla_norm_w, pool_w, pool_scale, sgu_w, sgu_b, sgu_ln_w, sgu_ln_b, cm_conv_w, cm_conv_b, cm_ln_w, cm_ln_b, w_out, ffn_w_up, ffn_conv_w, ffn_w_down, post_ln_w, post_ln_b):
    batch, seq, d = x.shape
    ctx_len = ctx.shape[1]
    depth = w_mod.shape[0]
    alpha = (2 * depth) ** 0.25

    cond_rows = -(-(batch + 1) // SUBLANES) * SUBLANES
    cond = jnp.concatenate([c, c_ctx[None, :], jnp.zeros((cond_rows - batch - 1, d), _F32)], axis=0)
    mod = _mod_call(cond, w_mod, b_mod).reshape(depth, cond_rows, 6, d)

    lat_mix_tile = _pick_tile(seq, MIX_TILE)
    ctx_mix_tile = _pick_tile(ctx_len, MIX_TILE)
    lat_ffn_tile = _pick_tile(seq, FFN_TILE)
    zero_state = jnp.zeros((batch, GLA_V, GLA_QK), _F32)

    mw = _prep_mixer_weights(w_in, gla_w_gate, gla_b_gate, gla_norm_w, pool_w, pool_scale, sgu_w, sgu_b,
                             sgu_ln_w, sgu_ln_b, cm_conv_w, cm_conv_b, cm_ln_w, cm_ln_b, w_out,
                             post_ln_w, post_ln_b)
    fw = _prep_ffn_weights(ffn_w_up, ffn_conv_w, ffn_w_down) + (mw["post_w"], mw["post_b"])
    lat = dict(ctx_row=None)
    cx = dict(ctx_row=batch)

    for li in range(depth):
        with_ctx_out = li < depth - 1
        ctx_sb, ctx_sb_fin = _state_call(ctx, mod, mw, zero_state, li=li, tile=ctx_mix_tile,
                                         reverse=True, **cx)
        if with_ctx_out:
            ctx_mixed, ctx_sf_fin = _mixer_call(ctx, mod, mw, ctx_sb, zero_state, li=li,
                                                tile=ctx_mix_tile, alpha=alpha, **cx)
        else:
            _, ctx_sf_fin = _state_call(ctx, mod, mw, zero_state, li=li, tile=ctx_mix_tile,
                                        reverse=False, **cx)

        lat_sb, _ = _state_call(x, mod, mw, ctx_sb_fin, li=li, tile=lat_mix_tile, reverse=True, **lat)
        x, _ = _mixer_call(x, mod, mw, lat_sb, ctx_sf_fin, li=li, tile=lat_mix_tile, alpha=alpha, **lat)
        x = _ffn_call(x, mod, fw, li=li, tile=lat_ffn_tile, gw=GRID_W, alpha=alpha, **lat)
        if with_ctx_out:
            ctx = _ffn_call(ctx_mixed, mod, fw, li=li, tile=ctx_len, gw=ctx_len, alpha=alpha, **cx)
    return x
```
